```python
import jax, jax.numpy as jnp
from jax import lax
import numpy as np

D_MODEL = 1024
BATCH = 8
SEQ = 8192
DEPTH = 2

EPS = 1e-6
CONV_WIDTH = 4
CONV_PAD_LEFT = 1

LRU_WIDTH = D_MODEL
LRU_HEADS = 8
LRU_BLOCK = LRU_WIDTH // LRU_HEADS
LRU_C = 8.0

SSD_WIDTH = D_MODEL
SSD_HEADDIM = 64
SSD_HEADS = SSD_WIDTH // SSD_HEADDIM
SSD_GROUPS = 4
SSD_STATE = 128
SSD_CHUNK = 128
SSD_GN = SSD_GROUPS * SSD_STATE
SSD_CONV_CH = SSD_WIDTH + 2 * SSD_GN

MIX_WIDTH = LRU_WIDTH + SSD_WIDTH
IN_PROJ = 2 * LRU_WIDTH + SSD_WIDTH + SSD_CONV_CH + 2 * SSD_HEADS

N_EXPERTS = 16
CAPACITY_FACTOR = 2
D_FF_EXPERT = 2048

kernel_name = "hybrid_rglru_ssd_ec_moe_encoder"


def rms_norm(x, g):
    xf = x.astype(jnp.float32)
    y = xf * lax.rsqrt(jnp.mean(xf * xf, axis=-1, keepdims=True) + EPS)
    return (y * g.astype(jnp.float32)).astype(x.dtype)


def depthwise_conv_centred(u, w, b):
    k, c = w.shape
    out = lax.conv_general_dilated(
        u, w[:, None, :].astype(u.dtype), window_strides=(1,),
        padding=[(CONV_PAD_LEFT, k - 1 - CONV_PAD_LEFT)],
        dimension_numbers=('NWC', 'WIO', 'NWC'), feature_group_count=c)
    return out + b.astype(u.dtype)


def flip_seq(a):
    return jnp.flip(a, axis=1)


def _lin_combine(left, right):
    a_l, b_l = left
    a_r, b_r = right
    return a_l * a_r, a_r * b_l + b_r


def linear_scan(a, u):
    _, h = lax.associative_scan(_lin_combine, (a, u), axis=1)
    return h


def rglru_mixer(xr, gate, conv_w, conv_b, wa, ba, wi, bi, lam):
    b, s, w = xr.shape
    xc = depthwise_conv_centred(xr, conv_w, conv_b).astype(jnp.float32)
    xb = xc.reshape(b, s, LRU_HEADS, LRU_BLOCK)
    h_sum = jnp.zeros_like(xc)
    for d in range(2):
        r = jax.nn.sigmoid(jnp.einsum('bshi,hij->bshj', xb, wa[d].astype(jnp.float32)).reshape(b, s, w) + ba[d].astype(jnp.float32))
        i = jax.nn.sigmoid(jnp.einsum('bshi,hij->bshj', xb, wi[d].astype(jnp.float32)).reshape(b, s, w) + bi[d].astype(jnp.float32))
        log_a = -LRU_C * r * jax.nn.softplus(-lam[d].astype(jnp.float32))
        a = jnp.exp(log_a)
        u = jnp.sqrt(-jnp.expm1(2.0 * log_a)) * (i * xc)
        if d == 0:
            h_sum = h_sum + linear_scan(a, u)
        else:
            h_sum = h_sum + flip_seq(linear_scan(flip_seq(a), flip_seq(u)))
    return h_sum * jax.nn.gelu(gate.astype(jnp.float32))


def ssd_chunked(x, dt, A, Bm, Cm):
    b, s, h, p = x.shape
    g, n = Bm.shape[2], Bm.shape[3]
    e = h // g
    L = SSD_CHUNK
    nc = s // L
    xc = x.reshape(b, nc, L, g, e, p)
    dtc = dt.reshape(b, nc, L, g, e)
    Bc = Bm.reshape(b, nc, L, g, n)
    Cc = Cm.reshape(b, nc, L, g, n)
    cum = jnp.cumsum(dtc * A.reshape(g, e), axis=2)
    lower = jnp.tril(jnp.ones((L, L), dtype=bool))[None, None, :, :, None, None]
    seg = cum[:, :, :, None] - cum[:, :, None, :]
    decay = jnp.exp(jnp.where(lower, seg, -jnp.inf))
    cb = jnp.einsum('bclgn,bcsgn->bclsg', Cc, Bc)
    y_diag = jnp.einsum('bclsg,bclsge,bcsge,bcsgep->bclgep', cb, decay, dtc, xc)
    decay_to_end = jnp.exp(cum[:, :, -1:] - cum)
    states = jnp.einsum('bclgn,bclge,bclgep->bcgepn', Bc, decay_to_end * dtc, xc)
    chunk_decay = jnp.exp(cum[:, :, -1])

    def step(carry, inp):
        st, dec = inp
        return carry * dec[..., None, None] + st, carry

    init = jnp.zeros((b, g, e, p, n), x.dtype)
    _, prev = lax.scan(step, init, (jnp.moveaxis(states, 1, 0), jnp.moveaxis(chunk_decay, 1, 0)))
    prev = jnp.moveaxis(prev, 0, 1)
    y_off = jnp.einsum('bclgn,bcgepn,bclge->bclgep', Cc, prev, jnp.exp(cum))
    return (y_diag + y_off).reshape(b, s, h, p)


def ssd_mixer(z, xbc, dt_raw, conv_w, conv_b, a_log, dt_bias, d_skip, norm_g):
    b, s, _ = z.shape
    xbc = jax.nn.silu(depthwise_conv_centred(xbc, conv_w, conv_b)).astype(jnp.float32)
    xs = xbc[..., :SSD_WIDTH].reshape(b, s, SSD_HEADS, SSD_HEADDIM)
    Bm = xbc[..., SSD_WIDTH:SSD_WIDTH + SSD_GN].reshape(b, s, SSD_GROUPS, SSD_STATE)
    Cm = xbc[..., SSD_WIDTH + SSD_GN:].reshape(b, s, SSD_GROUPS, SSD_STATE)
    dt_raw = dt_raw.astype(jnp.float32).reshape(b, s, 2, SSD_HEADS)
    y = d_skip.astype(jnp.float32)[:, None] * xs
    for d in range(2):
        dt = jax.nn.softplus(dt_raw[:, :, d] + dt_bias[d].astype(jnp.float32))
        A = -jnp.exp(a_log[d].astype(jnp.float32))
        if d == 0:
            y = y + ssd_chunked(xs, dt, A, Bm, Cm)
        else:
            y = y + flip_seq(ssd_chunked(flip_seq(xs), flip_seq(dt), A, flip_seq(Bm), flip_seq(Cm)))
    y = y.reshape(b, s, SSD_WIDTH) * jax.nn.silu(z.astype(jnp.float32))
    yg = y.reshape(b, s, SSD_GROUPS, SSD_WIDTH // SSD_GROUPS)
    yg = yg * lax.rsqrt(jnp.mean(yg * yg, axis=-1, keepdims=True) + EPS)
    return yg.reshape(b, s, SSD_WIDTH) * norm_g.astype(jnp.float32)


def expert_choice_ffn(x, w_router, w_gate, w_up, w_down):
    b, s, d = x.shape
    cap = max(1, CAPACITY_FACTOR * s // N_EXPERTS)
    logits = jnp.einsum('bsd,de->bse', x, w_router).astype(jnp.float32)
    affinity = jax.nn.softmax(logits, axis=-1)
    gates, idx = lax.top_k(jnp.swapaxes(affinity, 1, 2), cap)
    xg = jax.vmap(lambda xb, ib: xb[ib])(x, idx)
    hid = jax.nn.silu(jnp.einsum('becd,edf->becf', xg, w_gate)) * jnp.einsum('becd,edf->becf', xg, w_up)
    yg = jnp.einsum('becf,efd->becd', hid, w_down) * gates[..., None].astype(x.dtype)
    out = jax.vmap(lambda ib, yb: jnp.zeros((s, d), yb.dtype).at[ib.reshape(-1)].add(yb.reshape(-1, d)))(idx, yg)
    return out.astype(x.dtype)


def setup_inputs(seed: int = 0) -> dict:
    key = jax.random.key(seed)
    ks = jax.random.split(key, 24)
    f32 = jnp.float32
    nrm = lambda k, shape, scale: jax.random.normal(k, shape, f32) * scale
    x = jax.random.normal(ks[0], (BATCH, SEQ, D_MODEL), f32)
    norm_mix = 1.0 + nrm(ks[1], (DEPTH, D_MODEL), 0.02)
    w_in = nrm(ks[2], (DEPTH, D_MODEL, IN_PROJ), D_MODEL ** -0.5)
    lru_conv_w = nrm(ks[3], (DEPTH, CONV_WIDTH, LRU_WIDTH), CONV_WIDTH ** -0.5)
    lru_conv_b = nrm(ks[4], (DEPTH, LRU_WIDTH), 0.02)
    lru_wa = nrm(ks[5], (DEPTH, 2, LRU_HEADS, LRU_BLOCK, LRU_BLOCK), LRU_BLOCK ** -0.5)
    lru_ba = nrm(ks[6], (DEPTH, 2, LRU_WIDTH), 0.02)
    lru_wi = nrm(ks[7], (DEPTH, 2, LRU_HEADS, LRU_BLOCK, LRU_BLOCK), LRU_BLOCK ** -0.5)
    lru_bi = nrm(ks[8], (DEPTH, 2, LRU_WIDTH), 0.02)
    a0 = jax.random.uniform(ks[9], (DEPTH, 2, LRU_WIDTH), f32, 0.9, 0.999)
    root = a0 ** (1.0 / LRU_C)
    lru_lambda = jnp.log(root) - jnp.log1p(-root)
    ssd_conv_w = nrm(ks[10], (DEPTH, CONV_WIDTH, SSD_CONV_CH), CONV_WIDTH ** -0.5)
    ssd_conv_b = nrm(ks[11], (DEPTH, SSD_CONV_CH), 0.02)
    ssd_a_log = jnp.log(jax.random.uniform(ks[12], (DEPTH, 2, SSD_HEADS), f32, 1.0, 16.0))
    dt0 = jnp.exp(jax.random.uniform(ks[13], (DEPTH, 2, SSD_HEADS), f32, float(np.log(1e-3)), float(np.log(1e-1))))
    ssd_dt_bias = dt0 + jnp.log(-jnp.expm1(-dt0))
    ssd_d = 1.0 + nrm(ks[14], (DEPTH, SSD_HEADS), 0.1)
    ssd_norm = 1.0 + nrm(ks[15], (DEPTH, SSD_WIDTH), 0.02)
    w_out = nrm(ks[16], (DEPTH, MIX_WIDTH, D_MODEL), MIX_WIDTH ** -0.5)
    norm_ffn = 1.0 + nrm(ks[17], (DEPTH, D_MODEL), 0.02)
    w_router = nrm(ks[18], (DEPTH, D_MODEL, N_EXPERTS), D_MODEL ** -0.5)
    w_gate = nrm(ks[19], (DEPTH, N_EXPERTS, D_MODEL, D_FF_EXPERT), D_MODEL ** -0.5)
    w_up = nrm(ks[20], (DEPTH, N_EXPERTS, D_MODEL, D_FF_EXPERT), D_MODEL ** -0.5)
    w_down = nrm(ks[21], (DEPTH, N_EXPERTS, D_FF_EXPERT, D_MODEL), D_FF_EXPERT ** -0.5)
    norm_final = 1.0 + nrm(ks[22], (D_MODEL,), 0.02)
    return {"x": x, "norm_mix": norm_mix, "w_in": w_in,
            "lru_conv_w": lru_conv_w, "lru_conv_b": lru_conv_b,
            "lru_wa": lru_wa, "lru_ba": lru_ba, "lru_wi": lru_wi, "lru_bi": lru_bi,
            "lru_lambda": lru_lambda,
            "ssd_conv_w": ssd_conv_w, "ssd_conv_b": ssd_conv_b,
            "ssd_a_log": ssd_a_log, "ssd_dt_bias": ssd_dt_bias, "ssd_d": ssd_d,
            "ssd_norm": ssd_norm, "w_out": w_out, "norm_ffn": norm_ffn,
            "w_router": w_router, "w_gate": w_gate, "w_up": w_up, "w_down": w_down,
            "norm_final": norm_final}


def reference(x, norm_mix, w_in, lru_conv_w, lru_conv_b, lru_wa, lru_ba, lru_wi, lru_bi,
              lru_lambda, ssd_conv_w, ssd_conv_b, ssd_a_log, ssd_dt_bias, ssd_d, ssd_norm,
              w_out, norm_ffn, w_router, w_gate, w_up, w_down, norm_final):
    split_points = [LRU_WIDTH, 2 * LRU_WIDTH, 2 * LRU_WIDTH + SSD_WIDTH,
                    2 * LRU_WIDTH + SSD_WIDTH + SSD_CONV_CH]
    for l in range(DEPTH):
        hn = rms_norm(x, norm_mix[l])
        proj = jnp.einsum('bsd,dk->bsk', hn, w_in[l])
        xr, gate, z, xbc, dt_raw = jnp.split(proj, split_points, axis=-1)
        y_lru = rglru_mixer(xr, gate, lru_conv_w[l], lru_conv_b[l], lru_wa[l], lru_ba[l],
                            lru_wi[l], lru_bi[l], lru_lambda[l])
        y_ssd = ssd_mixer(z, xbc, dt_raw, ssd_conv_w[l], ssd_conv_b[l], ssd_a_log[l],
                          ssd_dt_bias[l], ssd_d[l], ssd_norm[l])
        y = jnp.concatenate([y_lru, y_ssd], axis=-1).astype(x.dtype)
        x = x + jnp.einsum('bsk,kd->bsd', y, w_out[l])
        x = x + expert_choice_ffn(rms_norm(x, norm_ffn[l]), w_router[l], w_gate[l], w_up[l], w_down[l])
    return rms_norm(x, norm_final)
```

```python
import functools

import jax
import jax.numpy as jnp
from jax import lax
from jax.experimental import pallas as pl
from jax.experimental.pallas import tpu as pltpu

f32 = jnp.float32
bf16 = jnp.bfloat16
i32 = jnp.int32

D_MODEL = 1024
EPS = 1e-6
CONV_WIDTH = 4
LRU_WIDTH = 1024
LRU_HEADS = 8
LRU_BLOCK = 128
LRU_C = 8.0
SSD_WIDTH = 1024
SSD_HEADDIM = 64
SSD_HEADS = 16
SSD_GROUPS = 4
SSD_STATE = 128
SSD_GN = SSD_GROUPS * SSD_STATE
SSD_CONV_CH = SSD_WIDTH + 2 * SSD_GN
GROUP_WIDTH = SSD_WIDTH // SSD_GROUPS
N_EXPERTS = 16
CAPACITY_FACTOR = 2
D_FF = 2048

LANES = 128
SUBLANES = 8
HALO = SUBLANES
VMEM_LIMIT = 56 * 1024 * 1024


def _pick(n, target):
    if n <= target:
        return n
    t = target
    while t >= LANES:
        if n % t == 0:
            return t
        t -= LANES
    return n


def _params(sem, vmem=None):
    return pltpu.CompilerParams(dimension_semantics=sem, vmem_limit_bytes=vmem)


def _norm_matmul_body(x_ref, g_ref, w_ref, o_ref, hn_ref):
    @pl.when(pl.program_id(1) == 0)
    def _():
        x = x_ref[...]
        ms = jnp.mean(x * x, axis=-1, keepdims=True)
        hn_ref[...] = (x * lax.rsqrt(ms + EPS) * g_ref[...]).astype(bf16)

    o_ref[...] = jnp.dot(hn_ref[...], w_ref[...], preferred_element_type=f32)


def _norm_matmul(x2d, g, w_bf16):
    t, d = x2d.shape
    n = w_bf16.shape[1]
    tm = _pick(t, 1024)
    tn = _pick(n, 512)
    return pl.pallas_call(
        _norm_matmul_body,
        grid=(t // tm, n // tn),
        in_specs=[
            pl.BlockSpec((tm, d), lambda i, j: (i, 0)),
            pl.BlockSpec((1, d), lambda i, j: (0, 0)),
            pl.BlockSpec((d, tn), lambda i, j: (0, j)),
        ],
        out_specs=pl.BlockSpec((tm, tn), lambda i, j: (i, j)),
        out_shape=jax.ShapeDtypeStruct((t, n), f32),
        scratch_shapes=[pltpu.VMEM((tm, d), bf16)],
        compiler_params=_params(("parallel", "arbitrary")),
        name="norm_matmul",
    )(x2d, g.reshape(1, d), w_bf16)


def _chunk_index(c, nc, reverse):
    return (nc - 1 - c) if reverse else c


def _halo_specs(tc, width, nc, s, reverse):
    per = tc // HALO
    last = s // HALO - 1

    def cur(b, c):
        return (b, _chunk_index(c, nc, reverse), 0)

    def prev(b, c):
        return (b, jnp.maximum(_chunk_index(c, nc, reverse) * per - 1, 0), 0)

    def nxt(b, c):
        return (b, jnp.minimum((_chunk_index(c, nc, reverse) + 1) * per, last), 0)

    return [
        pl.BlockSpec((1, HALO, width), prev),
        pl.BlockSpec((1, tc, width), cur),
        pl.BlockSpec((1, HALO, width), nxt),
    ]


def _conv_centred(prev_ref, x_ref, next_ref, cw_ref, cb_ref, ci, nc):
    tc = x_ref.shape[1]
    prev = jnp.where(ci == 0, 0.0, prev_ref[0])
    nxt = jnp.where(ci == nc - 1, 0.0, next_ref[0])
    xe = jnp.concatenate([prev, x_ref[0], nxt], axis=0)
    cw = cw_ref[...]
    acc = cb_ref[...] + xe[HALO - 1:HALO - 1 + tc] * cw[0:1]
    for k in range(1, CONV_WIDTH):
        acc = acc + xe[HALO - 1 + k:HALO - 1 + k + tc] * cw[k:k + 1]
    return acc


def _scan_rows8(a, u, reverse):
    r = lax.broadcasted_iota(i32, a.shape, 1)
    for k in (1, 2, 4):
        shift = (SUBLANES - k) if reverse else k
        valid = (r < SUBLANES - k) if reverse else (r >= k)
        a_s = pltpu.roll(a, shift, 1)
        u_s = pltpu.roll(u, shift, 1)
        u = jnp.where(valid, a * u_s + u, u)
        a = jnp.where(valid, a * a_s, a)
    return a, u


def _lru_body(prev_ref, x_ref, next_ref, cw_ref, cb_ref, w_ref, ba_ref, bi_ref, lam_ref,
              o_ref, carry_ref, a_scr, u_scr, *, reverse, nc):
    c = pl.program_id(1)
    ci = _chunk_index(c, nc, reverse)
    tc = x_ref.shape[1]
    groups = tc // SUBLANES

    @pl.when(c == 0)
    def _():
        carry_ref[...] = jnp.zeros_like(carry_ref)

    xc = _conv_centred(prev_ref, x_ref, next_ref, cw_ref, cb_ref, ci, nc)
    sp = jax.nn.softplus(-lam_ref[...])
    for h in range(LRU_HEADS):
        sl = slice(h * LRU_BLOCK, (h + 1) * LRU_BLOCK)
        xh = xc[:, sl]
        pre = jnp.dot(xh.astype(bf16), w_ref[h], preferred_element_type=f32)
        r = jax.nn.sigmoid(pre[:, :LRU_BLOCK] + ba_ref[:, sl])
        gi = jax.nn.sigmoid(pre[:, LRU_BLOCK:] + bi_ref[:, sl])
        log_a = (-LRU_C) * r * sp[:, sl]
        a = jnp.exp(log_a)
        u = jnp.sqrt(1.0 - a * a) * (gi * xh)
        a3, u3 = _scan_rows8(a.reshape(groups, SUBLANES, LRU_BLOCK),
                             u.reshape(groups, SUBLANES, LRU_BLOCK), reverse)
        a_scr[:, sl] = a3.reshape(tc, LRU_BLOCK)
        u_scr[:, sl] = u3.reshape(tc, LRU_BLOCK)

    def step(g, carry):
        gi_ = (groups - 1 - g) if reverse else g
        row = pl.multiple_of(gi_ * SUBLANES, SUBLANES)
        hblk = u_scr[pl.ds(row, SUBLANES), :] + a_scr[pl.ds(row, SUBLANES), :] * carry
        o_ref[0, pl.ds(row, SUBLANES), :] = hblk
        return hblk[0:1] if reverse else hblk[SUBLANES - 1:SUBLANES]

    carry_ref[...] = lax.fori_loop(0, groups, step, carry_ref[...], unroll=4)


def _lru_scan(xr, conv_w, conv_b, w_gate, ba, bi, lam, reverse):
    b, s, w = xr.shape
    tc = _pick(s, 256)
    nc = s // tc
    full = lambda shape: pl.BlockSpec(shape, lambda b_, c_: (0,) * len(shape))
    return pl.pallas_call(
        functools.partial(_lru_body, reverse=reverse, nc=nc),
        grid=(b, nc),
        in_specs=_halo_specs(tc, w, nc, s, reverse) + [
            full((CONV_WIDTH, w)), full((1, w)),
            full((LRU_HEADS, LRU_BLOCK, 2 * LRU_BLOCK)),
            full((1, w)), full((1, w)), full((1, w)),
        ],
        out_specs=pl.BlockSpec((1, tc, w), lambda b_, c_: (b_, _chunk_index(c_, nc, reverse), 0)),
        out_shape=jax.ShapeDtypeStruct((b, s, w), f32),
        scratch_shapes=[pltpu.VMEM((1, w), f32), pltpu.VMEM((tc, w), f32), pltpu.VMEM((tc, w), f32)],
        compiler_params=_params(("parallel", "arbitrary")),
        name="lru_bwd" if reverse else "lru_fwd",
    )(xr, xr, xr, conv_w, conv_b.reshape(1, w), w_gate, ba.reshape(1, w), bi.reshape(1, w),
      lam.reshape(1, w))


def _expand_heads(arr, base):
    rows = arr.shape[0]
    lane = lax.broadcasted_iota(i32, (rows, LANES), 1)
    tiles = []
    for k in range(SSD_HEADS // 2):
        c0 = arr[:, base + 2 * k:base + 2 * k + 1]
        c1 = arr[:, base + 2 * k + 1:base + 2 * k + 2]
        tiles.append(jnp.where(lane < SSD_HEADDIM, c0, c1))
    return jnp.concatenate(tiles, axis=1)


def _cumsum_rows(x, reverse):
    n = x.shape[0]
    r = lax.broadcasted_iota(i32, x.shape, 0)
    k = 1
    while k < n:
        if reverse:
            x = x + jnp.where(r < n - k, pltpu.roll(x, n - k, 0), 0.0)
        else:
            x = x + jnp.where(r >= k, pltpu.roll(x, k, 0), 0.0)
        k *= 2
    return x


def _ssd_body(prev_ref, x_ref, next_ref, dt_ref, cw_ref, cb_ref, alog_ref, dtb_ref, dskip_ref,
              o_ref, state_ref, *, reverse, nc):
    c = pl.program_id(1)
    ci = _chunk_index(c, nc, reverse)
    L = x_ref.shape[1]
    base = SSD_HEADS if reverse else 0

    @pl.when(c == 0)
    def _():
        state_ref[...] = jnp.zeros_like(state_ref)

    xact = jax.nn.silu(_conv_centred(prev_ref, x_ref, next_ref, cw_ref, cb_ref, ci, nc))
    xs = xact[:, :SSD_WIDTH]

    dt = jax.nn.softplus(dt_ref[0] + dtb_ref[...])
    d_a = dt * (-jnp.exp(alog_ref[...]))
    cum = _cumsum_rows(d_a, reverse)
    edge = cum[0:1] if reverse else cum[L - 1:L]
    cum_t = cum.T

    dtx = _expand_heads(dt, base) * xs
    e_cum = _expand_heads(jnp.exp(cum), base)
    e_end = _expand_heads(jnp.exp(edge - cum), base)
    e_edge = _expand_heads(jnp.exp(edge), base)
    w_all = (e_end * dtx).astype(bf16)
    dtx_b = dtx.astype(bf16)

    li = lax.broadcasted_iota(i32, (L, L), 0)
    si = lax.broadcasted_iota(i32, (L, L), 1)
    tri = (si >= li) if reverse else (li >= si)
    lane = lax.broadcasted_iota(i32, (L, LANES), 1)
    lo_half = lane < SSD_HEADDIM

    outs = []
    for g in range(SSD_GROUPS):
        bsl = slice(SSD_WIDTH + g * SSD_STATE, SSD_WIDTH + (g + 1) * SSD_STATE)
        csl = slice(SSD_WIDTH + SSD_GN + g * SSD_STATE, SSD_WIDTH + SSD_GN + (g + 1) * SSD_STATE)
        gsl = slice(g * GROUP_WIDTH, (g + 1) * GROUP_WIDTH)
        bm = xact[:, bsl].astype(bf16)
        cm = xact[:, csl].astype(bf16)
        cb = lax.dot_general(cm, bm, (((1,), (1,)), ((), ())), preferred_element_type=f32)
        s_old = state_ref[g]
        y_off = jnp.dot(cm, s_old.astype(bf16), preferred_element_type=f32) * e_cum[:, gsl]
        tiles = []
        for p in range(2):
            acc = None
            for q in range(2):
                j = base + g * 4 + 2 * p + q
                seg = cum[:, j:j + 1] - cum_t[j:j + 1, :]
                decay = jnp.exp(jnp.where(tri, seg, -jnp.inf))
                m = (cb * decay).astype(bf16)
                tsl = slice(g * GROUP_WIDTH + p * LANES, g * GROUP_WIDTH + (p + 1) * LANES)
                rhs = jnp.where(lo_half if q == 0 else jnp.logical_not(lo_half), dtx_b[:, tsl], 0.0)
                part = jnp.dot(m, rhs.astype(bf16), preferred_element_type=f32)
                acc = part if acc is None else acc + part
            tiles.append(acc)
        outs.append(jnp.concatenate(tiles, axis=1) + y_off)
        upd = lax.dot_general(bm, w_all[:, gsl], (((0,), (0,)), ((), ())), preferred_element_type=f32)
        state_ref[g] = s_old * e_edge[:, gsl] + upd
    y = jnp.concatenate(outs, axis=1)
    if not reverse:
        y = y + _expand_heads(dskip_ref[...], 0) * xs
    o_ref[0] = y


def _ssd_scan(xbc, dt_pad, conv_w, conv_b, alog_pad, dtb_pad, dskip_pad, reverse):
    b, s, w = xbc.shape
    L = _pick(s, 128)
    nc = s // L
    full = lambda shape: pl.BlockSpec(shape, lambda b_, c_: (0,) * len(shape))
    return pl.pallas_call(
        functools.partial(_ssd_body, reverse=reverse, nc=nc),
        grid=(b, nc),
        in_specs=_halo_specs(L, w, nc, s, reverse) + [
            pl.BlockSpec((1, L, LANES), lambda b_, c_: (b_, _chunk_index(c_, nc, reverse), 0)),
            full((CONV_WIDTH, w)), full((1, w)),
            full((1, LANES)), full((1, LANES)), full((1, LANES)),
        ],
        out_specs=pl.BlockSpec((1, L, SSD_WIDTH), lambda b_, c_: (b_, _chunk_index(c_, nc, reverse), 0)),
        out_shape=jax.ShapeDtypeStruct((b, s, SSD_WIDTH), f32),
        scratch_shapes=[pltpu.VMEM((SSD_GROUPS, SSD_STATE, GROUP_WIDTH), f32)],
        compiler_params=_params(("parallel", "arbitrary")),
        name="ssd_bwd" if reverse else "ssd_fwd",
    )(xbc, xbc, xbc, dt_pad, conv_w, conv_b.reshape(1, w), alog_pad, dtb_pad, dskip_pad)


def _rms(x, g):
    ms = jnp.mean(x * x, axis=-1, keepdims=True)
    return x * lax.rsqrt(ms + EPS) * g


def _mix_out_body(x_ref, hf_ref, hb_ref, gate_ref, yf_ref, yb_ref, z_ref, gn_ref, wo_ref,
                  gf_ref, wrh_ref, wrl_ref, xo_ref, hn_ref, lg_ref):
    y_lru = (hf_ref[...] + hb_ref[...]) * jax.nn.gelu(gate_ref[...])
    y = (yf_ref[...] + yb_ref[...]) * jax.nn.silu(z_ref[...])
    parts = []
    for g in range(SSD_GROUPS):
        yg = y[:, g * GROUP_WIDTH:(g + 1) * GROUP_WIDTH]
        ms = jnp.mean(yg * yg, axis=-1, keepdims=True)
        parts.append(yg * lax.rsqrt(ms + EPS))
    y_ssd = jnp.concatenate(parts, axis=1) * gn_ref[...]
    mix = jnp.concatenate([y_lru, y_ssd], axis=1).astype(bf16)
    xn = x_ref[...] + jnp.dot(mix, wo_ref[...], preferred_element_type=f32)
    xo_ref[...] = xn
    hn = _rms(xn, gf_ref[...])
    hn_ref[...] = hn
    h_hi = hn.astype(bf16)
    h_lo = (hn - h_hi.astype(f32)).astype(bf16)
    lg = jnp.dot(h_hi, wrh_ref[...], preferred_element_type=f32)
    lg = lg + jnp.dot(h_lo, wrh_ref[...], preferred_element_type=f32)
    lg = lg + jnp.dot(h_hi, wrl_ref[...], preferred_element_type=f32)
    lg_ref[...] = lg


def _mix_out(x2d, hf, hb, gate, yf, yb, z, ssd_norm, w_out_bf16, norm_ffn, wr_hi, wr_lo):
    t, d = x2d.shape
    tm = _pick(t, 256)
    row = lambda w: pl.BlockSpec((tm, w), lambda i: (i, 0))
    full = lambda shape: pl.BlockSpec(shape, lambda i: (0,) * len(shape))
    return pl.pallas_call(
        _mix_out_body,
        grid=(t // tm,),
        in_specs=[row(d)] + [row(LRU_WIDTH)] * 3 + [row(SSD_WIDTH)] * 3 + [
            full((1, SSD_WIDTH)), full((LRU_WIDTH + SSD_WIDTH, d)), full((1, d)),
            full((d, LANES)), full((d, LANES)),
        ],
        out_specs=[row(d), row(d), row(LANES)],
        out_shape=[jax.ShapeDtypeStruct((t, d), f32), jax.ShapeDtypeStruct((t, d), f32),
                   jax.ShapeDtypeStruct((t, LANES), f32)],
        compiler_params=_params(("parallel",), VMEM_LIMIT),
        name="mix_out",
    )(x2d, hf, hb, gate, yf, yb, z, ssd_norm.reshape(1, -1), w_out_bf16, norm_ffn.reshape(1, d),
      wr_hi, wr_lo)


def _route_body(lg_ref, idx_ref, gate_ref, pos_scr, vals_scr, *, cap):
    s = lg_ref.shape[1]
    nt = s // LANES
    nq = cap // LANES
    lt = lg_ref[0].T[:N_EXPERTS]
    mx = jnp.max(lt, axis=0, keepdims=True)
    ex = jnp.exp(lt - mx)
    aff = ex / jnp.sum(ex, axis=0, keepdims=True)
    key = pltpu.bitcast(aff, i32)

    def search(i, thr):
        cand = thr | (jnp.int32(1) << (30 - i))
        cnt = jnp.sum((key >= cand).astype(f32), axis=1, keepdims=True)
        return jnp.where(cnt >= float(cap), cand, thr)

    thr = lax.fori_loop(0, 31, search, jnp.zeros((N_EXPERTS, 1), i32))
    gt = key > thr
    eq = key == thr
    need = float(cap) - jnp.sum(gt.astype(f32), axis=1, keepdims=True)

    upper = (lax.broadcasted_iota(i32, (LANES, LANES), 0) <= lax.broadcasted_iota(i32, (LANES, LANES), 1)).astype(bf16)
    run = jnp.zeros((N_EXPERTS, 1), f32)
    sel_tiles = []
    for j in range(nt):
        tsl = slice(j * LANES, (j + 1) * LANES)
        eqj = eq[:, tsl].astype(f32)
        inc = jnp.dot(eqj.astype(bf16), upper, preferred_element_type=f32)
        excl = inc - eqj + run
        run = run + inc[:, LANES - 1:LANES]
        sel_tiles.append(jnp.logical_or(gt[:, tsl], jnp.logical_and(eq[:, tsl], excl < need)))

    lower = (lax.broadcasted_iota(i32, (LANES, LANES), 0) >= lax.broadcasted_iota(i32, (LANES, LANES), 1)).astype(bf16)
    zpad = jnp.zeros((LANES - N_EXPERTS, LANES), f32)
    run_t = jnp.zeros((1, LANES), f32)
    for j in range(nt):
        selj = jnp.concatenate([sel_tiles[j].astype(f32), zpad], axis=0).T
        inc = jnp.dot(lower, selj.astype(bf16), preferred_element_type=f32)
        excl = inc - selj + run_t
        run_t = run_t + inc[LANES - 1:LANES, :]
        pos_scr[pl.ds(j * LANES, LANES), :] = jnp.where(selj > 0.5, excl, -1.0)

    for j in range(nt):
        a = aff[:, j * LANES:(j + 1) * LANES]
        a_hi = a.astype(bf16).astype(f32)
        a_mid = (a - a_hi).astype(bf16).astype(f32)
        a_lo = (a - a_hi - a_mid).astype(bf16).astype(f32)
        vals_scr[0, j] = a_hi
        vals_scr[1, j] = a_mid
        vals_scr[2, j] = a_lo

    slot = lax.broadcasted_iota(i32, (LANES, LANES), 1).astype(f32)
    kind = lax.broadcasted_iota(i32, (SUBLANES, LANES), 0)
    tok_lo = lax.broadcasted_iota(i32, (SUBLANES, LANES), 1).astype(f32)
    for e in range(N_EXPERTS):
        def body(j, accs, e=e):
            row = pl.multiple_of(j * LANES, LANES)
            pcol = pos_scr[pl.ds(row, LANES), :][:, e:e + 1]
            v = jnp.where(kind == 0, jnp.asarray(j, i32).astype(f32), tok_lo)
            for k in range(3):
                v = jnp.where(kind == 2 + k, vals_scr[k, j][e:e + 1], v)
            v = jnp.where(kind > 4, 0.0, v).astype(bf16)
            out = []
            for q in range(nq):
                onehot = (pcol == slot + float(q * LANES)).astype(bf16)
                out.append(accs[q] + jnp.dot(v, onehot, preferred_element_type=f32))
            return tuple(out)

        accs = lax.fori_loop(0, nt, body, tuple(jnp.zeros((SUBLANES, LANES), f32) for _ in range(nq)))
        for q in range(nq):
            qsl = slice(q * LANES, (q + 1) * LANES)
            acc = accs[q]
            idx_ref[0, e:e + 1, qsl] = (acc[0:1] * float(LANES) + acc[1:2]).astype(i32)
            gate_ref[0, e:e + 1, qsl] = acc[2:3] + acc[3:4] + acc[4:5]


def _route(logits, cap):
    b, s, _ = logits.shape
    return pl.pallas_call(
        functools.partial(_route_body, cap=cap),
        grid=(b,),
        in_specs=[pl.BlockSpec((1, s, LANES), lambda i: (i, 0, 0))],
        out_specs=[pl.BlockSpec((1, N_EXPERTS, cap), lambda i: (i, 0, 0))] * 2,
        out_shape=[jax.ShapeDtypeStruct((b, N_EXPERTS, cap), i32),
                   jax.ShapeDtypeStruct((b, N_EXPERTS, cap), f32)],
        scratch_shapes=[pltpu.VMEM((s, LANES), f32),
                        pltpu.VMEM((3, s // LANES, N_EXPERTS, LANES), f32)],
        compiler_params=_params(("parallel",), VMEM_LIMIT),
        name="route",
    )(logits)


def _row_copy(src_hbm, b, tok, dst, r, sem):
    return pltpu.make_async_copy(src_hbm.at[b, pl.ds(tok, 1), :], dst.at[pl.ds(r, 1), :], sem)


def _moe_body(idx_ref, gate_ref, wg_ref, wu_ref, wd_ref, hn_hbm, o_ref, xg, sem, *, cap):
    b = pl.program_id(1)

    def start(r, carry):
        _row_copy(hn_hbm, b, idx_ref[0, 0, 0, r], xg, r, sem).start()
        return carry

    lax.fori_loop(0, cap, start, 0, unroll=8)

    def wait(r, carry):
        _row_copy(hn_hbm, b, 0, xg, r, sem).wait()
        return carry

    lax.fori_loop(0, cap, wait, 0, unroll=8)

    tm = min(cap, 256)
    for m in range(cap // tm):
        rows = slice(m * tm, (m + 1) * tm)
        xm = xg[rows, :].astype(bf16)
        hg = jnp.dot(xm, wg_ref[0], preferred_element_type=f32)
        hu = jnp.dot(xm, wu_ref[0], preferred_element_type=f32)
        hid = (jax.nn.silu(hg) * hu).astype(bf16)
        y = jnp.dot(hid, wd_ref[0], preferred_element_type=f32)
        o_ref[0, 0, rows, :] = y * gate_ref[0, 0, rows, :]


def _moe_ffn(hn3, idx, gates, wg, wu, wd):
    b, s, d = hn3.shape
    cap = idx.shape[-1]
    idx4 = idx.reshape(b, N_EXPERTS, 1, cap)
    gates4 = gates.reshape(b, N_EXPERTS, cap, 1)
    return pl.pallas_call(
        functools.partial(_moe_body, cap=cap),
        grid=(N_EXPERTS, b),
        in_specs=[
            pl.BlockSpec((1, 1, 1, cap), lambda e, i: (i, e, 0, 0), memory_space=pltpu.SMEM),
            pl.BlockSpec((1, 1, cap, 1), lambda e, i: (i, e, 0, 0)),
            pl.BlockSpec((1, d, D_FF), lambda e, i: (e, 0, 0)),
            pl.BlockSpec((1, d, D_FF), lambda e, i: (e, 0, 0)),
            pl.BlockSpec((1, D_FF, d), lambda e, i: (e, 0, 0)),
            pl.BlockSpec(memory_space=pl.ANY),
        ],
        out_specs=pl.BlockSpec((1, 1, cap, d), lambda e, i: (i, e, 0, 0)),
        out_shape=jax.ShapeDtypeStruct((b, N_EXPERTS, cap, d), f32),
        scratch_shapes=[pltpu.VMEM((cap, d), f32), pltpu.SemaphoreType.DMA],
        compiler_params=_params(("arbitrary", "arbitrary"), VMEM_LIMIT),
        name="moe_ffn",
    )(idx4, gates4, wg, wu, wd, hn3)


def _combine_body(idx_ref, yg_ref, x_hbm, o_hbm, buf, sem, *, cap):
    del x_hbm
    b = pl.program_id(0)

    def gather(r, carry):
        _row_copy(o_hbm, b, idx_ref[0, 0, 0, r], buf, r, sem).start()
        return carry

    lax.fori_loop(0, cap, gather, 0, unroll=8)

    def gather_wait(r, carry):
        _row_copy(o_hbm, b, 0, buf, r, sem).wait()
        return carry

    lax.fori_loop(0, cap, gather_wait, 0, unroll=8)
    buf[...] = buf[...] + yg_ref[0, 0]

    def scatter(r, carry):
        tok = idx_ref[0, 0, 0, r]
        pltpu.make_async_copy(buf.at[pl.ds(r, 1), :], o_hbm.at[b, pl.ds(tok, 1), :], sem).start()
        return carry

    lax.fori_loop(0, cap, scatter, 0, unroll=8)

    def scatter_wait(r, carry):
        pltpu.make_async_copy(buf.at[pl.ds(r, 1), :], o_hbm.at[b, pl.ds(0, 1), :], sem).wait()
        return carry

    lax.fori_loop(0, cap, scatter_wait, 0, unroll=8)


def _combine(x3, idx, yg):
    b, s, d = x3.shape
    cap = idx.shape[-1]
    idx4 = idx.reshape(b, N_EXPERTS, 1, cap)
    return pl.pallas_call(
        functools.partial(_combine_body, cap=cap),
        grid=(b, N_EXPERTS),
        in_specs=[
            pl.BlockSpec((1, 1, 1, cap), lambda i, e: (i, e, 0, 0), memory_space=pltpu.SMEM),
            pl.BlockSpec((1, 1, cap, d), lambda i, e: (i, e, 0, 0)),
            pl.BlockSpec(memory_space=pl.ANY),
        ],
        out_specs=pl.BlockSpec(memory_space=pl.ANY),
        out_shape=jax.ShapeDtypeStruct((b, s, d), f32),
        input_output_aliases={2: 0},
        scratch_shapes=[pltpu.VMEM((cap, d), f32), pltpu.SemaphoreType.DMA],
        compiler_params=_params(("arbitrary", "arbitrary")),
        name="moe_combine",
    )(idx4, yg, x3)


def _final_norm_body(x_ref, g_ref, o_ref):
    o_ref[...] = _rms(x_ref[...], g_ref[...])


def _final_norm(x2d, g):
    t, d = x2d.shape
    tm = _pick(t, 1024)
    return pl.pallas_call(
        _final_norm_body,
        grid=(t // tm,),
        in_specs=[pl.BlockSpec((tm, d), lambda i: (i, 0)), pl.BlockSpec((1, d), lambda i: (0, 0))],
        out_specs=pl.BlockSpec((tm, d), lambda i: (i, 0)),
        out_shape=jax.ShapeDtypeStruct((t, d), f32),
        compiler_params=_params(("parallel",)),
        name="final_norm",
    )(x2d, g.reshape(1, d))


def _pad_lanes(v):
    return jnp.zeros((1, LANES), f32).at[0, :v.shape[0]].set(v.astype(f32))


def _mixer(x3, norm_mix, w_in, lru_conv_w, lru_conv_b, lru_wa, lru_ba, lru_wi, lru_bi, lru_lambda,
           ssd_conv_w, ssd_conv_b, ssd_a_log, ssd_dt_bias, ssd_d):
    b, s, d = x3.shape
    x2d = x3.reshape(b * s, d)
    o0, o1, o2, o3 = LRU_WIDTH, 2 * LRU_WIDTH, 2 * LRU_WIDTH + SSD_WIDTH, 2 * LRU_WIDTH + SSD_WIDTH + SSD_CONV_CH
    w_bf = w_in.astype(bf16)
    w_dt = jnp.zeros((d, LANES), bf16).at[:, :2 * SSD_HEADS].set(w_bf[:, o3:])
    xr = _norm_matmul(x2d, norm_mix, w_bf[:, :o0]).reshape(b, s, LRU_WIDTH)
    gate = _norm_matmul(x2d, norm_mix, w_bf[:, o0:o1])
    z = _norm_matmul(x2d, norm_mix, w_bf[:, o1:o2])
    xbc = _norm_matmul(x2d, norm_mix, w_bf[:, o2:o3]).reshape(b, s, SSD_CONV_CH)
    dt = _norm_matmul(x2d, norm_mix, w_dt).reshape(b, s, LANES)

    hs = []
    for dr in range(2):
        w_gate = jnp.concatenate([lru_wa[dr], lru_wi[dr]], axis=-1).astype(bf16)
        hs.append(_lru_scan(xr, lru_conv_w, lru_conv_b, w_gate, lru_ba[dr], lru_bi[dr], lru_lambda[dr],
                            reverse=bool(dr)))
    alog_pad = _pad_lanes(ssd_a_log.reshape(-1))
    dtb_pad = _pad_lanes(ssd_dt_bias.reshape(-1))
    dskip_pad = _pad_lanes(ssd_d)
    ys = [_ssd_scan(xbc, dt, ssd_conv_w, ssd_conv_b, alog_pad, dtb_pad, dskip_pad, reverse=bool(dr))
          for dr in range(2)]
    t = b * s
    return (hs[0].reshape(t, -1), hs[1].reshape(t, -1), gate, ys[0].reshape(t, -1), ys[1].reshape(t, -1), z)


def _layer(x3, norm_mix, w_in, lru_conv_w, lru_conv_b, lru_wa, lru_ba, lru_wi, lru_bi, lru_lambda,
           ssd_conv_w, ssd_conv_b, ssd_a_log, ssd_dt_bias, ssd_d, ssd_norm, w_out, norm_ffn, w_router,
           w_gate, w_up, w_down):
    b, s, d = x3.shape
    cap = max(1, CAPACITY_FACTOR * s // N_EXPERTS)
    assert s % LANES == 0 and cap % LANES == 0, "sequence length must give 128-aligned expert capacity"
    hf, hb, gate, yf, yb, z = _mixer(x3, norm_mix, w_in, lru_conv_w, lru_conv_b, lru_wa, lru_ba, lru_wi,
                                     lru_bi, lru_lambda, ssd_conv_w, ssd_conv_b, ssd_a_log, ssd_dt_bias, ssd_d)
    wr = jnp.zeros((d, LANES), f32).at[:, :N_EXPERTS].set(w_router)
    wr_hi = wr.astype(bf16)
    wr_lo = (wr - wr_hi.astype(f32)).astype(bf16)
    xn, hn, logits = _mix_out(x3.reshape(b * s, d), hf, hb, gate, yf, yb, z, ssd_norm, w_out.astype(bf16),
                              norm_ffn, wr_hi, wr_lo)
    idx, gates = _route(logits.reshape(b, s, LANES), cap)
    yg = _moe_ffn(hn.reshape(b, s, d), idx, gates, w_gate.astype(bf16), w_up.astype(bf16), w_down.astype(bf16))
    return _combine(xn.reshape(b, s, d), idx, yg)


def kernel(x, norm_mix, w_in, lru_conv_w, lru_conv_b, lru_wa, lru_ba, lru_wi, lru_bi, lru_lambda, ssd_conv_w, ssd_conv_b, ssd_a_log, ssd_dt_bias, ssd_d, ssd_norm, w_out, norm_ffn, w_router, w_gate, w_up, w_down, norm_final):
    depth = norm_mix.shape[0]
    for l in range(depth):
        x = _layer(x, norm_mix[l], w_in[l], lru_conv_w[l], lru_conv_b[l], lru_wa[l], lru_ba[l], lru_wi[l],
                   lru_bi[l], lru_lambda[l], ssd_conv_w[l], ssd_conv_b[l], ssd_a_log[l], ssd_dt_bias[l],
                   ssd_d[l], ssd_norm[l], w_out[l], norm_ffn[l], w_router[l], w_gate[l], w_up[l], w_down[l])
    b, s, d = x.shape
    return _final_norm(x.reshape(b * s, d), norm_final).reshape(b, s, d)
```

```python
import functools

import jax
import jax.numpy as jnp
from jax import lax
from jax.experimental import pallas as pl
from jax.experimental.pallas import tpu as pltpu

f32 = jnp.float32
bf16 = jnp.bfloat16
i32 = jnp.int32

D_MODEL = 1024
EPS = 1e-6
CONV_WIDTH = 4
LRU_WIDTH = 1024
LRU_HEADS = 8
LRU_BLOCK = 128
LRU_C = 8.0
SSD_WIDTH = 1024
SSD_HEADDIM = 64
SSD_HEADS = 16
SSD_GROUPS = 4
SSD_STATE = 128
SSD_GN = SSD_GROUPS * SSD_STATE
SSD_CONV_CH = SSD_WIDTH + 2 * SSD_GN
GROUP_WIDTH = SSD_WIDTH // SSD_GROUPS
N_EXPERTS = 16
CAPACITY_FACTOR = 2
D_FF = 2048

LANES = 128
SUBLANES = 8
HALO = SUBLANES
WINDOW = LANES + SUBLANES
VMEM_LIMIT = 56 * 1024 * 1024


def _pick(n, target):
    if n <= target:
        return n
    t = target
    while t >= LANES:
        if n % t == 0:
            return t
        t -= LANES
    return n


def _params(sem, vmem=None):
    return pltpu.CompilerParams(dimension_semantics=sem, vmem_limit_bytes=vmem)


IN_SEGMENTS = (LRU_WIDTH, LRU_WIDTH, SSD_WIDTH, SSD_CONV_CH, LANES)


def _in_proj_body(x_ref, g_ref, w_ref, *o_refs):
    x = x_ref[...]
    ms = jnp.mean(x * x, axis=-1, keepdims=True)
    hn = (x * lax.rsqrt(ms + EPS) * g_ref[...]).astype(bf16)
    off = 0
    for o_ref, width in zip(o_refs, IN_SEGMENTS):
        o_ref[...] = jnp.dot(hn, w_ref[:, off:off + width], preferred_element_type=f32)
        off += width


def _in_proj(x2d, g, w_bf16):
    t, d = x2d.shape
    n = w_bf16.shape[1]
    tm = _pick(t, 512)
    return pl.pallas_call(
        _in_proj_body,
        grid=(t // tm,),
        in_specs=[
            pl.BlockSpec((tm, d), lambda i: (i, 0)),
            pl.BlockSpec((1, d), lambda i: (0, 0)),
            pl.BlockSpec((d, n), lambda i: (0, 0), pipeline_mode=pl.Buffered(1)),
        ],
        out_specs=[pl.BlockSpec((tm, w), lambda i: (i, 0)) for w in IN_SEGMENTS],
        out_shape=[jax.ShapeDtypeStruct((t, w), f32) for w in IN_SEGMENTS],
        compiler_params=_params(("parallel",), VMEM_LIMIT),
        name="in_proj",
    )(x2d, g.reshape(1, d), w_bf16)


def _chunk_index(c, nc, reverse):
    return (nc - 1 - c) if reverse else c


def _halo_specs(tc, width, nc, s, reverse):
    per = tc // HALO
    last = s // HALO - 1

    def cur(b, c):
        return (b, _chunk_index(c, nc, reverse), 0)

    def prev(b, c):
        return (b, jnp.maximum(_chunk_index(c, nc, reverse) * per - 1, 0), 0)

    def nxt(b, c):
        return (b, jnp.minimum((_chunk_index(c, nc, reverse) + 1) * per, last), 0)

    return [
        pl.BlockSpec((1, HALO, width), prev),
        pl.BlockSpec((1, tc, width), cur),
        pl.BlockSpec((1, HALO, width), nxt),
    ]


def _conv_centred(prev_ref, x_ref, next_ref, cw_ref, cb_ref, ci, nc):
    tc = x_ref.shape[1]
    prev = jnp.where(ci == 0, 0.0, prev_ref[0])
    nxt = jnp.where(ci == nc - 1, 0.0, next_ref[0])
    xe = jnp.concatenate([prev, x_ref[0], nxt], axis=0)
    cw = cw_ref[...]
    acc = cb_ref[...] + xe[HALO - 1:HALO - 1 + tc] * cw[0:1]
    for k in range(1, CONV_WIDTH):
        acc = acc + xe[HALO - 1 + k:HALO - 1 + k + tc] * cw[k:k + 1]
    return acc


def _scan_rows8(a, u, reverse):
    r = lax.broadcasted_iota(i32, a.shape, 1)
    for k in (1, 2, 4):
        shift = (SUBLANES - k) if reverse else k
        valid = (r < SUBLANES - k) if reverse else (r >= k)
        a_s = pltpu.roll(a, shift, 1)
        u_s = pltpu.roll(u, shift, 1)
        u = jnp.where(valid, a * u_s + u, u)
        a = jnp.where(valid, a * a_s, a)
    return a, u


def _lru_body(prev_ref, x_ref, next_ref, cw_ref, cb_ref, w_ref, ba_ref, bi_ref, lam_ref,
              o_ref, carry_ref, a_scr, u_scr, *, reverse, nc):
    c = pl.program_id(1)
    ci = _chunk_index(c, nc, reverse)
    tc = x_ref.shape[1]
    groups = tc // SUBLANES

    @pl.when(c == 0)
    def _():
        carry_ref[...] = jnp.zeros_like(carry_ref)

    xc = _conv_centred(prev_ref, x_ref, next_ref, cw_ref, cb_ref, ci, nc)
    sp = jax.nn.softplus(-lam_ref[...])
    for h in range(LRU_HEADS):
        sl = slice(h * LRU_BLOCK, (h + 1) * LRU_BLOCK)
        xh = xc[:, sl]
        pre = jnp.dot(xh.astype(bf16), w_ref[h], preferred_element_type=f32)
        r = jax.nn.sigmoid(pre[:, :LRU_BLOCK] + ba_ref[:, sl])
        gi = jax.nn.sigmoid(pre[:, LRU_BLOCK:] + bi_ref[:, sl])
        log_a = (-LRU_C) * r * sp[:, sl]
        a = jnp.exp(log_a)
        u = jnp.sqrt(1.0 - a * a) * (gi * xh)
        a3, u3 = _scan_rows8(a.reshape(groups, SUBLANES, LRU_BLOCK),
                             u.reshape(groups, SUBLANES, LRU_BLOCK), reverse)
        a_scr[:, sl] = a3.reshape(tc, LRU_BLOCK)
        u_scr[:, sl] = u3.reshape(tc, LRU_BLOCK)

    def step(g, carry):
        gi_ = (groups - 1 - g) if reverse else g
        row = pl.multiple_of(gi_ * SUBLANES, SUBLANES)
        hblk = u_scr[pl.ds(row, SUBLANES), :] + a_scr[pl.ds(row, SUBLANES), :] * carry
        o_ref[0, pl.ds(row, SUBLANES), :] = hblk
        return hblk[0:1] if reverse else hblk[SUBLANES - 1:SUBLANES]

    carry_ref[...] = lax.fori_loop(0, groups, step, carry_ref[...], unroll=4)


def _lru_scan(xr, conv_w, conv_b, w_gate, ba, bi, lam, reverse):
    b, s, w = xr.shape
    tc = _pick(s, 256)
    nc = s // tc
    full = lambda shape: pl.BlockSpec(shape, lambda b_, c_: (0,) * len(shape))
    return pl.pallas_call(
        functools.partial(_lru_body, reverse=reverse, nc=nc),
        grid=(b, nc),
        in_specs=_halo_specs(tc, w, nc, s, reverse) + [
            full((CONV_WIDTH, w)), full((1, w)),
            full((LRU_HEADS, LRU_BLOCK, 2 * LRU_BLOCK)),
            full((1, w)), full((1, w)), full((1, w)),
        ],
        out_specs=pl.BlockSpec((1, tc, w), lambda b_, c_: (b_, _chunk_index(c_, nc, reverse), 0)),
        out_shape=jax.ShapeDtypeStruct((b, s, w), f32),
        scratch_shapes=[pltpu.VMEM((1, w), f32), pltpu.VMEM((tc, w), f32), pltpu.VMEM((tc, w), f32)],
        compiler_params=_params(("parallel", "arbitrary")),
        name="lru_bwd" if reverse else "lru_fwd",
    )(xr, xr, xr, conv_w, conv_b.reshape(1, w), w_gate, ba.reshape(1, w), bi.reshape(1, w),
      lam.reshape(1, w))


def _expand_heads(arr, base):
    rows = arr.shape[0]
    lane = lax.broadcasted_iota(i32, (rows, LANES), 1)
    tiles = []
    for k in range(SSD_HEADS // 2):
        c0 = arr[:, base + 2 * k:base + 2 * k + 1]
        c1 = arr[:, base + 2 * k + 1:base + 2 * k + 2]
        tiles.append(jnp.where(lane < SSD_HEADDIM, c0, c1))
    return jnp.concatenate(tiles, axis=1)


def _cumsum_rows(x, reverse):
    n = x.shape[0]
    r = lax.broadcasted_iota(i32, x.shape, 0)
    k = 1
    while k < n:
        if reverse:
            x = x + jnp.where(r < n - k, pltpu.roll(x, n - k, 0), 0.0)
        else:
            x = x + jnp.where(r >= k, pltpu.roll(x, k, 0), 0.0)
        k *= 2
    return x


def _ssd_body(prev_ref, x_ref, next_ref, dt_ref, cw_ref, cb_ref, alog_ref, dtb_ref, dskip_ref,
              o_ref, state_ref, *, reverse, nc):
    c = pl.program_id(1)
    ci = _chunk_index(c, nc, reverse)
    L = x_ref.shape[1]
    base = SSD_HEADS if reverse else 0

    @pl.when(c == 0)
    def _():
        state_ref[...] = jnp.zeros_like(state_ref)

    xact = jax.nn.silu(_conv_centred(prev_ref, x_ref, next_ref, cw_ref, cb_ref, ci, nc))
    xs = xact[:, :SSD_WIDTH]

    dt = jax.nn.softplus(dt_ref[0] + dtb_ref[...])
    d_a = dt * (-jnp.exp(alog_ref[...]))
    cum = _cumsum_rows(d_a, reverse)
    edge = cum[0:1] if reverse else cum[L - 1:L]
    cum_t = cum.T

    dtx = _expand_heads(dt, base) * xs
    e_cum = _expand_heads(jnp.exp(cum), base)
    e_end = _expand_heads(jnp.exp(edge - cum), base)
    e_edge = _expand_heads(jnp.exp(edge), base)
    w_all = (e_end * dtx).astype(bf16)
    dtx_b = dtx.astype(bf16)

    li = lax.broadcasted_iota(i32, (L, L), 0)
    si = lax.broadcasted_iota(i32, (L, L), 1)
    tri = (si >= li) if reverse else (li >= si)
    lane = lax.broadcasted_iota(i32, (L, LANES), 1)
    lo_half = lane < SSD_HEADDIM

    outs = []
    for g in range(SSD_GROUPS):
        bsl = slice(SSD_WIDTH + g * SSD_STATE, SSD_WIDTH + (g + 1) * SSD_STATE)
        csl = slice(SSD_WIDTH + SSD_GN + g * SSD_STATE, SSD_WIDTH + SSD_GN + (g + 1) * SSD_STATE)
        gsl = slice(g * GROUP_WIDTH, (g + 1) * GROUP_WIDTH)
        bm = xact[:, bsl].astype(bf16)
        cm = xact[:, csl].astype(bf16)
        cb = lax.dot_general(cm, bm, (((1,), (1,)), ((), ())), preferred_element_type=f32)
        s_old = state_ref[g]
        y_off = jnp.dot(cm, s_old.astype(bf16), preferred_element_type=f32) * e_cum[:, gsl]
        tiles = []
        for p in range(2):
            acc = None
            for q in range(2):
                j = base + g * 4 + 2 * p + q
                seg = cum[:, j:j + 1] - cum_t[j:j + 1, :]
                decay = jnp.exp(jnp.where(tri, seg, -jnp.inf))
                m = (cb * decay).astype(bf16)
                tsl = slice(g * GROUP_WIDTH + p * LANES, g * GROUP_WIDTH + (p + 1) * LANES)
                rhs = jnp.where(lo_half if q == 0 else jnp.logical_not(lo_half), dtx_b[:, tsl], 0.0)
                part = jnp.dot(m, rhs.astype(bf16), preferred_element_type=f32)
                acc = part if acc is None else acc + part
            tiles.append(acc)
        outs.append(jnp.concatenate(tiles, axis=1) + y_off)
        upd = lax.dot_general(bm, w_all[:, gsl], (((0,), (0,)), ((), ())), preferred_element_type=f32)
        state_ref[g] = s_old * e_edge[:, gsl] + upd
    y = jnp.concatenate(outs, axis=1)
    if not reverse:
        y = y + _expand_heads(dskip_ref[...], 0) * xs
    o_ref[0] = y


def _ssd_scan(xbc, dt_pad, conv_w, conv_b, alog_pad, dtb_pad, dskip_pad, reverse):
    b, s, w = xbc.shape
    L = _pick(s, 128)
    nc = s // L
    full = lambda shape: pl.BlockSpec(shape, lambda b_, c_: (0,) * len(shape))
    return pl.pallas_call(
        functools.partial(_ssd_body, reverse=reverse, nc=nc),
        grid=(b, nc),
        in_specs=_halo_specs(L, w, nc, s, reverse) + [
            pl.BlockSpec((1, L, LANES), lambda b_, c_: (b_, _chunk_index(c_, nc, reverse), 0)),
            full((CONV_WIDTH, w)), full((1, w)),
            full((1, LANES)), full((1, LANES)), full((1, LANES)),
        ],
        out_specs=pl.BlockSpec((1, L, SSD_WIDTH), lambda b_, c_: (b_, _chunk_index(c_, nc, reverse), 0)),
        out_shape=jax.ShapeDtypeStruct((b, s, SSD_WIDTH), f32),
        scratch_shapes=[pltpu.VMEM((SSD_GROUPS, SSD_STATE, GROUP_WIDTH), f32)],
        compiler_params=_params(("parallel", "arbitrary")),
        name="ssd_bwd" if reverse else "ssd_fwd",
    )(xbc, xbc, xbc, dt_pad, conv_w, conv_b.reshape(1, w), alog_pad, dtb_pad, dskip_pad)


def _rms(x, g):
    ms = jnp.mean(x * x, axis=-1, keepdims=True)
    return x * lax.rsqrt(ms + EPS) * g


def _mix_out_body(x_ref, hf_ref, hb_ref, gate_ref, yf_ref, yb_ref, z_ref, gn_ref, wo_ref,
                  gf_ref, wrh_ref, wrl_ref, xo_ref, hn_ref, lg_ref):
    y_lru = (hf_ref[...] + hb_ref[...]) * jax.nn.gelu(gate_ref[...])
    y = (yf_ref[...] + yb_ref[...]) * jax.nn.silu(z_ref[...])
    parts = []
    for g in range(SSD_GROUPS):
        yg = y[:, g * GROUP_WIDTH:(g + 1) * GROUP_WIDTH]
        ms = jnp.mean(yg * yg, axis=-1, keepdims=True)
        parts.append(yg * lax.rsqrt(ms + EPS))
    y_ssd = jnp.concatenate(parts, axis=1) * gn_ref[...]
    mix = jnp.concatenate([y_lru, y_ssd], axis=1).astype(bf16)
    xn = x_ref[...] + jnp.dot(mix, wo_ref[...], preferred_element_type=f32)
    xo_ref[...] = xn
    hn = _rms(xn, gf_ref[...])
    tm = hn.shape[0]
    for k in range(hn.shape[1] // LANES):
        hn_ref[pl.ds(k, tm, stride=SUBLANES), :] = hn[:, k * LANES:(k + 1) * LANES]
    h_hi = hn.astype(bf16)
    h_lo = (hn - h_hi.astype(f32)).astype(bf16)
    lg = jnp.dot(h_hi, wrh_ref[...], preferred_element_type=f32)
    lg = lg + jnp.dot(h_lo, wrh_ref[...], preferred_element_type=f32)
    lg = lg + jnp.dot(h_hi, wrl_ref[...], preferred_element_type=f32)
    lg_ref[...] = lg


def _mix_out(x2d, hf, hb, gate, yf, yb, z, ssd_norm, w_out_bf16, norm_ffn, wr_hi, wr_lo):
    t, d = x2d.shape
    tm = _pick(t, 256)
    row = lambda w: pl.BlockSpec((tm, w), lambda i: (i, 0))
    full = lambda shape: pl.BlockSpec(shape, lambda i: (0,) * len(shape))
    return pl.pallas_call(
        _mix_out_body,
        grid=(t // tm,),
        in_specs=[row(d)] + [row(LRU_WIDTH)] * 3 + [row(SSD_WIDTH)] * 3 + [
            full((1, SSD_WIDTH)), full((LRU_WIDTH + SSD_WIDTH, d)), full((1, d)),
            full((d, LANES)), full((d, LANES)),
        ],
        out_specs=[row(d), pl.BlockSpec((tm * d // LANES, LANES), lambda i: (i, 0)), row(LANES)],
        out_shape=[jax.ShapeDtypeStruct((t, d), f32), jax.ShapeDtypeStruct((t * d // LANES, LANES), f32),
                   jax.ShapeDtypeStruct((t, LANES), f32)],
        compiler_params=_params(("parallel",), VMEM_LIMIT),
        name="mix_out",
    )(x2d, hf, hb, gate, yf, yb, z, ssd_norm.reshape(1, -1), w_out_bf16, norm_ffn.reshape(1, d),
      wr_hi, wr_lo)


def _tile_prefix(tiles, upper_incl, upper_strict, lane):
    incs = [jnp.dot(t.astype(bf16), upper_incl, preferred_element_type=f32) for t in tiles]
    tot = jnp.zeros((N_EXPERTS, LANES), f32)
    for j, inc in enumerate(incs):
        tot = jnp.where(lane == j, inc[:, LANES - 1:LANES], tot)
    start = jnp.dot(tot.astype(bf16), upper_strict, preferred_element_type=f32)
    return incs, tot, start


def _route_body(lg_ref, idx_ref, gate_ref, lpos_ref, tab_ref, lm_scr, vt_scr, list_scr, tabv_scr, tabs_scr, sem,
                *, cap):
    s = lg_ref.shape[1]
    nt = s // LANES
    lt = lg_ref[0].T[:N_EXPERTS]
    mx = jnp.max(lt, axis=0, keepdims=True)
    ex = jnp.exp(lt - mx)
    aff = ex / jnp.sum(ex, axis=0, keepdims=True)
    key = pltpu.bitcast(aff, i32)

    def search(i, thr):
        cand = thr | (jnp.int32(1) << (30 - i))
        cnt = jnp.sum((key >= cand).astype(f32), axis=1, keepdims=True)
        return jnp.where(cnt >= float(cap), cand, thr)

    thr = lax.fori_loop(0, 31, search, jnp.zeros((N_EXPERTS, 1), i32))
    gt = key > thr
    eq = key == thr
    need = float(cap) - jnp.sum(gt.astype(f32), axis=1, keepdims=True)

    sub_i = lax.broadcasted_iota(i32, (LANES, LANES), 0)
    lane_i = lax.broadcasted_iota(i32, (LANES, LANES), 1)
    upper_incl = (sub_i <= lane_i).astype(bf16)
    upper_strict = (sub_i < lane_i).astype(bf16)
    lane_e = lax.broadcasted_iota(i32, (N_EXPERTS, LANES), 1)
    tiles = lambda a: [a[:, j * LANES:(j + 1) * LANES] for j in range(nt)]

    eq_t = tiles(eq.astype(f32))
    incs, _, start = _tile_prefix(eq_t, upper_incl, upper_strict, lane_e)
    sel_t = []
    for j, (gtj, eqj) in enumerate(zip(tiles(gt), eq_t)):
        excl = incs[j] - eqj + start[:, j:j + 1]
        sel_t.append(jnp.logical_or(gtj, jnp.logical_and(eqj > 0.5, excl < need)).astype(f32))

    incs, tot, start = _tile_prefix(sel_t, upper_incl, upper_strict, lane_e)
    start8 = jnp.floor(start * (1.0 / SUBLANES)) * float(SUBLANES)
    tab_ref[0, 0] = start8.astype(i32)
    tab_ref[0, 1] = tot.astype(i32)
    tabv_scr[...] = start.astype(i32)
    to_smem = pltpu.make_async_copy(tabv_scr, tabs_scr, sem)
    to_smem.start()

    kind = lax.broadcasted_iota(i32, (SUBLANES, LANES), 0)
    tok_lane = lax.broadcasted_iota(i32, (SUBLANES, LANES), 1).astype(f32)
    fill = jnp.full((LANES - N_EXPERTS, LANES), -1.0, f32)
    zrows = jnp.zeros((LANES - 3 * N_EXPERTS - SUBLANES, LANES), f32)
    for j, a in enumerate(tiles(aff)):
        lm = jnp.where(sel_t[j] > 0.5, incs[j] - sel_t[j], -1.0)
        lm_scr[j] = lm
        shifted = jnp.where(sel_t[j] > 0.5, lm + (start[:, j:j + 1] - start8[:, j:j + 1]), -1.0)
        lpos_ref[0, pl.ds(j * LANES, LANES), :] = jnp.concatenate([shifted, fill], axis=0).T
        a_hi = a.astype(bf16).astype(f32)
        a_mid = (a - a_hi).astype(bf16).astype(f32)
        a_lo = (a - a_hi - a_mid).astype(bf16).astype(f32)
        tok = jnp.where(kind == 0, float(j), jnp.where(kind == 1, tok_lane, 0.0))
        vt_scr[j] = jnp.concatenate([a_hi, a_mid, a_lo, tok, zrows], axis=0).T.astype(bf16)

    to_smem.wait()
    rank = sub_i.astype(f32)

    def compact(j, carry):
        lm = lm_scr[j]
        vt = vt_scr[j]
        for e in range(N_EXPERTS):
            onehot = (lm[e:e + 1] == rank).astype(bf16)
            packed = jnp.dot(onehot, vt, preferred_element_type=f32)
            list_scr[e, pl.ds(tabs_scr[e, j], LANES), :] = packed
        return carry

    lax.fori_loop(0, nt, compact, 0)

    lane_c = lax.broadcasted_iota(i32, (cap, LANES), 1)
    idx_c = jnp.zeros((cap, LANES), f32)
    gate_c = jnp.zeros((cap, LANES), f32)
    for e in range(N_EXPERTS):
        rows = list_scr[e, 0:cap, :]
        g = rows[:, e:e + 1] + rows[:, N_EXPERTS + e:N_EXPERTS + e + 1] + rows[:, 2 * N_EXPERTS + e:2 * N_EXPERTS + e + 1]
        t = rows[:, 3 * N_EXPERTS:3 * N_EXPERTS + 1] * float(LANES) + rows[:, 3 * N_EXPERTS + 1:3 * N_EXPERTS + 2]
        idx_c = jnp.where(lane_c == e, t, idx_c)
        gate_c = jnp.where(lane_c == e, g, gate_c)
    idx_ref[0] = idx_c.T[:N_EXPERTS].astype(i32)
    gate_ref[0] = gate_c.T[:N_EXPERTS]


def _route(logits, cap):
    b, s, _ = logits.shape
    nt = s // LANES
    assert nt <= LANES
    return pl.pallas_call(
        functools.partial(_route_body, cap=cap),
        grid=(b,),
        in_specs=[pl.BlockSpec((1, s, LANES), lambda i: (i, 0, 0))],
        out_specs=[pl.BlockSpec((1, N_EXPERTS, cap), lambda i: (i, 0, 0)),
                   pl.BlockSpec((1, N_EXPERTS, cap), lambda i: (i, 0, 0)),
                   pl.BlockSpec((1, s, LANES), lambda i: (i, 0, 0)),
                   pl.BlockSpec((1, 2, N_EXPERTS, LANES), lambda i: (i, 0, 0, 0))],
        out_shape=[jax.ShapeDtypeStruct((b, N_EXPERTS, cap), i32),
                   jax.ShapeDtypeStruct((b, N_EXPERTS, cap), f32),
                   jax.ShapeDtypeStruct((b, s, LANES), f32),
                   jax.ShapeDtypeStruct((b, 2, N_EXPERTS, LANES), i32)],
        scratch_shapes=[pltpu.VMEM((nt, N_EXPERTS, LANES), f32),
                        pltpu.VMEM((nt, LANES, LANES), bf16),
                        pltpu.VMEM((N_EXPERTS, cap + LANES, LANES), f32),
                        pltpu.VMEM((N_EXPERTS, LANES), i32),
                        pltpu.SMEM((N_EXPERTS, LANES), i32),
                        pltpu.SemaphoreType.DMA],
        compiler_params=_params(("parallel",), VMEM_LIMIT),
        name="route",
    )(logits)


def _token_copy(src_hbm, tok, dst, r, sem):
    src = src_hbm.at[pl.ds(pl.multiple_of(tok * SUBLANES, SUBLANES), SUBLANES), :]
    return pltpu.make_async_copy(src, dst.at[pl.ds(pl.multiple_of(r * SUBLANES, SUBLANES), SUBLANES), :], sem)


def _moe_body(idx_ref, gate_ref, wg_ref, wu_ref, wd_ref, hn_hbm, o_ref, xg, sem, *, cap, seq):
    base = pl.program_id(1) * seq

    def start(r, carry):
        _token_copy(hn_hbm, base + idx_ref[0, 0, 0, r], xg, r, sem).start()
        return carry

    lax.fori_loop(0, cap, start, 0, unroll=8)

    def wait(r, carry):
        _token_copy(hn_hbm, 0, xg, r, sem).wait()
        return carry

    lax.fori_loop(0, cap, wait, 0, unroll=8)

    tm = min(cap, 256)
    ntile = wg_ref.shape[1] // LANES
    for m in range(cap // tm):
        rows = slice(m * tm, (m + 1) * tm)
        xm = jnp.concatenate([xg[pl.ds(m * tm * SUBLANES + k, tm, stride=SUBLANES), :] for k in range(ntile)],
                             axis=1).astype(bf16)
        hg = jnp.dot(xm, wg_ref[0], preferred_element_type=f32)
        hu = jnp.dot(xm, wu_ref[0], preferred_element_type=f32)
        hid = (jax.nn.silu(hg) * hu).astype(bf16)
        y = jnp.dot(hid, wd_ref[0], preferred_element_type=f32) * gate_ref[0, 0, rows, :]
        o_ref[0, 0, rows, :] = _pack_halves(y)
    o_ref[0, 0, cap:cap + WINDOW, :] = jnp.zeros((WINDOW, o_ref.shape[-1]), jnp.uint32)


def _pack_halves(y):
    w = y.shape[1] // 2
    hi = pltpu.bitcast(y[:, :w].astype(bf16).astype(f32), jnp.uint32)
    lo = pltpu.bitcast(y[:, w:].astype(bf16).astype(f32), jnp.uint32)
    return hi | (lo >> 16)


def _unpack_halves(p):
    hi = pltpu.bitcast(p & jnp.uint32(0xFFFF0000), f32)
    lo = pltpu.bitcast(p << 16, f32)
    return jnp.concatenate([hi, lo], axis=1).astype(bf16)


def _moe_ffn(hn_tiles, idx, gates, wg, wu, wd):
    b, _, cap = idx.shape
    d = wg.shape[1]
    s = hn_tiles.shape[0] * LANES // d // b
    idx4 = idx.reshape(b, N_EXPERTS, 1, cap)
    gates4 = gates.reshape(b, N_EXPERTS, cap, 1)
    return pl.pallas_call(
        functools.partial(_moe_body, cap=cap, seq=s),
        grid=(N_EXPERTS, b),
        in_specs=[
            pl.BlockSpec((1, 1, 1, cap), lambda e, i: (i, e, 0, 0), memory_space=pltpu.SMEM),
            pl.BlockSpec((1, 1, cap, 1), lambda e, i: (i, e, 0, 0)),
            pl.BlockSpec((1, d, D_FF), lambda e, i: (e, 0, 0)),
            pl.BlockSpec((1, d, D_FF), lambda e, i: (e, 0, 0)),
            pl.BlockSpec((1, D_FF, d), lambda e, i: (e, 0, 0)),
            pl.BlockSpec(memory_space=pl.ANY),
        ],
        out_specs=pl.BlockSpec((1, 1, cap + WINDOW, d // 2), lambda e, i: (i, e, 0, 0)),
        out_shape=jax.ShapeDtypeStruct((b, N_EXPERTS, cap + WINDOW, d // 2), jnp.uint32),
        scratch_shapes=[pltpu.VMEM((cap * d // LANES, LANES), f32), pltpu.SemaphoreType.DMA],
        compiler_params=_params(("arbitrary", "arbitrary"), VMEM_LIMIT),
        name="moe_ffn",
    )(idx4, gates4, wg, wu, wd, hn_tiles)


def _window_copy(yg_hbm, b, e, start, win, slot, sem):
    src = yg_hbm.at[b, e, pl.ds(pl.multiple_of(start, SUBLANES), WINDOW), :]
    return pltpu.make_async_copy(src, win.at[slot, e], sem.at[slot])


def _combine_body(tab_ref, tabn_ref, lpos_ref, x_ref, yg_hbm, o_ref, win, sem, *, nb, nt):
    b = pl.program_id(0)
    j = pl.program_id(1)
    n = b * nt + j
    slot = lax.rem(n, 2)

    @pl.when(n == 0)
    def _():
        for e in range(N_EXPERTS):
            _window_copy(yg_hbm, b, e, tab_ref[0, 0, e, j], win, slot, sem).start()

    @pl.when(n + 1 < nb * nt)
    def _():
        wrap = j + 1 == nt
        jn = jnp.where(wrap, 0, j + 1)
        bn = jnp.where(wrap, b + 1, b)
        for e in range(N_EXPERTS):
            _window_copy(yg_hbm, bn, e, tabn_ref[0, 0, e, jn], win, 1 - slot, sem).start()

    for e in range(N_EXPERTS):
        _window_copy(yg_hbm, b, e, 0, win, slot, sem).wait()

    lm = lpos_ref[0]
    row = lax.broadcasted_iota(i32, (LANES, WINDOW), 1).astype(f32)
    acc = x_ref[...]
    for e in range(N_EXPERTS):
        place = (lm[:, e:e + 1] == row).astype(bf16)
        acc = acc + jnp.dot(place, _unpack_halves(win[slot, e]), preferred_element_type=f32)
    o_ref[...] = acc


def _combine(x2d, lpos, tab, yg, nb):
    t, d = x2d.shape
    nt = t // nb // LANES

    def next_batch(i, j):
        return (jnp.minimum(i + (j + 1) // nt, nb - 1), 0, 0, 0)

    return pl.pallas_call(
        functools.partial(_combine_body, nb=nb, nt=nt),
        grid=(nb, nt),
        in_specs=[
            pl.BlockSpec((1, 2, N_EXPERTS, LANES), lambda i, j: (i, 0, 0, 0), memory_space=pltpu.SMEM),
            pl.BlockSpec((1, 2, N_EXPERTS, LANES), next_batch, memory_space=pltpu.SMEM),
            pl.BlockSpec((1, LANES, LANES), lambda i, j: (i, j, 0)),
            pl.BlockSpec((LANES, d), lambda i, j: (i * nt + j, 0)),
            pl.BlockSpec(memory_space=pl.ANY),
        ],
        out_specs=pl.BlockSpec((LANES, d), lambda i, j: (i * nt + j, 0)),
        out_shape=jax.ShapeDtypeStruct((t, d), f32),
        scratch_shapes=[pltpu.VMEM((2, N_EXPERTS, WINDOW, d // 2), jnp.uint32), pltpu.SemaphoreType.DMA((2,))],
        compiler_params=_params(("arbitrary", "arbitrary")),
        name="moe_combine",
    )(tab, tab, lpos, x2d, yg)


def _final_norm_body(x_ref, g_ref, o_ref):
    o_ref[...] = _rms(x_ref[...], g_ref[...])


def _final_norm(x2d, g):
    t, d = x2d.shape
    tm = _pick(t, 1024)
    return pl.pallas_call(
        _final_norm_body,
        grid=(t // tm,),
        in_specs=[pl.BlockSpec((tm, d), lambda i: (i, 0)), pl.BlockSpec((1, d), lambda i: (0, 0))],
        out_specs=pl.BlockSpec((tm, d), lambda i: (i, 0)),
        out_shape=jax.ShapeDtypeStruct((t, d), f32),
        compiler_params=_params(("parallel",)),
        name="final_norm",
    )(x2d, g.reshape(1, d))


def _pad_lanes(v):
    return jnp.zeros((1, LANES), f32).at[0, :v.shape[0]].set(v.astype(f32))


def _mixer(x3, norm_mix, w_in, lru_conv_w, lru_conv_b, lru_wa, lru_ba, lru_wi, lru_bi, lru_lambda,
           ssd_conv_w, ssd_conv_b, ssd_a_log, ssd_dt_bias, ssd_d):
    b, s, d = x3.shape
    x2d = x3.reshape(b * s, d)
    pad = jnp.zeros((d, LANES - 2 * SSD_HEADS), bf16)
    w_bf = jnp.concatenate([w_in.astype(bf16), pad], axis=1)
    xr, gate, z, xbc, dt = _in_proj(x2d, norm_mix, w_bf)
    xr = xr.reshape(b, s, LRU_WIDTH)
    xbc = xbc.reshape(b, s, SSD_CONV_CH)
    dt = dt.reshape(b, s, LANES)

    hs = []
    for dr in range(2):
        w_gate = jnp.concatenate([lru_wa[dr], lru_wi[dr]], axis=-1).astype(bf16)
        hs.append(_lru_scan(xr, lru_conv_w, lru_conv_b, w_gate, lru_ba[dr], lru_bi[dr], lru_lambda[dr],
                            reverse=bool(dr)))
    alog_pad = _pad_lanes(ssd_a_log.reshape(-1))
    dtb_pad = _pad_lanes(ssd_dt_bias.reshape(-1))
    dskip_pad = _pad_lanes(ssd_d)
    ys = [_ssd_scan(xbc, dt, ssd_conv_w, ssd_conv_b, alog_pad, dtb_pad, dskip_pad, reverse=bool(dr))
          for dr in range(2)]
    t = b * s
    return (hs[0].reshape(t, -1), hs[1].reshape(t, -1), gate, ys[0].reshape(t, -1), ys[1].reshape(t, -1), z)


def _layer(x3, norm_mix, w_in, lru_conv_w, lru_conv_b, lru_wa, lru_ba, lru_wi, lru_bi, lru_lambda,
           ssd_conv_w, ssd_conv_b, ssd_a_log, ssd_dt_bias, ssd_d, ssd_norm, w_out, norm_ffn, w_router,
           w_gate, w_up, w_down):
    b, s, d = x3.shape
    cap = max(1, CAPACITY_FACTOR * s // N_EXPERTS)
    assert s % LANES == 0 and cap % LANES == 0, "sequence length must give 128-aligned expert capacity"
    hf, hb, gate, yf, yb, z = _mixer(x3, norm_mix, w_in, lru_conv_w, lru_conv_b, lru_wa, lru_ba, lru_wi,
                                     lru_bi, lru_lambda, ssd_conv_w, ssd_conv_b, ssd_a_log, ssd_dt_bias, ssd_d)
    wr = jnp.zeros((d, LANES), f32).at[:, :N_EXPERTS].set(w_router)
    wr_hi = wr.astype(bf16)
    wr_lo = (wr - wr_hi.astype(f32)).astype(bf16)
    xn, hn, logits = _mix_out(x3.reshape(b * s, d), hf, hb, gate, yf, yb, z, ssd_norm, w_out.astype(bf16),
                              norm_ffn, wr_hi, wr_lo)
    idx, gates, lpos, tab = _route(logits.reshape(b, s, LANES), cap)
    yg = _moe_ffn(hn, idx, gates, w_gate.astype(bf16), w_up.astype(bf16), w_down.astype(bf16))
    return _combine(xn, lpos, tab, yg, b).reshape(b, s, d)


def kernel(x, norm_mix, w_in, lru_conv_w, lru_conv_b, lru_wa, lru_ba, lru_wi, lru_bi, lru_lambda, ssd_conv_w, ssd_conv_b, ssd_a_log, ssd_dt_bias, ssd_d, ssd_norm, w_out, norm_ffn, w_router, w_gate, w_up, w_down, norm_final):
    depth = norm_mix.shape[0]
    for l in range(depth):
        x = _layer(x, norm_mix[l], w_in[l], lru_conv_w[l], lru_conv_b[l], lru_wa[l], lru_ba[l], lru_wi[l],
                   lru_bi[l], lru_lambda[l], ssd_conv_w[l], ssd_conv_b[l], ssd_a_log[l], ssd_dt_bias[l],
                   ssd_d[l], ssd_norm[l], w_out[l], norm_ffn[l], w_router[l], w_gate[l], w_up[l], w_down[l])
    b, s, d = x.shape
    return _final_norm(x.reshape(b * s, d), norm_final).reshape(b, s, d)
```

```python
import functools

import jax
import jax.numpy as jnp
from jax import lax
from jax.experimental import pallas as pl
from jax.experimental.pallas import tpu as pltpu

f32 = jnp.float32
bf16 = jnp.bfloat16
i32 = jnp.int32

D_MODEL = 1024
EPS = 1e-6
CONV_WIDTH = 4
LRU_WIDTH = 1024
LRU_HEADS = 8
LRU_BLOCK = 128
LRU_C = 8.0
SSD_WIDTH = 1024
SSD_HEADDIM = 64
SSD_HEADS = 16
SSD_GROUPS = 4
SSD_STATE = 128
SSD_GN = SSD_GROUPS * SSD_STATE
SSD_CONV_CH = SSD_WIDTH + 2 * SSD_GN
GROUP_WIDTH = SSD_WIDTH // SSD_GROUPS
N_EXPERTS = 16
CAPACITY_FACTOR = 2
D_FF = 2048

LANES = 128
SUBLANES = 8
HALO = SUBLANES
WINDOW = LANES + SUBLANES
VMEM_LIMIT = 56 * 1024 * 1024


def _pick(n, target):
    if n <= target:
        return n
    t = target
    while t >= LANES:
        if n % t == 0:
            return t
        t -= LANES
    return n


def _params(sem, vmem=None):
    return pltpu.CompilerParams(dimension_semantics=sem, vmem_limit_bytes=vmem)


IN_SEGMENTS = (LRU_WIDTH, LRU_WIDTH, SSD_WIDTH, SSD_CONV_CH, LANES)


def _conv_centred(pe, first, last, cw, cb):
    tm = pe.shape[0] - 2 * HALO
    before = jnp.where(first, 0.0, pe[:HALO])
    after = jnp.where(last, 0.0, pe[HALO + tm:])
    c = pe[HALO:HALO + tm]
    r8 = lax.broadcasted_iota(i32, (SUBLANES, pe.shape[1]), 0)
    back1 = pltpu.roll(c, 1, 0)
    fwd1 = pltpu.roll(c, tm - 1, 0)
    fwd2 = pltpu.roll(c, tm - 2, 0)
    back1 = jnp.concatenate([jnp.where(r8 == 0, before[HALO - 1:HALO], back1[:HALO]), back1[HALO:]], axis=0)
    fwd1 = jnp.concatenate([fwd1[:tm - HALO], jnp.where(r8 == HALO - 1, after[0:1], fwd1[tm - HALO:])], axis=0)
    tail2 = jnp.where(r8 == HALO - 2, after[0:1], jnp.where(r8 == HALO - 1, after[1:2], fwd2[tm - HALO:]))
    fwd2 = jnp.concatenate([fwd2[:tm - HALO], tail2], axis=0)
    return cb + back1 * cw[0:1] + c * cw[1:2] + fwd1 * cw[2:3] + fwd2 * cw[3:4]


def _in_proj_body(prev_ref, x_ref, next_ref, g_ref, w_ref, lcw_ref, lcb_ref, scw_ref, scb_ref,
                  xc_ref, gate_ref, z_ref, xa_ref, dt_ref, *, tiles_per_seq):
    pos = lax.rem(pl.program_id(0), tiles_per_seq)
    first = pos == 0
    last = pos == tiles_per_seq - 1
    tm = x_ref.shape[0]
    xe = jnp.concatenate([prev_ref[...], x_ref[...], next_ref[...]], axis=0)
    ms = jnp.mean(xe * xe, axis=-1, keepdims=True)
    hn = (xe * lax.rsqrt(ms + EPS) * g_ref[...]).astype(bf16)
    hc = hn[HALO:HALO + tm]
    o0, o1, o2, o3 = LRU_WIDTH, 2 * LRU_WIDTH, 2 * LRU_WIDTH + SSD_WIDTH, 2 * LRU_WIDTH + SSD_WIDTH + SSD_CONV_CH
    pe = jnp.dot(hn, w_ref[:, :o0], preferred_element_type=f32)
    xc_ref[...] = _conv_centred(pe, first, last, lcw_ref[...], lcb_ref[...]).astype(bf16)
    gate_ref[...] = jnp.dot(hc, w_ref[:, o0:o1], preferred_element_type=f32).astype(bf16)
    z_ref[...] = jnp.dot(hc, w_ref[:, o1:o2], preferred_element_type=f32).astype(bf16)
    pe = jnp.dot(hn, w_ref[:, o2:o3], preferred_element_type=f32)
    xa_ref[...] = jax.nn.silu(_conv_centred(pe, first, last, scw_ref[...], scb_ref[...])).astype(bf16)
    dt_ref[...] = jnp.dot(hc, w_ref[:, o3:], preferred_element_type=f32)


def _in_proj(x2d, g, w_bf16, lru_cw, lru_cb, ssd_cw, ssd_cb, seq):
    t, d = x2d.shape
    n = w_bf16.shape[1]
    tm = _pick(seq, 512)
    per = tm // HALO
    last_blk = t // HALO - 1
    full = lambda shape: pl.BlockSpec(shape, lambda i: (0,) * len(shape))
    return pl.pallas_call(
        functools.partial(_in_proj_body, tiles_per_seq=seq // tm),
        grid=(t // tm,),
        in_specs=[
            pl.BlockSpec((HALO, d), lambda i: (jnp.maximum(i * per - 1, 0), 0)),
            pl.BlockSpec((tm, d), lambda i: (i, 0)),
            pl.BlockSpec((HALO, d), lambda i: (jnp.minimum((i + 1) * per, last_blk), 0)),
            full((1, d)),
            pl.BlockSpec((d, n), lambda i: (0, 0), pipeline_mode=pl.Buffered(1)),
            full((CONV_WIDTH, LRU_WIDTH)), full((1, LRU_WIDTH)),
            full((CONV_WIDTH, SSD_CONV_CH)), full((1, SSD_CONV_CH)),
        ],
        out_specs=[pl.BlockSpec((tm, w), lambda i: (i, 0)) for w in IN_SEGMENTS],
        out_shape=[jax.ShapeDtypeStruct((t, w), dt) for w, dt in zip(IN_SEGMENTS, (bf16, bf16, bf16, bf16, f32))],
        compiler_params=_params(("parallel",), VMEM_LIMIT),
        name="in_proj",
    )(x2d, x2d, x2d, g.reshape(1, d), w_bf16, lru_cw, lru_cb.reshape(1, -1), ssd_cw, ssd_cb.reshape(1, -1))


def _chunk_index(c, nc, reverse):
    return (nc - 1 - c) if reverse else c


def _scan_rows8(a, u, reverse):
    r = lax.broadcasted_iota(i32, a.shape, 1)
    for k in (1, 2, 4):
        shift = (SUBLANES - k) if reverse else k
        valid = (r < SUBLANES - k) if reverse else (r >= k)
        a_s = pltpu.roll(a, shift, 1)
        u_s = pltpu.roll(u, shift, 1)
        u = jnp.where(valid, a * u_s + u, u)
        a = jnp.where(valid, a * a_s, a)
    return a, u


def _lru_body(x_ref, w_ref, ba_ref, bi_ref, lam_ref, o_ref, carry_ref, a_scr, u_scr, *, reverse):
    c = pl.program_id(1)
    tc = x_ref.shape[1]
    groups = tc // SUBLANES

    @pl.when(c == 0)
    def _():
        carry_ref[...] = jnp.zeros_like(carry_ref)

    xc = x_ref[0]
    sp = jax.nn.softplus(-lam_ref[...])
    for h in range(LRU_HEADS):
        sl = slice(h * LRU_BLOCK, (h + 1) * LRU_BLOCK)
        pre = jnp.dot(xc[:, sl], w_ref[h], preferred_element_type=f32)
        xh = xc[:, sl].astype(f32)
        r = jax.nn.sigmoid(pre[:, :LRU_BLOCK] + ba_ref[:, sl])
        gi = jax.nn.sigmoid(pre[:, LRU_BLOCK:] + bi_ref[:, sl])
        log_a = (-LRU_C) * r * sp[:, sl]
        a = jnp.exp(log_a)
        u = jnp.sqrt(1.0 - a * a) * (gi * xh)
        a3, u3 = _scan_rows8(a.reshape(groups, SUBLANES, LRU_BLOCK),
                             u.reshape(groups, SUBLANES, LRU_BLOCK), reverse)
        a_scr[:, sl] = a3.reshape(tc, LRU_BLOCK)
        u_scr[:, sl] = u3.reshape(tc, LRU_BLOCK)

    pair = 2 * SUBLANES

    def step(g, carry):
        gi_ = (groups // 2 - 1 - g) if reverse else g
        row = pl.multiple_of(gi_ * pair, pair)
        halves = [None, None]
        for k in ((1, 0) if reverse else (0, 1)):
            rows = pl.ds(pl.multiple_of(row + k * SUBLANES, SUBLANES), SUBLANES)
            hblk = u_scr[rows, :] + a_scr[rows, :] * carry
            halves[k] = hblk
            carry = hblk[0:1] if reverse else hblk[SUBLANES - 1:SUBLANES]
        o_ref[0, pl.ds(row, pair), :] = jnp.concatenate(halves, axis=0).astype(bf16)
        return carry

    carry_ref[...] = lax.fori_loop(0, groups // 2, step, carry_ref[...], unroll=2)


def _lru_scan(xc, w_gate, ba, bi, lam, reverse):
    b, s, w = xc.shape
    tc = _pick(s, 256)
    nc = s // tc
    full = lambda shape: pl.BlockSpec(shape, lambda b_, c_: (0,) * len(shape))
    chunk = pl.BlockSpec((1, tc, w), lambda b_, c_: (b_, _chunk_index(c_, nc, reverse), 0))
    return pl.pallas_call(
        functools.partial(_lru_body, reverse=reverse),
        grid=(b, nc),
        in_specs=[chunk, full((LRU_HEADS, LRU_BLOCK, 2 * LRU_BLOCK)), full((1, w)), full((1, w)), full((1, w))],
        out_specs=chunk,
        out_shape=jax.ShapeDtypeStruct((b, s, w), bf16),
        scratch_shapes=[pltpu.VMEM((1, w), f32), pltpu.VMEM((tc, w), f32), pltpu.VMEM((tc, w), f32)],
        compiler_params=_params(("parallel", "arbitrary")),
        name="lru_bwd" if reverse else "lru_fwd",
    )(xc, w_gate, ba.reshape(1, w), bi.reshape(1, w), lam.reshape(1, w))


def _expand_heads(arr, base):
    rows = arr.shape[0]
    lane = lax.broadcasted_iota(i32, (rows, LANES), 1)
    tiles = []
    for k in range(SSD_HEADS // 2):
        c0 = arr[:, base + 2 * k:base + 2 * k + 1]
        c1 = arr[:, base + 2 * k + 1:base + 2 * k + 2]
        tiles.append(jnp.where(lane < SSD_HEADDIM, c0, c1))
    return jnp.concatenate(tiles, axis=1)


def _cumsum_rows(x, reverse):
    n = x.shape[0]
    r = lax.broadcasted_iota(i32, x.shape, 0)
    k = 1
    while k < n:
        if reverse:
            x = x + jnp.where(r < n - k, pltpu.roll(x, n - k, 0), 0.0)
        else:
            x = x + jnp.where(r >= k, pltpu.roll(x, k, 0), 0.0)
        k *= 2
    return x


def _ssd_body(x_ref, dt_ref, alog_ref, dtb_ref, dskip_ref, o_ref, state_ref, *, reverse):
    c = pl.program_id(1)
    L = x_ref.shape[1]
    base = SSD_HEADS if reverse else 0

    @pl.when(c == 0)
    def _():
        state_ref[...] = jnp.zeros_like(state_ref)

    xact = x_ref[0]
    xs = xact[:, :SSD_WIDTH].astype(f32)

    dt = jax.nn.softplus(dt_ref[0] + dtb_ref[...])
    d_a = dt * (-jnp.exp(alog_ref[...]))
    cum = _cumsum_rows(d_a, reverse)
    edge = cum[0:1] if reverse else cum[L - 1:L]
    cum_t = cum.T

    dtx = _expand_heads(dt, base) * xs
    e_cum = _expand_heads(jnp.exp(cum), base)
    e_end = _expand_heads(jnp.exp(edge - cum), base)
    e_edge = _expand_heads(jnp.exp(edge), base)
    w_all = (e_end * dtx).astype(bf16)
    dtx_b = dtx.astype(bf16)

    li = lax.broadcasted_iota(i32, (L, L), 0)
    si = lax.broadcasted_iota(i32, (L, L), 1)
    tri = (si >= li) if reverse else (li >= si)
    lane = lax.broadcasted_iota(i32, (L, LANES), 1)
    lo_half = lane < SSD_HEADDIM

    outs = []
    for g in range(SSD_GROUPS):
        bsl = slice(SSD_WIDTH + g * SSD_STATE, SSD_WIDTH + (g + 1) * SSD_STATE)
        csl = slice(SSD_WIDTH + SSD_GN + g * SSD_STATE, SSD_WIDTH + SSD_GN + (g + 1) * SSD_STATE)
        gsl = slice(g * GROUP_WIDTH, (g + 1) * GROUP_WIDTH)
        bm = xact[:, bsl]
        cm = xact[:, csl]
        cb = lax.dot_general(cm, bm, (((1,), (1,)), ((), ())), preferred_element_type=f32)
        s_old = state_ref[g]
        y_off = jnp.dot(cm, s_old.astype(bf16), preferred_element_type=f32) * e_cum[:, gsl]
        tiles = []
        for p in range(2):
            acc = None
            for q in range(2):
                j = base + g * 4 + 2 * p + q
                seg = cum[:, j:j + 1] - cum_t[j:j + 1, :]
                decay = jnp.exp(jnp.where(tri, seg, -jnp.inf))
                m = (cb * decay).astype(bf16)
                tsl = slice(g * GROUP_WIDTH + p * LANES, g * GROUP_WIDTH + (p + 1) * LANES)
                rhs = jnp.where(lo_half if q == 0 else jnp.logical_not(lo_half), dtx_b[:, tsl], 0.0)
                part = jnp.dot(m, rhs.astype(bf16), preferred_element_type=f32)
                acc = part if acc is None else acc + part
            tiles.append(acc)
        outs.append(jnp.concatenate(tiles, axis=1) + y_off)
        upd = lax.dot_general(bm, w_all[:, gsl], (((0,), (0,)), ((), ())), preferred_element_type=f32)
        state_ref[g] = s_old * e_edge[:, gsl] + upd
    y = jnp.concatenate(outs, axis=1)
    if not reverse:
        y = y + _expand_heads(dskip_ref[...], 0) * xs
    o_ref[0] = y.astype(bf16)


def _ssd_scan(xact, dt_pad, alog_pad, dtb_pad, dskip_pad, reverse):
    b, s, w = xact.shape
    L = _pick(s, 128)
    nc = s // L
    full = lambda shape: pl.BlockSpec(shape, lambda b_, c_: (0,) * len(shape))
    chunk = lambda width: pl.BlockSpec((1, L, width), lambda b_, c_: (b_, _chunk_index(c_, nc, reverse), 0))
    return pl.pallas_call(
        functools.partial(_ssd_body, reverse=reverse),
        grid=(b, nc),
        in_specs=[chunk(w), chunk(LANES), full((1, LANES)), full((1, LANES)), full((1, LANES))],
        out_specs=chunk(SSD_WIDTH),
        out_shape=jax.ShapeDtypeStruct((b, s, SSD_WIDTH), bf16),
        scratch_shapes=[pltpu.VMEM((SSD_GROUPS, SSD_STATE, GROUP_WIDTH), f32)],
        compiler_params=_params(("parallel", "arbitrary")),
        name="ssd_bwd" if reverse else "ssd_fwd",
    )(xact, dt_pad, alog_pad, dtb_pad, dskip_pad)


def _rms(x, g):
    ms = jnp.mean(x * x, axis=-1, keepdims=True)
    return x * lax.rsqrt(ms + EPS) * g


def _mix_out_body(x_ref, hf_ref, hb_ref, gate_ref, yf_ref, yb_ref, z_ref, gn_ref, wo_ref,
                  gf_ref, wrh_ref, wrl_ref, xo_ref, hn_ref, lg_ref):
    up = lambda ref: ref[...].astype(f32)
    y_lru = (up(hf_ref) + up(hb_ref)) * jax.nn.gelu(up(gate_ref))
    y = (up(yf_ref) + up(yb_ref)) * jax.nn.silu(up(z_ref))
    parts = []
    for g in range(SSD_GROUPS):
        yg = y[:, g * GROUP_WIDTH:(g + 1) * GROUP_WIDTH]
        ms = jnp.mean(yg * yg, axis=-1, keepdims=True)
        parts.append(yg * lax.rsqrt(ms + EPS))
    y_ssd = jnp.concatenate(parts, axis=1) * gn_ref[...]
    mix = jnp.concatenate([y_lru, y_ssd], axis=1).astype(bf16)
    xn = x_ref[...] + jnp.dot(mix, wo_ref[...], preferred_element_type=f32)
    xo_ref[...] = xn
    hn = _rms(xn, gf_ref[...])
    tm = hn.shape[0]
    for k in range(hn.shape[1] // LANES):
        hn_ref[pl.ds(k, tm, stride=SUBLANES), :] = hn[:, k * LANES:(k + 1) * LANES]
    h_hi = hn.astype(bf16)
    h_lo = (hn - h_hi.astype(f32)).astype(bf16)
    lg = jnp.dot(h_hi, wrh_ref[...], preferred_element_type=f32)
    lg = lg + jnp.dot(h_lo, wrh_ref[...], preferred_element_type=f32)
    lg = lg + jnp.dot(h_hi, wrl_ref[...], preferred_element_type=f32)
    lg_ref[...] = lg


def _mix_out(x2d, hf, hb, gate, yf, yb, z, ssd_norm, w_out_bf16, norm_ffn, wr_hi, wr_lo):
    t, d = x2d.shape
    tm = _pick(t, 256)
    row = lambda w: pl.BlockSpec((tm, w), lambda i: (i, 0))
    full = lambda shape: pl.BlockSpec(shape, lambda i: (0,) * len(shape))
    return pl.pallas_call(
        _mix_out_body,
        grid=(t // tm,),
        in_specs=[row(d)] + [row(LRU_WIDTH)] * 3 + [row(SSD_WIDTH)] * 3 + [
            full((1, SSD_WIDTH)), full((LRU_WIDTH + SSD_WIDTH, d)), full((1, d)),
            full((d, LANES)), full((d, LANES)),
        ],
        out_specs=[row(d), pl.BlockSpec((tm * d // LANES, LANES), lambda i: (i, 0)), row(LANES)],
        out_shape=[jax.ShapeDtypeStruct((t, d), f32), jax.ShapeDtypeStruct((t * d // LANES, LANES), f32),
                   jax.ShapeDtypeStruct((t, LANES), f32)],
        compiler_params=_params(("parallel",), VMEM_LIMIT),
        name="mix_out",
    )(x2d, hf, hb, gate, yf, yb, z, ssd_norm.reshape(1, -1), w_out_bf16, norm_ffn.reshape(1, d),
      wr_hi, wr_lo)


def _tile_prefix(tiles, upper_incl, upper_strict, lane):
    incs = [jnp.dot(t.astype(bf16), upper_incl, preferred_element_type=f32) for t in tiles]
    tot = jnp.zeros((N_EXPERTS, LANES), f32)
    for j, inc in enumerate(incs):
        tot = jnp.where(lane == j, inc[:, LANES - 1:LANES], tot)
    start = jnp.dot(tot.astype(bf16), upper_strict, preferred_element_type=f32)
    return incs, tot, start


def _route_body(lg_ref, idx_ref, gate_ref, lpos_ref, tab_ref, lm_scr, vt_scr, list_scr, tabv_scr, tabs_scr, sem,
                *, cap):
    s = lg_ref.shape[1]
    nt = s // LANES
    lt = lg_ref[0].T[:N_EXPERTS]
    mx = jnp.max(lt, axis=0, keepdims=True)
    ex = jnp.exp(lt - mx)
    aff = ex / jnp.sum(ex, axis=0, keepdims=True)
    key = pltpu.bitcast(aff, i32)

    def search(i, thr):
        cand = thr | (jnp.int32(1) << (30 - i))
        cnt = jnp.sum((key >= cand).astype(f32), axis=1, keepdims=True)
        return jnp.where(cnt >= float(cap), cand, thr)

    thr = lax.fori_loop(0, 31, search, jnp.zeros((N_EXPERTS, 1), i32))
    gt = key > thr
    eq = key == thr
    need = float(cap) - jnp.sum(gt.astype(f32), axis=1, keepdims=True)

    sub_i = lax.broadcasted_iota(i32, (LANES, LANES), 0)
    lane_i = lax.broadcasted_iota(i32, (LANES, LANES), 1)
    upper_incl = (sub_i <= lane_i).astype(bf16)
    upper_strict = (sub_i < lane_i).astype(bf16)
    lane_e = lax.broadcasted_iota(i32, (N_EXPERTS, LANES), 1)
    tiles = lambda a: [a[:, j * LANES:(j + 1) * LANES] for j in range(nt)]

    eq_t = tiles(eq.astype(f32))
    incs, _, start = _tile_prefix(eq_t, upper_incl, upper_strict, lane_e)
    sel_t = []
    for j, (gtj, eqj) in enumerate(zip(tiles(gt), eq_t)):
        excl = incs[j] - eqj + start[:, j:j + 1]
        sel_t.append(jnp.logical_or(gtj, jnp.logical_and(eqj > 0.5, excl < need)).astype(f32))

    incs, tot, start = _tile_prefix(sel_t, upper_incl, upper_strict, lane_e)
    start8 = jnp.floor(start * (1.0 / SUBLANES)) * float(SUBLANES)
    tab_ref[0, 0] = start8.astype(i32)
    tab_ref[0, 1] = tot.astype(i32)
    tabv_scr[...] = start.astype(i32)
    to_smem = pltpu.make_async_copy(tabv_scr, tabs_scr, sem)
    to_smem.start()

    kind = lax.broadcasted_iota(i32, (SUBLANES, LANES), 0)
    tok_lane = lax.broadcasted_iota(i32, (SUBLANES, LANES), 1).astype(f32)
    fill = jnp.full((LANES - N_EXPERTS, LANES), -1.0, f32)
    zrows = jnp.zeros((LANES - 3 * N_EXPERTS - SUBLANES, LANES), f32)
    for j, a in enumerate(tiles(aff)):
        lm = jnp.where(sel_t[j] > 0.5, incs[j] - sel_t[j], -1.0)
        lm_scr[j] = lm
        shifted = jnp.where(sel_t[j] > 0.5, lm + (start[:, j:j + 1] - start8[:, j:j + 1]), -1.0)
        lpos_ref[0, pl.ds(j * LANES, LANES), :] = jnp.concatenate([shifted, fill], axis=0).T
        a_hi = a.astype(bf16).astype(f32)
        a_mid = (a - a_hi).astype(bf16).astype(f32)
        a_lo = (a - a_hi - a_mid).astype(bf16).astype(f32)
        tok = jnp.where(kind == 0, float(j), jnp.where(kind == 1, tok_lane, 0.0))
        vt_scr[j] = jnp.concatenate([a_hi, a_mid, a_lo, tok, zrows], axis=0).T.astype(bf16)

    to_smem.wait()
    rank = sub_i.astype(f32)

    def compact(j, carry):
        lm = lm_scr[j]
        vt = vt_scr[j]
        for e in range(N_EXPERTS):
            onehot = (lm[e:e + 1] == rank).astype(bf16)
            packed = jnp.dot(onehot, vt, preferred_element_type=f32)
            list_scr[e, pl.ds(tabs_scr[e, j], LANES), :] = packed
        return carry

    lax.fori_loop(0, nt, compact, 0)

    lane_c = lax.broadcasted_iota(i32, (cap, LANES), 1)
    idx_c = jnp.zeros((cap, LANES), f32)
    gate_c = jnp.zeros((cap, LANES), f32)
    for e in range(N_EXPERTS):
        rows = list_scr[e, 0:cap, :]
        g = rows[:, e:e + 1] + rows[:, N_EXPERTS + e:N_EXPERTS + e + 1] + rows[:, 2 * N_EXPERTS + e:2 * N_EXPERTS + e + 1]
        t = rows[:, 3 * N_EXPERTS:3 * N_EXPERTS + 1] * float(LANES) + rows[:, 3 * N_EXPERTS + 1:3 * N_EXPERTS + 2]
        idx_c = jnp.where(lane_c == e, t, idx_c)
        gate_c = jnp.where(lane_c == e, g, gate_c)
    idx_ref[0] = idx_c.T[:N_EXPERTS].astype(i32)
    gate_ref[0] = gate_c.T[:N_EXPERTS]


def _route(logits, cap):
    b, s, _ = logits.shape
    nt = s // LANES
    assert nt <= LANES
    return pl.pallas_call(
        functools.partial(_route_body, cap=cap),
        grid=(b,),
        in_specs=[pl.BlockSpec((1, s, LANES), lambda i: (i, 0, 0))],
        out_specs=[pl.BlockSpec((1, N_EXPERTS, cap), lambda i: (i, 0, 0)),
                   pl.BlockSpec((1, N_EXPERTS, cap), lambda i: (i, 0, 0)),
                   pl.BlockSpec((1, s, LANES), lambda i: (i, 0, 0)),
                   pl.BlockSpec((1, 2, N_EXPERTS, LANES), lambda i: (i, 0, 0, 0))],
        out_shape=[jax.ShapeDtypeStruct((b, N_EXPERTS, cap), i32),
                   jax.ShapeDtypeStruct((b, N_EXPERTS, cap), f32),
                   jax.ShapeDtypeStruct((b, s, LANES), f32),
                   jax.ShapeDtypeStruct((b, 2, N_EXPERTS, LANES), i32)],
        scratch_shapes=[pltpu.VMEM((nt, N_EXPERTS, LANES), f32),
                        pltpu.VMEM((nt, LANES, LANES), bf16),
                        pltpu.VMEM((N_EXPERTS, cap + LANES, LANES), f32),
                        pltpu.VMEM((N_EXPERTS, LANES), i32),
                        pltpu.SMEM((N_EXPERTS, LANES), i32),
                        pltpu.SemaphoreType.DMA],
        compiler_params=_params(("parallel",), VMEM_LIMIT),
        name="route",
    )(logits)


def _token_copy(src_hbm, tok, dst, r, sem):
    src = src_hbm.at[pl.ds(pl.multiple_of(tok * SUBLANES, SUBLANES), SUBLANES), :]
    return pltpu.make_async_copy(src, dst.at[pl.ds(pl.multiple_of(r * SUBLANES, SUBLANES), SUBLANES), :], sem)


def _moe_body(idx_ref, idxn_ref, gate_ref, wg_ref, wu_ref, wd_ref, hn_hbm, o_ref, xg, sem, *, cap, seq, nb):
    n = pl.program_id(0) * nb + pl.program_id(1)
    total = N_EXPERTS * nb
    slot = lax.rem(n, 2)
    other = 1 - slot

    @pl.when(n == 0)
    def _():
        def start(r, carry):
            _token_copy(hn_hbm, idx_ref[0, 0, 0, r], xg.at[slot], r, sem.at[slot]).start()
            return carry

        lax.fori_loop(0, cap, start, 0, unroll=8)

    def wait_all(buf):
        def wait(r, carry):
            _token_copy(hn_hbm, 0, xg.at[buf], r, sem.at[buf]).wait()
            return carry

        lax.fori_loop(0, cap, wait, 0, unroll=8)

    wait_all(slot)

    base_next = lax.rem(jnp.minimum(n + 1, total - 1), nb) * seq
    tm = min(cap, 256)
    ntile = wg_ref.shape[1] // LANES
    for m in range(cap // tm):
        rows = slice(m * tm, (m + 1) * tm)
        xm = jnp.concatenate(
            [xg[slot, pl.ds(m * tm * SUBLANES + k, tm, stride=SUBLANES), :] for k in range(ntile)],
            axis=1).astype(bf16)
        hg = jnp.dot(xm, wg_ref[0], preferred_element_type=f32)
        hu = jnp.dot(xm, wu_ref[0], preferred_element_type=f32)
        hid = (jax.nn.silu(hg) * hu).astype(bf16)
        y = jnp.dot(hid, wd_ref[0], preferred_element_type=f32) * gate_ref[0, 0, rows, :]
        o_ref[0, 0, rows, :] = _pack_halves(y)
        for r in range(m * tm, (m + 1) * tm):
            _token_copy(hn_hbm, base_next + idxn_ref[0, 0, 0, r], xg.at[other], r, sem.at[other]).start()
    o_ref[0, 0, cap:cap + WINDOW, :] = jnp.zeros((WINDOW, o_ref.shape[-1]), jnp.uint32)

    @pl.when(n == total - 1)
    def _():
        wait_all(other)


def _pack_halves(y):
    w = y.shape[1] // 2
    hi = pltpu.bitcast(y[:, :w].astype(bf16).astype(f32), jnp.uint32)
    lo = pltpu.bitcast(y[:, w:].astype(bf16).astype(f32), jnp.uint32)
    return hi | (lo >> 16)


def _unpack_halves(p):
    hi = pltpu.bitcast(p & jnp.uint32(0xFFFF0000), f32)
    lo = pltpu.bitcast(p << 16, f32)
    return jnp.concatenate([hi, lo], axis=1).astype(bf16)


def _moe_ffn(hn_tiles, idx, gates, wg, wu, wd):
    b, _, cap = idx.shape
    d = wg.shape[1]
    s = hn_tiles.shape[0] * LANES // d // b
    idx4 = idx.reshape(b, N_EXPERTS, 1, cap)
    gates4 = gates.reshape(b, N_EXPERTS, cap, 1)
    def next_step(e, i):
        n1 = jnp.minimum(e * b + i + 1, N_EXPERTS * b - 1)
        return (lax.rem(n1, b), n1 // b, 0, 0)

    return pl.pallas_call(
        functools.partial(_moe_body, cap=cap, seq=s, nb=b),
        grid=(N_EXPERTS, b),
        in_specs=[
            pl.BlockSpec((1, 1, 1, cap), lambda e, i: (i, e, 0, 0), memory_space=pltpu.SMEM),
            pl.BlockSpec((1, 1, 1, cap), next_step, memory_space=pltpu.SMEM),
            pl.BlockSpec((1, 1, cap, 1), lambda e, i: (i, e, 0, 0)),
            pl.BlockSpec((1, d, D_FF), lambda e, i: (e, 0, 0)),
            pl.BlockSpec((1, d, D_FF), lambda e, i: (e, 0, 0)),
            pl.BlockSpec((1, D_FF, d), lambda e, i: (e, 0, 0)),
            pl.BlockSpec(memory_space=pl.ANY),
        ],
        out_specs=pl.BlockSpec((1, 1, cap + WINDOW, d // 2), lambda e, i: (i, e, 0, 0)),
        out_shape=jax.ShapeDtypeStruct((b, N_EXPERTS, cap + WINDOW, d // 2), jnp.uint32),
        scratch_shapes=[pltpu.VMEM((2, cap * d // LANES, LANES), f32), pltpu.SemaphoreType.DMA((2,))],
        compiler_params=_params(("arbitrary", "arbitrary"), VMEM_LIMIT),
        name="moe_ffn",
    )(idx4, idx4, gates4, wg, wu, wd, hn_tiles)


def _window_copy(yg_hbm, b, e, start, win, slot, sem):
    src = yg_hbm.at[b, e, pl.ds(pl.multiple_of(start, SUBLANES), WINDOW), :]
    return pltpu.make_async_copy(src, win.at[slot, e], sem.at[slot])


def _combine_body(tab_ref, tabn_ref, lpos_ref, x_ref, yg_hbm, o_ref, win, sem, *, nb, nt):
    b = pl.program_id(0)
    j = pl.program_id(1)
    n = b * nt + j
    slot = lax.rem(n, 2)

    @pl.when(n == 0)
    def _():
        for e in range(N_EXPERTS):
            _window_copy(yg_hbm, b, e, tab_ref[0, 0, e, j], win, slot, sem).start()

    @pl.when(n + 1 < nb * nt)
    def _():
        wrap = j + 1 == nt
        jn = jnp.where(wrap, 0, j + 1)
        bn = jnp.where(wrap, b + 1, b)
        for e in range(N_EXPERTS):
            _window_copy(yg_hbm, bn, e, tabn_ref[0, 0, e, jn], win, 1 - slot, sem).start()

    for e in range(N_EXPERTS):
        _window_copy(yg_hbm, b, e, 0, win, slot, sem).wait()

    lm = lpos_ref[0]
    row = lax.broadcasted_iota(i32, (LANES, WINDOW), 1).astype(f32)
    acc = x_ref[...]
    for e in range(N_EXPERTS):
        place = (lm[:, e:e + 1] == row).astype(bf16)
        acc = acc + jnp.dot(place, _unpack_halves(win[slot, e]), preferred_element_type=f32)
    o_ref[...] = acc


def _combine(x2d, lpos, tab, yg, nb):
    t, d = x2d.shape
    nt = t // nb // LANES

    def next_batch(i, j):
        return (jnp.minimum(i + (j + 1) // nt, nb - 1), 0, 0, 0)

    return pl.pallas_call(
        functools.partial(_combine_body, nb=nb, nt=nt),
        grid=(nb, nt),
        in_specs=[
            pl.BlockSpec((1, 2, N_EXPERTS, LANES), lambda i, j: (i, 0, 0, 0), memory_space=pltpu.SMEM),
            pl.BlockSpec((1, 2, N_EXPERTS, LANES), next_batch, memory_space=pltpu.SMEM),
            pl.BlockSpec((1, LANES, LANES), lambda i, j: (i, j, 0)),
            pl.BlockSpec((LANES, d), lambda i, j: (i * nt + j, 0)),
            pl.BlockSpec(memory_space=pl.ANY),
        ],
        out_specs=pl.BlockSpec((LANES, d), lambda i, j: (i * nt + j, 0)),
        out_shape=jax.ShapeDtypeStruct((t, d), f32),
        scratch_shapes=[pltpu.VMEM((2, N_EXPERTS, WINDOW, d // 2), jnp.uint32), pltpu.SemaphoreType.DMA((2,))],
        compiler_params=_params(("arbitrary", "arbitrary")),
        name="moe_combine",
    )(tab, tab, lpos, x2d, yg)


def _final_norm_body(x_ref, g_ref, o_ref):
    o_ref[...] = _rms(x_ref[...], g_ref[...])


def _final_norm(x2d, g):
    t, d = x2d.shape
    tm = _pick(t, 1024)
    return pl.pallas_call(
        _final_norm_body,
        grid=(t // tm,),
        in_specs=[pl.BlockSpec((tm, d), lambda i: (i, 0)), pl.BlockSpec((1, d), lambda i: (0, 0))],
        out_specs=pl.BlockSpec((tm, d), lambda i: (i, 0)),
        out_shape=jax.ShapeDtypeStruct((t, d), f32),
        compiler_params=_params(("parallel",)),
        name="final_norm",
    )(x2d, g.reshape(1, d))


def _pad_lanes(v):
    return jnp.zeros((1, LANES), f32).at[0, :v.shape[0]].set(v.astype(f32))


def _mixer(x3, norm_mix, w_in, lru_conv_w, lru_conv_b, lru_wa, lru_ba, lru_wi, lru_bi, lru_lambda,
           ssd_conv_w, ssd_conv_b, ssd_a_log, ssd_dt_bias, ssd_d):
    b, s, d = x3.shape
    x2d = x3.reshape(b * s, d)
    pad = jnp.zeros((d, LANES - 2 * SSD_HEADS), bf16)
    w_bf = jnp.concatenate([w_in.astype(bf16), pad], axis=1)
    xc, gate, z, xact, dt = _in_proj(x2d, norm_mix, w_bf, lru_conv_w, lru_conv_b, ssd_conv_w, ssd_conv_b, s)
    xc = xc.reshape(b, s, LRU_WIDTH)
    xact = xact.reshape(b, s, SSD_CONV_CH)
    dt = dt.reshape(b, s, LANES)

    hs = []
    for dr in range(2):
        w_gate = jnp.concatenate([lru_wa[dr], lru_wi[dr]], axis=-1).astype(bf16)
        hs.append(_lru_scan(xc, w_gate, lru_ba[dr], lru_bi[dr], lru_lambda[dr], reverse=bool(dr)))
    alog_pad = _pad_lanes(ssd_a_log.reshape(-1))
    dtb_pad = _pad_lanes(ssd_dt_bias.reshape(-1))
    dskip_pad = _pad_lanes(ssd_d)
    ys = [_ssd_scan(xact, dt, alog_pad, dtb_pad, dskip_pad, reverse=bool(dr)) for dr in range(2)]
    t = b * s
    return (hs[0].reshape(t, -1), hs[1].reshape(t, -1), gate, ys[0].reshape(t, -1), ys[1].reshape(t, -1), z)


def _layer(x3, norm_mix, w_in, lru_conv_w, lru_conv_b, lru_wa, lru_ba, lru_wi, lru_bi, lru_lambda,
           ssd_conv_w, ssd_conv_b, ssd_a_log, ssd_dt_bias, ssd_d, ssd_norm, w_out, norm_ffn, w_router,
           w_gate, w_up, w_down):
    b, s, d = x3.shape
    cap = max(1, CAPACITY_FACTOR * s // N_EXPERTS)
    assert s % LANES == 0 and cap % LANES == 0, "sequence length must give 128-aligned expert capacity"
    hf, hb, gate, yf, yb, z = _mixer(x3, norm_mix, w_in, lru_conv_w, lru_conv_b, lru_wa, lru_ba, lru_wi,
                                     lru_bi, lru_lambda, ssd_conv_w, ssd_conv_b, ssd_a_log, ssd_dt_bias, ssd_d)
    wr = jnp.zeros((d, LANES), f32).at[:, :N_EXPERTS].set(w_router)
    wr_hi = wr.astype(bf16)
    wr_lo = (wr - wr_hi.astype(f32)).astype(bf16)
    xn, hn, logits = _mix_out(x3.reshape(b * s, d), hf, hb, gate, yf, yb, z, ssd_norm, w_out.astype(bf16),
                              norm_ffn, wr_hi, wr_lo)
    idx, gates, lpos, tab = _route(logits.reshape(b, s, LANES), cap)
    yg = _moe_ffn(hn, idx, gates, w_gate.astype(bf16), w_up.astype(bf16), w_down.astype(bf16))
    return _combine(xn, lpos, tab, yg, b).reshape(b, s, d)


def kernel(x, norm_mix, w_in, lru_conv_w, lru_conv_b, lru_wa, lru_ba, lru_wi, lru_bi, lru_lambda, ssd_conv_w, ssd_conv_b, ssd_a_log, ssd_dt_bias, ssd_d, ssd_norm, w_out, norm_ffn, w_router, w_gate, w_up, w_down, norm_final):
    depth = norm_mix.shape[0]
    for l in range(depth):
        x = _layer(x, norm_mix[l], w_in[l], lru_conv_w[l], lru_conv_b[l], lru_wa[l], lru_ba[l], lru_wi[l],
                   lru_bi[l], lru_lambda[l], ssd_conv_w[l], ssd_conv_b[l], ssd_a_log[l], ssd_dt_bias[l],
                   ssd_d[l], ssd_norm[l], w_out[l], norm_ffn[l], w_router[l], w_gate[l], w_up[l], w_down[l])
    b, s, d = x.shape
    return _final_norm(x.reshape(b * s, d), norm_final).reshape(b, s, d)
```

```python
import functools

import jax
import jax.numpy as jnp
from jax import lax
from jax.experimental import pallas as pl
from jax.experimental.pallas import tpu as pltpu

f32 = jnp.float32
bf16 = jnp.bfloat16
i32 = jnp.int32

D_MODEL = 1024
EPS = 1e-6
CONV_WIDTH = 4
LRU_WIDTH = 1024
LRU_HEADS = 8
LRU_BLOCK = 128
LRU_C = 8.0
SSD_WIDTH = 1024
SSD_HEADDIM = 64
SSD_HEADS = 16
SSD_GROUPS = 4
SSD_STATE = 128
SSD_GN = SSD_GROUPS * SSD_STATE
SSD_CONV_CH = SSD_WIDTH + 2 * SSD_GN
GROUP_WIDTH = SSD_WIDTH // SSD_GROUPS
N_EXPERTS = 16
CAPACITY_FACTOR = 2
D_FF = 2048

LANES = 128
SUBLANES = 8
HALO = SUBLANES
WINDOW = LANES + SUBLANES
VMEM_LIMIT = 56 * 1024 * 1024


def _pick(n, target):
    if n <= target:
        return n
    t = target
    while t >= LANES:
        if n % t == 0:
            return t
        t -= LANES
    return n


def _params(sem, vmem=None):
    return pltpu.CompilerParams(dimension_semantics=sem, vmem_limit_bytes=vmem)


IN_SEGMENTS = (LRU_WIDTH, LRU_WIDTH, SSD_WIDTH, SSD_CONV_CH, LANES)


def _conv_centred(pe, first, last, cw, cb):
    tm = pe.shape[0] - 2 * HALO
    before = jnp.where(first, 0.0, pe[:HALO])
    after = jnp.where(last, 0.0, pe[HALO + tm:])
    c = pe[HALO:HALO + tm]
    r8 = lax.broadcasted_iota(i32, (SUBLANES, pe.shape[1]), 0)
    back1 = pltpu.roll(c, 1, 0)
    fwd1 = pltpu.roll(c, tm - 1, 0)
    fwd2 = pltpu.roll(c, tm - 2, 0)
    back1 = jnp.concatenate([jnp.where(r8 == 0, before[HALO - 1:HALO], back1[:HALO]), back1[HALO:]], axis=0)
    fwd1 = jnp.concatenate([fwd1[:tm - HALO], jnp.where(r8 == HALO - 1, after[0:1], fwd1[tm - HALO:])], axis=0)
    tail2 = jnp.where(r8 == HALO - 2, after[0:1], jnp.where(r8 == HALO - 1, after[1:2], fwd2[tm - HALO:]))
    fwd2 = jnp.concatenate([fwd2[:tm - HALO], tail2], axis=0)
    return cb + back1 * cw[0:1] + c * cw[1:2] + fwd1 * cw[2:3] + fwd2 * cw[3:4]


def _in_proj_body(prev_ref, x_ref, next_ref, g_ref, w_ref, lcw_ref, lcb_ref, scw_ref, scb_ref,
                  xc_ref, gate_ref, z_ref, xa_ref, dt_ref, *, tiles_per_seq):
    pos = lax.rem(pl.program_id(0), tiles_per_seq)
    first = pos == 0
    last = pos == tiles_per_seq - 1
    tm = x_ref.shape[0]
    xe = jnp.concatenate([prev_ref[...], x_ref[...], next_ref[...]], axis=0)
    ms = jnp.mean(xe * xe, axis=-1, keepdims=True)
    hn = (xe * lax.rsqrt(ms + EPS) * g_ref[...]).astype(bf16)
    hc = hn[HALO:HALO + tm]
    o0, o1, o2, o3 = LRU_WIDTH, 2 * LRU_WIDTH, 2 * LRU_WIDTH + SSD_WIDTH, 2 * LRU_WIDTH + SSD_WIDTH + SSD_CONV_CH
    pe = jnp.dot(hn, w_ref[:, :o0], preferred_element_type=f32)
    xc_ref[...] = _conv_centred(pe, first, last, lcw_ref[...], lcb_ref[...]).astype(bf16)
    gate_ref[...] = jnp.dot(hc, w_ref[:, o0:o1], preferred_element_type=f32).astype(bf16)
    z_ref[...] = jnp.dot(hc, w_ref[:, o1:o2], preferred_element_type=f32).astype(bf16)
    pe = jnp.dot(hn, w_ref[:, o2:o3], preferred_element_type=f32)
    xa_ref[...] = jax.nn.silu(_conv_centred(pe, first, last, scw_ref[...], scb_ref[...])).astype(bf16)
    dt_ref[...] = jnp.dot(hc, w_ref[:, o3:], preferred_element_type=f32)


def _in_proj(x2d, g, w_bf16, lru_cw, lru_cb, ssd_cw, ssd_cb, seq):
    t, d = x2d.shape
    n = w_bf16.shape[1]
    tm = _pick(seq, 512)
    per = tm // HALO
    last_blk = t // HALO - 1
    full = lambda shape: pl.BlockSpec(shape, lambda i: (0,) * len(shape))
    return pl.pallas_call(
        functools.partial(_in_proj_body, tiles_per_seq=seq // tm),
        grid=(t // tm,),
        in_specs=[
            pl.BlockSpec((HALO, d), lambda i: (jnp.maximum(i * per - 1, 0), 0)),
            pl.BlockSpec((tm, d), lambda i: (i, 0)),
            pl.BlockSpec((HALO, d), lambda i: (jnp.minimum((i + 1) * per, last_blk), 0)),
            full((1, d)),
            pl.BlockSpec((d, n), lambda i: (0, 0), pipeline_mode=pl.Buffered(1)),
            full((CONV_WIDTH, LRU_WIDTH)), full((1, LRU_WIDTH)),
            full((CONV_WIDTH, SSD_CONV_CH)), full((1, SSD_CONV_CH)),
        ],
        out_specs=[pl.BlockSpec((tm, w), lambda i: (i, 0)) for w in IN_SEGMENTS],
        out_shape=[jax.ShapeDtypeStruct((t, w), dt) for w, dt in zip(IN_SEGMENTS, (bf16, bf16, bf16, bf16, f32))],
        compiler_params=_params(("parallel",), VMEM_LIMIT),
        name="in_proj",
    )(x2d, x2d, x2d, g.reshape(1, d), w_bf16, lru_cw, lru_cb.reshape(1, -1), ssd_cw, ssd_cb.reshape(1, -1))


def _scan_rows8(a, u, reverse):
    r = lax.broadcasted_iota(i32, a.shape, 1)
    for k in (1, 2, 4):
        shift = (SUBLANES - k) if reverse else k
        valid = (r < SUBLANES - k) if reverse else (r >= k)
        a_s = pltpu.roll(a, shift, 1)
        u_s = pltpu.roll(u, shift, 1)
        u = jnp.where(valid, a * u_s + u, u)
        a = jnp.where(valid, a * a_s, a)
    return a, u


def _lru_gates(x_ref, w_ref, ba_ref, bi_ref, lam_ref, a_scr, u_scr, d):
    tc = x_ref.shape[1]
    groups = tc // SUBLANES
    xc = x_ref[0]
    sp = jax.nn.softplus(-lam_ref[d:d + 1, :])
    for h in range(LRU_HEADS):
        sl = slice(h * LRU_BLOCK, (h + 1) * LRU_BLOCK)
        pre = jnp.dot(xc[:, sl], w_ref[d, h], preferred_element_type=f32)
        xh = xc[:, sl].astype(f32)
        r = jax.nn.sigmoid(pre[:, :LRU_BLOCK] + ba_ref[d:d + 1, sl])
        gi = jax.nn.sigmoid(pre[:, LRU_BLOCK:] + bi_ref[d:d + 1, sl])
        log_a = (-LRU_C) * r * sp[:, sl]
        a = jnp.exp(log_a)
        u = jnp.sqrt(1.0 - a * a) * (gi * xh)
        a3, u3 = _scan_rows8(a.reshape(groups, SUBLANES, LRU_BLOCK),
                             u.reshape(groups, SUBLANES, LRU_BLOCK), bool(d))
        a_scr[d, :, sl] = a3.reshape(tc, LRU_BLOCK)
        u_scr[d, :, sl] = u3.reshape(tc, LRU_BLOCK)


def _lru_body(xf_ref, xb_ref, w_ref, ba_ref, bi_ref, lam_ref, of_ref, ob_ref, carry_ref, a_scr, u_scr):
    tc = xf_ref.shape[1]
    npair = tc // (2 * SUBLANES)

    @pl.when(pl.program_id(1) == 0)
    def _():
        carry_ref[...] = jnp.zeros_like(carry_ref)

    _lru_gates(xf_ref, w_ref, ba_ref, bi_ref, lam_ref, a_scr, u_scr, 0)
    _lru_gates(xb_ref, w_ref, ba_ref, bi_ref, lam_ref, a_scr, u_scr, 1)

    def advance(d, o_ref, row, carry):
        halves = [None, None]
        for k in ((1, 0) if d else (0, 1)):
            rows = pl.ds(pl.multiple_of(row + k * SUBLANES, SUBLANES), SUBLANES)
            hblk = u_scr[d, rows, :] + a_scr[d, rows, :] * carry
            halves[k] = hblk
            carry = hblk[0:1] if d else hblk[SUBLANES - 1:SUBLANES]
        o_ref[0, pl.ds(row, 2 * SUBLANES), :] = jnp.concatenate(halves, axis=0).astype(bf16)
        return carry

    def step(g, carries):
        cf = advance(0, of_ref, pl.multiple_of(g * 2 * SUBLANES, 2 * SUBLANES), carries[0])
        cb = advance(1, ob_ref, pl.multiple_of((npair - 1 - g) * 2 * SUBLANES, 2 * SUBLANES), carries[1])
        return cf, cb

    cf, cb = lax.fori_loop(0, npair, step, (carry_ref[0:1, :], carry_ref[1:2, :]), unroll=2)
    carry_ref[0:1, :] = cf
    carry_ref[1:2, :] = cb


def _lru_scan(xc, w_gate, ba, bi, lam):
    b, s, w = xc.shape
    tc = _pick(s, 256)
    nc = s // tc
    full = lambda shape: pl.BlockSpec(shape, lambda b_, c_: (0,) * len(shape))
    fwd = pl.BlockSpec((1, tc, w), lambda b_, c_: (b_, c_, 0))
    bwd = pl.BlockSpec((1, tc, w), lambda b_, c_: (b_, nc - 1 - c_, 0))
    return pl.pallas_call(
        _lru_body,
        grid=(b, nc),
        in_specs=[fwd, bwd, full((2, LRU_HEADS, LRU_BLOCK, 2 * LRU_BLOCK)), full((2, w)), full((2, w)), full((2, w))],
        out_specs=[fwd, bwd],
        out_shape=[jax.ShapeDtypeStruct((b, s, w), bf16)] * 2,
        scratch_shapes=[pltpu.VMEM((2, w), f32), pltpu.VMEM((2, tc, w), f32), pltpu.VMEM((2, tc, w), f32)],
        compiler_params=_params(("parallel", "arbitrary")),
        name="lru_scan",
    )(xc, xc, w_gate, ba, bi, lam)


def _expand_heads(arr, base):
    rows = arr.shape[0]
    lane = lax.broadcasted_iota(i32, (rows, LANES), 1)
    tiles = []
    for k in range(SSD_HEADS // 2):
        c0 = arr[:, base + 2 * k:base + 2 * k + 1]
        c1 = arr[:, base + 2 * k + 1:base + 2 * k + 2]
        tiles.append(jnp.where(lane < SSD_HEADDIM, c0, c1))
    return jnp.concatenate(tiles, axis=1)


def _cumsum_rows(x, reverse):
    n = x.shape[0]
    r = lax.broadcasted_iota(i32, x.shape, 0)
    k = 1
    while k < n:
        if reverse:
            x = x + jnp.where(r < n - k, pltpu.roll(x, n - k, 0), 0.0)
        else:
            x = x + jnp.where(r >= k, pltpu.roll(x, k, 0), 0.0)
        k *= 2
    return x


def _ssd_body(xf_ref, xb_ref, dtf_ref, dtb_in_ref, alog_ref, dtb_ref, dskip_ref, sele_ref, selc_ref,
              of_ref, ob_ref, state_ref):
    @pl.when(pl.program_id(1) == 0)
    def _():
        state_ref[...] = jnp.zeros_like(state_ref)

    _ssd_dir(xf_ref, dtf_ref, alog_ref, dtb_ref, dskip_ref, sele_ref, selc_ref, of_ref, state_ref.at[0], False)
    _ssd_dir(xb_ref, dtb_in_ref, alog_ref, dtb_ref, dskip_ref, sele_ref, selc_ref, ob_ref, state_ref.at[1], True)


def _split_bf16(x, parts):
    out = []
    for _ in range(parts):
        p = x.astype(bf16)
        out.append(p)
        x = x - p.astype(f32)
    return jnp.concatenate(out, axis=1)


def _ssd_selectors(chunk):
    lane = jnp.arange(LANES)[:, None]
    spread, cols = [], []
    for d in range(2):
        head = lane - d * SSD_HEADS
        s = (head == jnp.arange(SSD_WIDTH)[None, :] // SSD_HEADDIM).astype(bf16)
        c = (head == jnp.arange(SSD_HEADS * chunk)[None, :] // chunk).astype(bf16)
        spread.append(jnp.concatenate([s, s], axis=0))
        cols.append(jnp.concatenate([c, c], axis=0))
    return jnp.stack(spread), jnp.stack(cols)


def _ssd_dir(x_ref, dt_ref, alog_ref, dtb_ref, dskip_ref, sele_ref, selc_ref, o_ref, state_ref, reverse):
    L = x_ref.shape[1]
    base = SSD_HEADS if reverse else 0
    xact = x_ref[0]
    xs = xact[:, :SSD_WIDTH].astype(f32)

    dt = jax.nn.softplus(dt_ref[0] + dtb_ref[...])
    d_a = dt * (-jnp.exp(alog_ref[...]))
    cum = _cumsum_rows(d_a, reverse)
    edge = cum[0:1] if reverse else cum[L - 1:L]
    cum_t = cum.T

    d = 1 if reverse else 0
    stack = jnp.concatenate([_split_bf16(dt, 2), _split_bf16(jnp.exp(cum), 2), _split_bf16(jnp.exp(edge - cum), 2)],
                            axis=0)
    spread = jnp.dot(stack, sele_ref[d], preferred_element_type=f32)
    dtx = spread[:L] * xs
    e_cum = spread[L:2 * L]
    e_end = spread[2 * L:]
    cum_cols = jnp.dot(_split_bf16(cum, 2), selc_ref[d], preferred_element_type=f32)
    e_edge = _expand_heads(jnp.exp(edge), base)
    w_all = (e_end * dtx).astype(bf16)
    dtx_b = dtx.astype(bf16)

    li = lax.broadcasted_iota(i32, (L, L), 0)
    si = lax.broadcasted_iota(i32, (L, L), 1)
    tri = (si >= li) if reverse else (li >= si)
    lane = lax.broadcasted_iota(i32, (L, LANES), 1)
    lo_half = lane < SSD_HEADDIM

    outs = []
    for g in range(SSD_GROUPS):
        bsl = slice(SSD_WIDTH + g * SSD_STATE, SSD_WIDTH + (g + 1) * SSD_STATE)
        csl = slice(SSD_WIDTH + SSD_GN + g * SSD_STATE, SSD_WIDTH + SSD_GN + (g + 1) * SSD_STATE)
        gsl = slice(g * GROUP_WIDTH, (g + 1) * GROUP_WIDTH)
        bm = xact[:, bsl]
        cm = xact[:, csl]
        cb = lax.dot_general(cm, bm, (((1,), (1,)), ((), ())), preferred_element_type=f32)
        s_old = state_ref[g]
        y_off = jnp.dot(cm, s_old.astype(bf16), preferred_element_type=f32) * e_cum[:, gsl]
        tiles = []
        for p in range(2):
            acc = None
            for q in range(2):
                h = g * 4 + 2 * p + q
                j = base + h
                seg = cum_cols[:, h * L:(h + 1) * L] - cum_t[j:j + 1, :]
                decay = jnp.exp(jnp.where(tri, seg, -jnp.inf))
                m = (cb * decay).astype(bf16)
                tsl = slice(g * GROUP_WIDTH + p * LANES, g * GROUP_WIDTH + (p + 1) * LANES)
                rhs = jnp.where(lo_half if q == 0 else jnp.logical_not(lo_half), dtx_b[:, tsl], 0.0)
                part = jnp.dot(m, rhs.astype(bf16), preferred_element_type=f32)
                acc = part if acc is None else acc + part
            tiles.append(acc)
        outs.append(jnp.concatenate(tiles, axis=1) + y_off)
        upd = lax.dot_general(bm, w_all[:, gsl], (((0,), (0,)), ((), ())), preferred_element_type=f32)
        state_ref[g] = s_old * e_edge[:, gsl] + upd
    y = jnp.concatenate(outs, axis=1)
    if not reverse:
        y = y + _expand_heads(dskip_ref[...], 0) * xs
    o_ref[0] = y.astype(bf16)


def _ssd_scan(xact, dt_pad, alog_pad, dtb_pad, dskip_pad):
    b, s, w = xact.shape
    L = _pick(s, 128)
    nc = s // L
    full = lambda shape: pl.BlockSpec(shape, lambda b_, c_: (0,) * len(shape))
    fwd = lambda width: pl.BlockSpec((1, L, width), lambda b_, c_: (b_, c_, 0))
    bwd = lambda width: pl.BlockSpec((1, L, width), lambda b_, c_: (b_, nc - 1 - c_, 0))
    spread, cols = _ssd_selectors(L)
    return pl.pallas_call(
        _ssd_body,
        grid=(b, nc),
        in_specs=[fwd(w), bwd(w), fwd(LANES), bwd(LANES), full((1, LANES)), full((1, LANES)), full((1, LANES)),
                  full(spread.shape), full(cols.shape)],
        out_specs=[fwd(SSD_WIDTH), bwd(SSD_WIDTH)],
        out_shape=[jax.ShapeDtypeStruct((b, s, SSD_WIDTH), bf16)] * 2,
        scratch_shapes=[pltpu.VMEM((2, SSD_GROUPS, SSD_STATE, GROUP_WIDTH), f32)],
        compiler_params=_params(("parallel", "arbitrary")),
        name="ssd_scan",
    )(xact, xact, dt_pad, dt_pad, alog_pad, dtb_pad, dskip_pad, spread, cols)


def _rms(x, g):
    ms = jnp.mean(x * x, axis=-1, keepdims=True)
    return x * lax.rsqrt(ms + EPS) * g


def _mix_out_body(x_ref, hf_ref, hb_ref, gate_ref, yf_ref, yb_ref, z_ref, gn_ref, wo_ref,
                  gf_ref, wrh_ref, wrl_ref, xo_ref, hn_ref, lg_ref):
    up = lambda ref: ref[...].astype(f32)
    y_lru = (up(hf_ref) + up(hb_ref)) * jax.nn.gelu(up(gate_ref))
    y = (up(yf_ref) + up(yb_ref)) * jax.nn.silu(up(z_ref))
    parts = []
    for g in range(SSD_GROUPS):
        yg = y[:, g * GROUP_WIDTH:(g + 1) * GROUP_WIDTH]
        ms = jnp.mean(yg * yg, axis=-1, keepdims=True)
        parts.append(yg * lax.rsqrt(ms + EPS))
    y_ssd = jnp.concatenate(parts, axis=1) * gn_ref[...]
    mix = jnp.concatenate([y_lru, y_ssd], axis=1).astype(bf16)
    xn = x_ref[...] + jnp.dot(mix, wo_ref[...], preferred_element_type=f32)
    xo_ref[...] = xn
    hn = _rms(xn, gf_ref[...])
    tm = hn.shape[0]
    for k in range(hn.shape[1] // LANES):
        hn_ref[pl.ds(k, tm, stride=SUBLANES), :] = hn[:, k * LANES:(k + 1) * LANES]
    h_hi = hn.astype(bf16)
    h_lo = (hn - h_hi.astype(f32)).astype(bf16)
    lg = jnp.dot(h_hi, wrh_ref[...], preferred_element_type=f32)
    lg = lg + jnp.dot(h_lo, wrh_ref[...], preferred_element_type=f32)
    lg = lg + jnp.dot(h_hi, wrl_ref[...], preferred_element_type=f32)
    lg_ref[...] = lg


def _mix_out(x2d, hf, hb, gate, yf, yb, z, ssd_norm, w_out_bf16, norm_ffn, wr_hi, wr_lo):
    t, d = x2d.shape
    tm = _pick(t, 256)
    row = lambda w: pl.BlockSpec((tm, w), lambda i: (i, 0))
    full = lambda shape: pl.BlockSpec(shape, lambda i: (0,) * len(shape))
    return pl.pallas_call(
        _mix_out_body,
        grid=(t // tm,),
        in_specs=[row(d)] + [row(LRU_WIDTH)] * 3 + [row(SSD_WIDTH)] * 3 + [
            full((1, SSD_WIDTH)), full((LRU_WIDTH + SSD_WIDTH, d)), full((1, d)),
            full((d, LANES)), full((d, LANES)),
        ],
        out_specs=[row(d), pl.BlockSpec((tm * d // LANES, LANES), lambda i: (i, 0)), row(LANES)],
        out_shape=[jax.ShapeDtypeStruct((t, d), f32), jax.ShapeDtypeStruct((t * d // LANES, LANES), f32),
                   jax.ShapeDtypeStruct((t, LANES), f32)],
        compiler_params=_params(("parallel",), VMEM_LIMIT),
        name="mix_out",
    )(x2d, hf, hb, gate, yf, yb, z, ssd_norm.reshape(1, -1), w_out_bf16, norm_ffn.reshape(1, d),
      wr_hi, wr_lo)


def _tile_prefix(tiles, upper_incl, upper_strict, lane):
    incs = [jnp.dot(t.astype(bf16), upper_incl, preferred_element_type=f32) for t in tiles]
    tot = jnp.zeros((N_EXPERTS, LANES), f32)
    for j, inc in enumerate(incs):
        tot = jnp.where(lane == j, inc[:, LANES - 1:LANES], tot)
    start = jnp.dot(tot.astype(bf16), upper_strict, preferred_element_type=f32)
    return incs, tot, start


def _route_body(lg_ref, idx_ref, gate_ref, lpos_ref, tab_ref, lm_scr, vt_scr, list_scr, tabv_scr, tabs_scr, sem,
                *, cap):
    s = lg_ref.shape[1]
    nt = s // LANES
    lt = lg_ref[0].T[:N_EXPERTS]
    mx = jnp.max(lt, axis=0, keepdims=True)
    ex = jnp.exp(lt - mx)
    aff = ex / jnp.sum(ex, axis=0, keepdims=True)
    key = pltpu.bitcast(aff, i32)

    def search(i, thr):
        cand = thr | (jnp.int32(1) << (30 - i))
        cnt = jnp.sum((key >= cand).astype(f32), axis=1, keepdims=True)
        return jnp.where(cnt >= float(cap), cand, thr)

    thr = lax.fori_loop(0, 31, search, jnp.zeros((N_EXPERTS, 1), i32))
    gt = key > thr
    eq = key == thr
    need = float(cap) - jnp.sum(gt.astype(f32), axis=1, keepdims=True)

    sub_i = lax.broadcasted_iota(i32, (LANES, LANES), 0)
    lane_i = lax.broadcasted_iota(i32, (LANES, LANES), 1)
    upper_incl = (sub_i <= lane_i).astype(bf16)
    upper_strict = (sub_i < lane_i).astype(bf16)
    lane_e = lax.broadcasted_iota(i32, (N_EXPERTS, LANES), 1)
    tiles = lambda a: [a[:, j * LANES:(j + 1) * LANES] for j in range(nt)]

    eq_t = tiles(eq.astype(f32))
    incs, _, start = _tile_prefix(eq_t, upper_incl, upper_strict, lane_e)
    sel_t = []
    for j, (gtj, eqj) in enumerate(zip(tiles(gt), eq_t)):
        excl = incs[j] - eqj + start[:, j:j + 1]
        sel_t.append(jnp.logical_or(gtj, jnp.logical_and(eqj > 0.5, excl < need)).astype(f32))

    incs, tot, start = _tile_prefix(sel_t, upper_incl, upper_strict, lane_e)
    start8 = jnp.floor(start * (1.0 / SUBLANES)) * float(SUBLANES)
    tab_ref[0, 0] = start8.astype(i32)
    tab_ref[0, 1] = tot.astype(i32)
    tabv_scr[...] = start.astype(i32)
    to_smem = pltpu.make_async_copy(tabv_scr, tabs_scr, sem)
    to_smem.start()

    kind = lax.broadcasted_iota(i32, (SUBLANES, LANES), 0)
    tok_lane = lax.broadcasted_iota(i32, (SUBLANES, LANES), 1).astype(f32)
    fill = jnp.full((LANES - N_EXPERTS, LANES), -1.0, f32)
    zrows = jnp.zeros((LANES - 3 * N_EXPERTS - SUBLANES, LANES), f32)
    for j, a in enumerate(tiles(aff)):
        lm = jnp.where(sel_t[j] > 0.5, incs[j] - sel_t[j], -1.0)
        lm_scr[j] = lm
        shifted = jnp.where(sel_t[j] > 0.5, lm + (start[:, j:j + 1] - start8[:, j:j + 1]), -1.0)
        lpos_ref[0, pl.ds(j * LANES, LANES), :] = jnp.concatenate([shifted, fill], axis=0).T
        a_hi = a.astype(bf16).astype(f32)
        a_mid = (a - a_hi).astype(bf16).astype(f32)
        a_lo = (a - a_hi - a_mid).astype(bf16).astype(f32)
        tok = jnp.where(kind == 0, float(j), jnp.where(kind == 1, tok_lane, 0.0))
        vt_scr[j] = jnp.concatenate([a_hi, a_mid, a_lo, tok, zrows], axis=0).T.astype(bf16)

    to_smem.wait()
    rank = sub_i.astype(f32)

    def compact(j, carry):
        lm = lm_scr[j]
        vt = vt_scr[j]
        for e in range(N_EXPERTS):
            onehot = (lm[e:e + 1] == rank).astype(bf16)
            packed = jnp.dot(onehot, vt, preferred_element_type=f32)
            list_scr[e, pl.ds(tabs_scr[e, j], LANES), :] = packed
        return carry

    lax.fori_loop(0, nt, compact, 0)

    lane_c = lax.broadcasted_iota(i32, (cap, LANES), 1)
    idx_c = jnp.zeros((cap, LANES), f32)
    gate_c = jnp.zeros((cap, LANES), f32)
    for e in range(N_EXPERTS):
        rows = list_scr[e, 0:cap, :]
        g = rows[:, e:e + 1] + rows[:, N_EXPERTS + e:N_EXPERTS + e + 1] + rows[:, 2 * N_EXPERTS + e:2 * N_EXPERTS + e + 1]
        t = rows[:, 3 * N_EXPERTS:3 * N_EXPERTS + 1] * float(LANES) + rows[:, 3 * N_EXPERTS + 1:3 * N_EXPERTS + 2]
        idx_c = jnp.where(lane_c == e, t, idx_c)
        gate_c = jnp.where(lane_c == e, g, gate_c)
    idx_ref[0] = idx_c.T[:N_EXPERTS].astype(i32)
    gate_ref[0] = gate_c.T[:N_EXPERTS]


def _route(logits, cap):
    b, s, _ = logits.shape
    nt = s // LANES
    assert nt <= LANES
    return pl.pallas_call(
        functools.partial(_route_body, cap=cap),
        grid=(b,),
        in_specs=[pl.BlockSpec((1, s, LANES), lambda i: (i, 0, 0))],
        out_specs=[pl.BlockSpec((1, N_EXPERTS, cap), lambda i: (i, 0, 0)),
                   pl.BlockSpec((1, N_EXPERTS, cap), lambda i: (i, 0, 0)),
                   pl.BlockSpec((1, s, LANES), lambda i: (i, 0, 0)),
                   pl.BlockSpec((1, 2, N_EXPERTS, LANES), lambda i: (i, 0, 0, 0))],
        out_shape=[jax.ShapeDtypeStruct((b, N_EXPERTS, cap), i32),
                   jax.ShapeDtypeStruct((b, N_EXPERTS, cap), f32),
                   jax.ShapeDtypeStruct((b, s, LANES), f32),
                   jax.ShapeDtypeStruct((b, 2, N_EXPERTS, LANES), i32)],
        scratch_shapes=[pltpu.VMEM((nt, N_EXPERTS, LANES), f32),
                        pltpu.VMEM((nt, LANES, LANES), bf16),
                        pltpu.VMEM((N_EXPERTS, cap + LANES, LANES), f32),
                        pltpu.VMEM((N_EXPERTS, LANES), i32),
                        pltpu.SMEM((N_EXPERTS, LANES), i32),
                        pltpu.SemaphoreType.DMA],
        compiler_params=_params(("parallel",), VMEM_LIMIT),
        name="route",
    )(logits)


def _token_copy(src_hbm, tok, dst, r, sem):
    src = src_hbm.at[pl.ds(pl.multiple_of(tok * SUBLANES, SUBLANES), SUBLANES), :]
    return pltpu.make_async_copy(src, dst.at[pl.ds(pl.multiple_of(r * SUBLANES, SUBLANES), SUBLANES), :], sem)


def _moe_body(idx_ref, idxn_ref, gate_ref, wg_ref, wu_ref, wd_ref, hn_hbm, o_ref, xg, sem, *, cap, seq, nb):
    n = pl.program_id(0) * nb + pl.program_id(1)
    total = N_EXPERTS * nb
    slot = lax.rem(n, 2)
    other = 1 - slot

    @pl.when(n == 0)
    def _():
        def start(r, carry):
            _token_copy(hn_hbm, idx_ref[0, 0, 0, r], xg.at[slot], r, sem.at[slot]).start()
            return carry

        lax.fori_loop(0, cap, start, 0, unroll=8)

    def wait_all(buf):
        def wait(r, carry):
            _token_copy(hn_hbm, 0, xg.at[buf], r, sem.at[buf]).wait()
            return carry

        lax.fori_loop(0, cap, wait, 0, unroll=8)

    wait_all(slot)

    base_next = lax.rem(jnp.minimum(n + 1, total - 1), nb) * seq
    tm = min(cap, 256)
    ntile = wg_ref.shape[1] // LANES
    for m in range(cap // tm):
        rows = slice(m * tm, (m + 1) * tm)
        xm = jnp.concatenate(
            [xg[slot, pl.ds(m * tm * SUBLANES + k, tm, stride=SUBLANES), :] for k in range(ntile)],
            axis=1).astype(bf16)
        hg = jnp.dot(xm, wg_ref[0], preferred_element_type=f32)
        hu = jnp.dot(xm, wu_ref[0], preferred_element_type=f32)
        hid = (jax.nn.silu(hg) * hu).astype(bf16)
        y = jnp.dot(hid, wd_ref[0], preferred_element_type=f32) * gate_ref[0, 0, rows, :]
        o_ref[0, 0, rows, :] = _pack_halves(y)
        for r in range(m * tm, (m + 1) * tm):
            _token_copy(hn_hbm, base_next + idxn_ref[0, 0, 0, r], xg.at[other], r, sem.at[other]).start()
    o_ref[0, 0, cap:cap + WINDOW, :] = jnp.zeros((WINDOW, o_ref.shape[-1]), jnp.uint32)

    @pl.when(n == total - 1)
    def _():
        wait_all(other)


def _pack_halves(y):
    w = y.shape[1] // 2
    hi = pltpu.bitcast(y[:, :w].astype(bf16).astype(f32), jnp.uint32)
    lo = pltpu.bitcast(y[:, w:].astype(bf16).astype(f32), jnp.uint32)
    return hi | (lo >> 16)


def _unpack_halves(p):
    hi = pltpu.bitcast(p & jnp.uint32(0xFFFF0000), f32)
    lo = pltpu.bitcast(p << 16, f32)
    return jnp.concatenate([hi, lo], axis=1).astype(bf16)


def _moe_ffn(hn_tiles, idx, gates, wg, wu, wd):
    b, _, cap = idx.shape
    d = wg.shape[1]
    s = hn_tiles.shape[0] * LANES // d // b
    idx4 = idx.reshape(b, N_EXPERTS, 1, cap)
    gates4 = gates.reshape(b, N_EXPERTS, cap, 1)
    def next_step(e, i):
        n1 = jnp.minimum(e * b + i + 1, N_EXPERTS * b - 1)
        return (lax.rem(n1, b), n1 // b, 0, 0)

    return pl.pallas_call(
        functools.partial(_moe_body, cap=cap, seq=s, nb=b),
        grid=(N_EXPERTS, b),
        in_specs=[
            pl.BlockSpec((1, 1, 1, cap), lambda e, i: (i, e, 0, 0), memory_space=pltpu.SMEM),
            pl.BlockSpec((1, 1, 1, cap), next_step, memory_space=pltpu.SMEM),
            pl.BlockSpec((1, 1, cap, 1), lambda e, i: (i, e, 0, 0)),
            pl.BlockSpec((1, d, D_FF), lambda e, i: (e, 0, 0)),
            pl.BlockSpec((1, d, D_FF), lambda e, i: (e, 0, 0)),
            pl.BlockSpec((1, D_FF, d), lambda e, i: (e, 0, 0)),
            pl.BlockSpec(memory_space=pl.ANY),
        ],
        out_specs=pl.BlockSpec((1, 1, cap + WINDOW, d // 2), lambda e, i: (i, e, 0, 0)),
        out_shape=jax.ShapeDtypeStruct((b, N_EXPERTS, cap + WINDOW, d // 2), jnp.uint32),
        scratch_shapes=[pltpu.VMEM((2, cap * d // LANES, LANES), f32), pltpu.SemaphoreType.DMA((2,))],
        compiler_params=_params(("arbitrary", "arbitrary"), VMEM_LIMIT),
        name="moe_ffn",
    )(idx4, idx4, gates4, wg, wu, wd, hn_tiles)


def _window_copy(yg_hbm, b, e, start, win, slot, sem):
    src = yg_hbm.at[b, e, pl.ds(pl.multiple_of(start, SUBLANES), WINDOW), :]
    return pltpu.make_async_copy(src, win.at[slot, e], sem.at[slot])


def _combine_body(tab_ref, tabn_ref, lpos_ref, x_ref, yg_hbm, o_ref, win, sem, *, nb, nt):
    b = pl.program_id(0)
    j = pl.program_id(1)
    n = b * nt + j
    slot = lax.rem(n, 2)

    @pl.when(n == 0)
    def _():
        for e in range(N_EXPERTS):
            _window_copy(yg_hbm, b, e, tab_ref[0, 0, e, j], win, slot, sem).start()

    @pl.when(n + 1 < nb * nt)
    def _():
        wrap = j + 1 == nt
        jn = jnp.where(wrap, 0, j + 1)
        bn = jnp.where(wrap, b + 1, b)
        for e in range(N_EXPERTS):
            _window_copy(yg_hbm, bn, e, tabn_ref[0, 0, e, jn], win, 1 - slot, sem).start()

    for e in range(N_EXPERTS):
        _window_copy(yg_hbm, b, e, 0, win, slot, sem).wait()

    lm = lpos_ref[0]
    row = lax.broadcasted_iota(i32, (LANES, WINDOW), 1).astype(f32)
    acc = x_ref[...]
    for e in range(N_EXPERTS):
        place = (lm[:, e:e + 1] == row).astype(bf16)
        acc = acc + jnp.dot(place, _unpack_halves(win[slot, e]), preferred_element_type=f32)
    o_ref[...] = acc


def _combine(x2d, lpos, tab, yg, nb):
    t, d = x2d.shape
    nt = t // nb // LANES

    def next_batch(i, j):
        return (jnp.minimum(i + (j + 1) // nt, nb - 1), 0, 0, 0)

    return pl.pallas_call(
        functools.partial(_combine_body, nb=nb, nt=nt),
        grid=(nb, nt),
        in_specs=[
            pl.BlockSpec((1, 2, N_EXPERTS, LANES), lambda i, j: (i, 0, 0, 0), memory_space=pltpu.SMEM),
            pl.BlockSpec((1, 2, N_EXPERTS, LANES), next_batch, memory_space=pltpu.SMEM),
            pl.BlockSpec((1, LANES, LANES), lambda i, j: (i, j, 0)),
            pl.BlockSpec((LANES, d), lambda i, j: (i * nt + j, 0)),
            pl.BlockSpec(memory_space=pl.ANY),
        ],
        out_specs=pl.BlockSpec((LANES, d), lambda i, j: (i * nt + j, 0)),
        out_shape=jax.ShapeDtypeStruct((t, d), f32),
        scratch_shapes=[pltpu.VMEM((2, N_EXPERTS, WINDOW, d // 2), jnp.uint32), pltpu.SemaphoreType.DMA((2,))],
        compiler_params=_params(("arbitrary", "arbitrary")),
        name="moe_combine",
    )(tab, tab, lpos, x2d, yg)


def _final_norm_body(x_ref, g_ref, o_ref):
    o_ref[...] = _rms(x_ref[...], g_ref[...])


def _final_norm(x2d, g):
    t, d = x2d.shape
    tm = _pick(t, 1024)
    return pl.pallas_call(
        _final_norm_body,
        grid=(t // tm,),
        in_specs=[pl.BlockSpec((tm, d), lambda i: (i, 0)), pl.BlockSpec((1, d), lambda i: (0, 0))],
        out_specs=pl.BlockSpec((tm, d), lambda i: (i, 0)),
        out_shape=jax.ShapeDtypeStruct((t, d), f32),
        compiler_params=_params(("parallel",)),
        name="final_norm",
    )(x2d, g.reshape(1, d))


def _pad_lanes(v):
    return jnp.zeros((1, LANES), f32).at[0, :v.shape[0]].set(v.astype(f32))


def _mixer(x3, norm_mix, w_in, lru_conv_w, lru_conv_b, lru_wa, lru_ba, lru_wi, lru_bi, lru_lambda,
           ssd_conv_w, ssd_conv_b, ssd_a_log, ssd_dt_bias, ssd_d):
    b, s, d = x3.shape
    x2d = x3.reshape(b * s, d)
    pad = jnp.zeros((d, LANES - 2 * SSD_HEADS), bf16)
    w_bf = jnp.concatenate([w_in.astype(bf16), pad], axis=1)
    xc, gate, z, xact, dt = _in_proj(x2d, norm_mix, w_bf, lru_conv_w, lru_conv_b, ssd_conv_w, ssd_conv_b, s)
    xc = xc.reshape(b, s, LRU_WIDTH)
    xact = xact.reshape(b, s, SSD_CONV_CH)
    dt = dt.reshape(b, s, LANES)

    w_gate = jnp.concatenate([lru_wa, lru_wi], axis=-1).astype(bf16)
    hs = _lru_scan(xc, w_gate, lru_ba, lru_bi, lru_lambda)
    alog_pad = _pad_lanes(ssd_a_log.reshape(-1))
    dtb_pad = _pad_lanes(ssd_dt_bias.reshape(-1))
    dskip_pad = _pad_lanes(ssd_d)
    ys = _ssd_scan(xact, dt, alog_pad, dtb_pad, dskip_pad)
    t = b * s
    return (hs[0].reshape(t, -1), hs[1].reshape(t, -1), gate, ys[0].reshape(t, -1), ys[1].reshape(t, -1), z)


def _layer(x3, norm_mix, w_in, lru_conv_w, lru_conv_b, lru_wa, lru_ba, lru_wi, lru_bi, lru_lambda,
           ssd_conv_w, ssd_conv_b, ssd_a_log, ssd_dt_bias, ssd_d, ssd_norm, w_out, norm_ffn, w_router,
           w_gate, w_up, w_down):
    b, s, d = x3.shape
    cap = max(1, CAPACITY_FACTOR * s // N_EXPERTS)
    assert s % LANES == 0 and cap % LANES == 0, "sequence length must give 128-aligned expert capacity"
    hf, hb, gate, yf, yb, z = _mixer(x3, norm_mix, w_in, lru_conv_w, lru_conv_b, lru_wa, lru_ba, lru_wi,
                                     lru_bi, lru_lambda, ssd_conv_w, ssd_conv_b, ssd_a_log, ssd_dt_bias, ssd_d)
    wr = jnp.zeros((d, LANES), f32).at[:, :N_EXPERTS].set(w_router)
    wr_hi = wr.astype(bf16)
    wr_lo = (wr - wr_hi.astype(f32)).astype(bf16)
    xn, hn, logits = _mix_out(x3.reshape(b * s, d), hf, hb, gate, yf, yb, z, ssd_norm, w_out.astype(bf16),
                              norm_ffn, wr_hi, wr_lo)
    idx, gates, lpos, tab = _route(logits.reshape(b, s, LANES), cap)
    yg = _moe_ffn(hn, idx, gates, w_gate.astype(bf16), w_up.astype(bf16), w_down.astype(bf16))
    return _combine(xn, lpos, tab, yg, b).reshape(b, s, d)


def kernel(x, norm_mix, w_in, lru_conv_w, lru_conv_b, lru_wa, lru_ba, lru_wi, lru_bi, lru_lambda, ssd_conv_w, ssd_conv_b, ssd_a_log, ssd_dt_bias, ssd_d, ssd_norm, w_out, norm_ffn, w_router, w_gate, w_up, w_down, norm_final):
    depth = norm_mix.shape[0]
    for l in range(depth):
        x = _layer(x, norm_mix[l], w_in[l], lru_conv_w[l], lru_conv_b[l], lru_wa[l], lru_ba[l], lru_wi[l],
                   lru_bi[l], lru_lambda[l], ssd_conv_w[l], ssd_conv_b[l], ssd_a_log[l], ssd_dt_bias[l],
                   ssd_d[l], ssd_norm[l], w_out[l], norm_ffn[l], w_router[l], w_gate[l], w_up[l], w_down[l])
    b, s, d = x.shape
    return _final_norm(x.reshape(b * s, d), norm_final).reshape(b, s, d)
```

```python
import functools

import jax
import jax.numpy as jnp
from jax import lax
from jax.experimental import pallas as pl
from jax.experimental.pallas import tpu as pltpu

f32 = jnp.float32
bf16 = jnp.bfloat16
i32 = jnp.int32

D_MODEL = 1024
EPS = 1e-6
CONV_WIDTH = 4
LRU_WIDTH = 1024
LRU_HEADS = 8
LRU_BLOCK = 128
LRU_C = 8.0
SSD_WIDTH = 1024
SSD_HEADDIM = 64
SSD_HEADS = 16
SSD_GROUPS = 4
SSD_STATE = 128
SSD_GN = SSD_GROUPS * SSD_STATE
SSD_CONV_CH = SSD_WIDTH + 2 * SSD_GN
GROUP_WIDTH = SSD_WIDTH // SSD_GROUPS
N_EXPERTS = 16
CAPACITY_FACTOR = 2
D_FF = 2048

LANES = 128
SUBLANES = 8
HALO = SUBLANES
BF16_ROWS = 2 * SUBLANES
WINDOW = LANES + BF16_ROWS
VMEM_LIMIT = 56 * 1024 * 1024


def _pick(n, target):
    if n <= target:
        return n
    t = target
    while t >= LANES:
        if n % t == 0:
            return t
        t -= LANES
    return n


def _params(sem, vmem=None):
    return pltpu.CompilerParams(dimension_semantics=sem, vmem_limit_bytes=vmem)


IN_SEGMENTS = (LRU_WIDTH, LRU_WIDTH, SSD_WIDTH, SSD_CONV_CH, LANES)


def _conv_centred(pe, first, last, cw, cb):
    tm = pe.shape[0] - 2 * HALO
    before = jnp.where(first, 0.0, pe[:HALO])
    after = jnp.where(last, 0.0, pe[HALO + tm:])
    c = pe[HALO:HALO + tm]
    r8 = lax.broadcasted_iota(i32, (SUBLANES, pe.shape[1]), 0)
    back1 = pltpu.roll(c, 1, 0)
    fwd1 = pltpu.roll(c, tm - 1, 0)
    fwd2 = pltpu.roll(c, tm - 2, 0)
    back1 = jnp.concatenate([jnp.where(r8 == 0, before[HALO - 1:HALO], back1[:HALO]), back1[HALO:]], axis=0)
    fwd1 = jnp.concatenate([fwd1[:tm - HALO], jnp.where(r8 == HALO - 1, after[0:1], fwd1[tm - HALO:])], axis=0)
    tail2 = jnp.where(r8 == HALO - 2, after[0:1], jnp.where(r8 == HALO - 1, after[1:2], fwd2[tm - HALO:]))
    fwd2 = jnp.concatenate([fwd2[:tm - HALO], tail2], axis=0)
    return cb + back1 * cw[0:1] + c * cw[1:2] + fwd1 * cw[2:3] + fwd2 * cw[3:4]


def _in_proj_body(prev_ref, x_ref, next_ref, g_ref, w_ref, lcw_ref, lcb_ref, scw_ref, scb_ref,
                  xc_ref, gate_ref, z_ref, xa_ref, dt_ref, *, tiles_per_seq):
    pos = lax.rem(pl.program_id(0), tiles_per_seq)
    first = pos == 0
    last = pos == tiles_per_seq - 1
    tm = x_ref.shape[0]
    xe = jnp.concatenate([prev_ref[...], x_ref[...], next_ref[...]], axis=0)
    ms = jnp.mean(xe * xe, axis=-1, keepdims=True)
    hn = (xe * lax.rsqrt(ms + EPS) * g_ref[...]).astype(bf16)
    hc = hn[HALO:HALO + tm]
    o0, o1, o2, o3 = LRU_WIDTH, 2 * LRU_WIDTH, 2 * LRU_WIDTH + SSD_WIDTH, 2 * LRU_WIDTH + SSD_WIDTH + SSD_CONV_CH
    pe = jnp.dot(hn, w_ref[:, :o0], preferred_element_type=f32)
    xc_ref[...] = _conv_centred(pe, first, last, lcw_ref[...], lcb_ref[...]).astype(bf16)
    gate_ref[...] = jnp.dot(hc, w_ref[:, o0:o1], preferred_element_type=f32).astype(bf16)
    z_ref[...] = jnp.dot(hc, w_ref[:, o1:o2], preferred_element_type=f32).astype(bf16)
    pe = jnp.dot(hn, w_ref[:, o2:o3], preferred_element_type=f32)
    xa_ref[...] = jax.nn.silu(_conv_centred(pe, first, last, scw_ref[...], scb_ref[...])).astype(bf16)
    dt_ref[...] = jnp.dot(hc, w_ref[:, o3:], preferred_element_type=f32)


def _in_proj(x2d, g, w_bf16, lru_cw, lru_cb, ssd_cw, ssd_cb, seq):
    t, d = x2d.shape
    n = w_bf16.shape[1]
    tm = _pick(seq, 512)
    per = tm // HALO
    last_blk = t // HALO - 1
    full = lambda shape: pl.BlockSpec(shape, lambda i: (0,) * len(shape))
    return pl.pallas_call(
        functools.partial(_in_proj_body, tiles_per_seq=seq // tm),
        grid=(t // tm,),
        in_specs=[
            pl.BlockSpec((HALO, d), lambda i: (jnp.maximum(i * per - 1, 0), 0)),
            pl.BlockSpec((tm, d), lambda i: (i, 0)),
            pl.BlockSpec((HALO, d), lambda i: (jnp.minimum((i + 1) * per, last_blk), 0)),
            full((1, d)),
            pl.BlockSpec((d, n), lambda i: (0, 0), pipeline_mode=pl.Buffered(1)),
            full((CONV_WIDTH, LRU_WIDTH)), full((1, LRU_WIDTH)),
            full((CONV_WIDTH, SSD_CONV_CH)), full((1, SSD_CONV_CH)),
        ],
        out_specs=[pl.BlockSpec((tm, w), lambda i: (i, 0)) for w in IN_SEGMENTS],
        out_shape=[jax.ShapeDtypeStruct((t, w), dt) for w, dt in zip(IN_SEGMENTS, (bf16, bf16, bf16, bf16, f32))],
        compiler_params=_params(("parallel",), VMEM_LIMIT),
        name="in_proj",
    )(x2d, x2d, x2d, g.reshape(1, d), w_bf16, lru_cw, lru_cb.reshape(1, -1), ssd_cw, ssd_cb.reshape(1, -1))


def _scan_rows8(a, u, reverse):
    r = lax.broadcasted_iota(i32, a.shape, 1)
    for k in (1, 2, 4):
        shift = (SUBLANES - k) if reverse else k
        valid = (r < SUBLANES - k) if reverse else (r >= k)
        a_s = pltpu.roll(a, shift, 1)
        u_s = pltpu.roll(u, shift, 1)
        u = jnp.where(valid, a * u_s + u, u)
        a = jnp.where(valid, a * a_s, a)
    return a, u


def _lru_gates(x_ref, w_ref, ba_ref, bi_ref, lam_ref, a_scr, u_scr, d):
    tc = x_ref.shape[1]
    groups = tc // SUBLANES
    xc = x_ref[0]
    sp = jax.nn.softplus(-lam_ref[d:d + 1, :])
    for h in range(LRU_HEADS):
        sl = slice(h * LRU_BLOCK, (h + 1) * LRU_BLOCK)
        pre = jnp.dot(xc[:, sl], w_ref[d, h], preferred_element_type=f32)
        xh = xc[:, sl].astype(f32)
        r = jax.nn.sigmoid(pre[:, :LRU_BLOCK] + ba_ref[d:d + 1, sl])
        gi = jax.nn.sigmoid(pre[:, LRU_BLOCK:] + bi_ref[d:d + 1, sl])
        log_a = (-LRU_C) * r * sp[:, sl]
        a = jnp.exp(log_a)
        u = jnp.sqrt(1.0 - a * a) * (gi * xh)
        a3, u3 = _scan_rows8(a.reshape(groups, SUBLANES, LRU_BLOCK),
                             u.reshape(groups, SUBLANES, LRU_BLOCK), bool(d))
        a_scr[d, :, sl] = a3.reshape(tc, LRU_BLOCK)
        u_scr[d, :, sl] = u3.reshape(tc, LRU_BLOCK)


def _lru_body(xf_ref, xb_ref, w_ref, ba_ref, bi_ref, lam_ref, of_ref, ob_ref, carry_ref, a_scr, u_scr):
    tc = xf_ref.shape[1]
    npair = tc // (2 * SUBLANES)

    @pl.when(pl.program_id(1) == 0)
    def _():
        carry_ref[...] = jnp.zeros_like(carry_ref)

    _lru_gates(xf_ref, w_ref, ba_ref, bi_ref, lam_ref, a_scr, u_scr, 0)
    _lru_gates(xb_ref, w_ref, ba_ref, bi_ref, lam_ref, a_scr, u_scr, 1)

    def advance(d, o_ref, row, carry):
        halves = [None, None]
        for k in ((1, 0) if d else (0, 1)):
            rows = pl.ds(pl.multiple_of(row + k * SUBLANES, SUBLANES), SUBLANES)
            hblk = u_scr[d, rows, :] + a_scr[d, rows, :] * carry
            halves[k] = hblk
            carry = hblk[0:1] if d else hblk[SUBLANES - 1:SUBLANES]
        o_ref[0, pl.ds(row, 2 * SUBLANES), :] = jnp.concatenate(halves, axis=0).astype(bf16)
        return carry

    def step(g, carries):
        cf = advance(0, of_ref, pl.multiple_of(g * 2 * SUBLANES, 2 * SUBLANES), carries[0])
        cb = advance(1, ob_ref, pl.multiple_of((npair - 1 - g) * 2 * SUBLANES, 2 * SUBLANES), carries[1])
        return cf, cb

    cf, cb = lax.fori_loop(0, npair, step, (carry_ref[0:1, :], carry_ref[1:2, :]), unroll=2)
    carry_ref[0:1, :] = cf
    carry_ref[1:2, :] = cb


def _lru_scan(xc, w_gate, ba, bi, lam):
    b, s, w = xc.shape
    tc = _pick(s, 256)
    nc = s // tc
    full = lambda shape: pl.BlockSpec(shape, lambda b_, c_: (0,) * len(shape))
    fwd = pl.BlockSpec((1, tc, w), lambda b_, c_: (b_, c_, 0))
    bwd = pl.BlockSpec((1, tc, w), lambda b_, c_: (b_, nc - 1 - c_, 0))
    return pl.pallas_call(
        _lru_body,
        grid=(b, nc),
        in_specs=[fwd, bwd, full((2, LRU_HEADS, LRU_BLOCK, 2 * LRU_BLOCK)), full((2, w)), full((2, w)), full((2, w))],
        out_specs=[fwd, bwd],
        out_shape=[jax.ShapeDtypeStruct((b, s, w), bf16)] * 2,
        scratch_shapes=[pltpu.VMEM((2, w), f32), pltpu.VMEM((2, tc, w), f32), pltpu.VMEM((2, tc, w), f32)],
        compiler_params=_params(("parallel", "arbitrary")),
        name="lru_scan",
    )(xc, xc, w_gate, ba, bi, lam)


def _expand_heads(arr, base):
    rows = arr.shape[0]
    lane = lax.broadcasted_iota(i32, (rows, LANES), 1)
    tiles = []
    for k in range(SSD_HEADS // 2):
        c0 = arr[:, base + 2 * k:base + 2 * k + 1]
        c1 = arr[:, base + 2 * k + 1:base + 2 * k + 2]
        tiles.append(jnp.where(lane < SSD_HEADDIM, c0, c1))
    return jnp.concatenate(tiles, axis=1)


def _cumsum_rows(x, reverse):
    n = x.shape[0]
    r = lax.broadcasted_iota(i32, x.shape, 0)
    k = 1
    while k < n:
        if reverse:
            x = x + jnp.where(r < n - k, pltpu.roll(x, n - k, 0), 0.0)
        else:
            x = x + jnp.where(r >= k, pltpu.roll(x, k, 0), 0.0)
        k *= 2
    return x


def _ssd_body(xf_ref, xb_ref, dtf_ref, dtb_in_ref, alog_ref, dtb_ref, dskip_ref, sele_ref, selc_ref,
              of_ref, ob_ref, state_ref):
    @pl.when(pl.program_id(1) == 0)
    def _():
        state_ref[...] = jnp.zeros_like(state_ref)

    _ssd_dir(xf_ref, dtf_ref, alog_ref, dtb_ref, dskip_ref, sele_ref, selc_ref, of_ref, state_ref.at[0], False)
    _ssd_dir(xb_ref, dtb_in_ref, alog_ref, dtb_ref, dskip_ref, sele_ref, selc_ref, ob_ref, state_ref.at[1], True)


def _split_bf16(x, parts):
    out = []
    for _ in range(parts):
        p = x.astype(bf16)
        out.append(p)
        x = x - p.astype(f32)
    return jnp.concatenate(out, axis=1)


def _ssd_selectors(chunk):
    lane = jnp.arange(LANES)[:, None]
    spread, cols = [], []
    for d in range(2):
        head = lane - d * SSD_HEADS
        s = (head == jnp.arange(SSD_WIDTH)[None, :] // SSD_HEADDIM).astype(bf16)
        c = (head == jnp.arange(SSD_HEADS * chunk)[None, :] // chunk).astype(bf16)
        spread.append(jnp.concatenate([s, s], axis=0))
        cols.append(jnp.concatenate([c, c], axis=0))
    return jnp.stack(spread), jnp.stack(cols)


def _ssd_dir(x_ref, dt_ref, alog_ref, dtb_ref, dskip_ref, sele_ref, selc_ref, o_ref, state_ref, reverse):
    L = x_ref.shape[1]
    base = SSD_HEADS if reverse else 0
    xact = x_ref[0]
    xs = xact[:, :SSD_WIDTH].astype(f32)

    dt = jax.nn.softplus(dt_ref[0] + dtb_ref[...])
    d_a = dt * (-jnp.exp(alog_ref[...]))
    cum = _cumsum_rows(d_a, reverse)
    edge = cum[0:1] if reverse else cum[L - 1:L]
    cum_t = cum.T

    d = 1 if reverse else 0
    stack = jnp.concatenate([_split_bf16(dt, 2), _split_bf16(jnp.exp(cum), 2), _split_bf16(jnp.exp(edge - cum), 2)],
                            axis=0)
    spread = jnp.dot(stack, sele_ref[d], preferred_element_type=f32)
    dtx = spread[:L] * xs
    e_cum = spread[L:2 * L]
    e_end = spread[2 * L:]
    cum_cols = jnp.dot(_split_bf16(cum, 2), selc_ref[d], preferred_element_type=f32)
    e_edge = _expand_heads(jnp.exp(edge), base)
    w_all = (e_end * dtx).astype(bf16)
    dtx_b = dtx.astype(bf16)

    li = lax.broadcasted_iota(i32, (L, L), 0)
    si = lax.broadcasted_iota(i32, (L, L), 1)
    tri = (si >= li) if reverse else (li >= si)
    lane = lax.broadcasted_iota(i32, (L, LANES), 1)
    lo_half = lane < SSD_HEADDIM

    outs = []
    for g in range(SSD_GROUPS):
        bsl = slice(SSD_WIDTH + g * SSD_STATE, SSD_WIDTH + (g + 1) * SSD_STATE)
        csl = slice(SSD_WIDTH + SSD_GN + g * SSD_STATE, SSD_WIDTH + SSD_GN + (g + 1) * SSD_STATE)
        gsl = slice(g * GROUP_WIDTH, (g + 1) * GROUP_WIDTH)
        bm = xact[:, bsl]
        cm = xact[:, csl]
        cb = lax.dot_general(cm, bm, (((1,), (1,)), ((), ())), preferred_element_type=f32)
        s_old = state_ref[g]
        y_off = jnp.dot(cm, s_old.astype(bf16), preferred_element_type=f32) * e_cum[:, gsl]
        tiles = []
        for p in range(2):
            acc = None
            for q in range(2):
                h = g * 4 + 2 * p + q
                j = base + h
                seg = cum_cols[:, h * L:(h + 1) * L] - cum_t[j:j + 1, :]
                decay = jnp.exp(jnp.where(tri, seg, -jnp.inf))
                m = (cb * decay).astype(bf16)
                tsl = slice(g * GROUP_WIDTH + p * LANES, g * GROUP_WIDTH + (p + 1) * LANES)
                rhs = jnp.where(lo_half if q == 0 else jnp.logical_not(lo_half), dtx_b[:, tsl], 0.0)
                part = jnp.dot(m, rhs.astype(bf16), preferred_element_type=f32)
                acc = part if acc is None else acc + part
            tiles.append(acc)
        outs.append(jnp.concatenate(tiles, axis=1) + y_off)
        upd = lax.dot_general(bm, w_all[:, gsl], (((0,), (0,)), ((), ())), preferred_element_type=f32)
        state_ref[g] = s_old * e_edge[:, gsl] + upd
    y = jnp.concatenate(outs, axis=1)
    if not reverse:
        y = y + _expand_heads(dskip_ref[...], 0) * xs
    o_ref[0] = y.astype(bf16)


def _ssd_scan(xact, dt_pad, alog_pad, dtb_pad, dskip_pad):
    b, s, w = xact.shape
    L = _pick(s, 128)
    nc = s // L
    full = lambda shape: pl.BlockSpec(shape, lambda b_, c_: (0,) * len(shape))
    fwd = lambda width: pl.BlockSpec((1, L, width), lambda b_, c_: (b_, c_, 0))
    bwd = lambda width: pl.BlockSpec((1, L, width), lambda b_, c_: (b_, nc - 1 - c_, 0))
    spread, cols = _ssd_selectors(L)
    return pl.pallas_call(
        _ssd_body,
        grid=(b, nc),
        in_specs=[fwd(w), bwd(w), fwd(LANES), bwd(LANES), full((1, LANES)), full((1, LANES)), full((1, LANES)),
                  full(spread.shape), full(cols.shape)],
        out_specs=[fwd(SSD_WIDTH), bwd(SSD_WIDTH)],
        out_shape=[jax.ShapeDtypeStruct((b, s, SSD_WIDTH), bf16)] * 2,
        scratch_shapes=[pltpu.VMEM((2, SSD_GROUPS, SSD_STATE, GROUP_WIDTH), f32)],
        compiler_params=_params(("parallel", "arbitrary")),
        name="ssd_scan",
    )(xact, xact, dt_pad, dt_pad, alog_pad, dtb_pad, dskip_pad, spread, cols)


def _rms(x, g):
    ms = jnp.mean(x * x, axis=-1, keepdims=True)
    return x * lax.rsqrt(ms + EPS) * g


def _mix_out_body(x_ref, hf_ref, hb_ref, gate_ref, yf_ref, yb_ref, z_ref, gn_ref, wo_ref,
                  gf_ref, wrh_ref, wrl_ref, xo_ref, hn_ref, lg_ref):
    up = lambda ref: ref[...].astype(f32)
    y_lru = (up(hf_ref) + up(hb_ref)) * jax.nn.gelu(up(gate_ref))
    y = (up(yf_ref) + up(yb_ref)) * jax.nn.silu(up(z_ref))
    parts = []
    for g in range(SSD_GROUPS):
        yg = y[:, g * GROUP_WIDTH:(g + 1) * GROUP_WIDTH]
        ms = jnp.mean(yg * yg, axis=-1, keepdims=True)
        parts.append(yg * lax.rsqrt(ms + EPS))
    y_ssd = jnp.concatenate(parts, axis=1) * gn_ref[...]
    mix = jnp.concatenate([y_lru, y_ssd], axis=1).astype(bf16)
    xn = x_ref[...] + jnp.dot(mix, wo_ref[...], preferred_element_type=f32)
    xo_ref[...] = xn
    hn = _rms(xn, gf_ref[...])
    tm = hn.shape[0]
    for k in range(hn.shape[1] // LANES):
        hn_ref[pl.ds(k, tm, stride=SUBLANES), :] = hn[:, k * LANES:(k + 1) * LANES]
    h_hi = hn.astype(bf16)
    h_lo = (hn - h_hi.astype(f32)).astype(bf16)
    lg = jnp.dot(h_hi, wrh_ref[...], preferred_element_type=f32)
    lg = lg + jnp.dot(h_lo, wrh_ref[...], preferred_element_type=f32)
    lg = lg + jnp.dot(h_hi, wrl_ref[...], preferred_element_type=f32)
    lg_ref[...] = lg


def _mix_out(x2d, hf, hb, gate, yf, yb, z, ssd_norm, w_out_bf16, norm_ffn, wr_hi, wr_lo):
    t, d = x2d.shape
    tm = _pick(t, 256)
    row = lambda w: pl.BlockSpec((tm, w), lambda i: (i, 0))
    full = lambda shape: pl.BlockSpec(shape, lambda i: (0,) * len(shape))
    return pl.pallas_call(
        _mix_out_body,
        grid=(t // tm,),
        in_specs=[row(d)] + [row(LRU_WIDTH)] * 3 + [row(SSD_WIDTH)] * 3 + [
            full((1, SSD_WIDTH)), full((LRU_WIDTH + SSD_WIDTH, d)), full((1, d)),
            full((d, LANES)), full((d, LANES)),
        ],
        out_specs=[row(d), pl.BlockSpec((tm * d // LANES, LANES), lambda i: (i, 0)), row(LANES)],
        out_shape=[jax.ShapeDtypeStruct((t, d), f32), jax.ShapeDtypeStruct((t * d // LANES, LANES), f32),
                   jax.ShapeDtypeStruct((t, LANES), f32)],
        compiler_params=_params(("parallel",), VMEM_LIMIT),
        name="mix_out",
    )(x2d, hf, hb, gate, yf, yb, z, ssd_norm.reshape(1, -1), w_out_bf16, norm_ffn.reshape(1, d),
      wr_hi, wr_lo)


def _tile_prefix(tiles, upper_incl, upper_strict, lane):
    incs = [jnp.dot(t.astype(bf16), upper_incl, preferred_element_type=f32) for t in tiles]
    tot = jnp.zeros((N_EXPERTS, LANES), f32)
    for j, inc in enumerate(incs):
        tot = jnp.where(lane == j, inc[:, LANES - 1:LANES], tot)
    start = jnp.dot(tot.astype(bf16), upper_strict, preferred_element_type=f32)
    return incs, tot, start


def _route_body(lg_ref, idx_ref, gate_ref, lpos_ref, tab_ref, lm_scr, vt_scr, list_scr, tabv_scr, tabs_scr, sem,
                *, cap):
    s = lg_ref.shape[1]
    nt = s // LANES
    lt = lg_ref[0].T[:N_EXPERTS]
    mx = jnp.max(lt, axis=0, keepdims=True)
    ex = jnp.exp(lt - mx)
    aff = ex / jnp.sum(ex, axis=0, keepdims=True)
    key = pltpu.bitcast(aff, i32)

    def search(i, thr):
        cand = thr | (jnp.int32(1) << (30 - i))
        cnt = jnp.sum((key >= cand).astype(f32), axis=1, keepdims=True)
        return jnp.where(cnt >= float(cap), cand, thr)

    thr = lax.fori_loop(0, 31, search, jnp.zeros((N_EXPERTS, 1), i32))
    gt = key > thr
    eq = key == thr
    need = float(cap) - jnp.sum(gt.astype(f32), axis=1, keepdims=True)

    sub_i = lax.broadcasted_iota(i32, (LANES, LANES), 0)
    lane_i = lax.broadcasted_iota(i32, (LANES, LANES), 1)
    upper_incl = (sub_i <= lane_i).astype(bf16)
    upper_strict = (sub_i < lane_i).astype(bf16)
    lane_e = lax.broadcasted_iota(i32, (N_EXPERTS, LANES), 1)
    tiles = lambda a: [a[:, j * LANES:(j + 1) * LANES] for j in range(nt)]

    eq_t = tiles(eq.astype(f32))
    incs, _, start = _tile_prefix(eq_t, upper_incl, upper_strict, lane_e)
    sel_t = []
    for j, (gtj, eqj) in enumerate(zip(tiles(gt), eq_t)):
        excl = incs[j] - eqj + start[:, j:j + 1]
        sel_t.append(jnp.logical_or(gtj, jnp.logical_and(eqj > 0.5, excl < need)).astype(f32))

    incs, tot, start = _tile_prefix(sel_t, upper_incl, upper_strict, lane_e)
    start8 = jnp.floor(start * (1.0 / BF16_ROWS)) * float(BF16_ROWS)
    tab_ref[0, 0] = start8.astype(i32)
    tab_ref[0, 1] = tot.astype(i32)
    tab_ref[0, 2] = (start - start8 + tot).astype(i32)
    tabv_scr[...] = start.astype(i32)
    to_smem = pltpu.make_async_copy(tabv_scr, tabs_scr, sem)
    to_smem.start()

    kind = lax.broadcasted_iota(i32, (SUBLANES, LANES), 0)
    tok_lane = lax.broadcasted_iota(i32, (SUBLANES, LANES), 1).astype(f32)
    fill = jnp.full((LANES - N_EXPERTS, LANES), -1.0, f32)
    zrows = jnp.zeros((LANES - 3 * N_EXPERTS - SUBLANES, LANES), f32)
    for j, a in enumerate(tiles(aff)):
        lm = jnp.where(sel_t[j] > 0.5, incs[j] - sel_t[j], -1.0)
        lm_scr[j] = lm
        shifted = jnp.where(sel_t[j] > 0.5, lm + (start[:, j:j + 1] - start8[:, j:j + 1]), -1.0)
        lpos_ref[0, pl.ds(j * LANES, LANES), :] = jnp.concatenate([shifted, fill], axis=0).T
        a_hi = a.astype(bf16).astype(f32)
        a_mid = (a - a_hi).astype(bf16).astype(f32)
        a_lo = (a - a_hi - a_mid).astype(bf16).astype(f32)
        tok = jnp.where(kind == 0, float(j), jnp.where(kind == 1, tok_lane, 0.0))
        vt_scr[j] = jnp.concatenate([a_hi, a_mid, a_lo, tok, zrows], axis=0).T.astype(bf16)

    to_smem.wait()
    rank = sub_i.astype(f32)

    def compact(j, carry):
        lm = lm_scr[j]
        vt = vt_scr[j]
        for e in range(N_EXPERTS):
            onehot = (lm[e:e + 1] == rank).astype(bf16)
            packed = jnp.dot(onehot, vt, preferred_element_type=f32)
            list_scr[e, pl.ds(tabs_scr[e, j], LANES), :] = packed
        return carry

    lax.fori_loop(0, nt, compact, 0)

    lane_c = lax.broadcasted_iota(i32, (cap, LANES), 1)
    idx_c = jnp.zeros((cap, LANES), f32)
    gate_c = jnp.zeros((cap, LANES), f32)
    for e in range(N_EXPERTS):
        rows = list_scr[e, 0:cap, :]
        g = rows[:, e:e + 1] + rows[:, N_EXPERTS + e:N_EXPERTS + e + 1] + rows[:, 2 * N_EXPERTS + e:2 * N_EXPERTS + e + 1]
        t = rows[:, 3 * N_EXPERTS:3 * N_EXPERTS + 1] * float(LANES) + rows[:, 3 * N_EXPERTS + 1:3 * N_EXPERTS + 2]
        idx_c = jnp.where(lane_c == e, t, idx_c)
        gate_c = jnp.where(lane_c == e, g, gate_c)
    idx_ref[0] = idx_c.T[:N_EXPERTS].astype(i32)
    gate_ref[0] = gate_c.T[:N_EXPERTS]


def _route(logits, cap):
    b, s, _ = logits.shape
    nt = s // LANES
    assert nt <= LANES
    return pl.pallas_call(
        functools.partial(_route_body, cap=cap),
        grid=(b,),
        in_specs=[pl.BlockSpec((1, s, LANES), lambda i: (i, 0, 0))],
        out_specs=[pl.BlockSpec((1, N_EXPERTS, cap), lambda i: (i, 0, 0)),
                   pl.BlockSpec((1, N_EXPERTS, cap), lambda i: (i, 0, 0)),
                   pl.BlockSpec((1, s, LANES), lambda i: (i, 0, 0)),
                   pl.BlockSpec((1, 3, N_EXPERTS, LANES), lambda i: (i, 0, 0, 0))],
        out_shape=[jax.ShapeDtypeStruct((b, N_EXPERTS, cap), i32),
                   jax.ShapeDtypeStruct((b, N_EXPERTS, cap), f32),
                   jax.ShapeDtypeStruct((b, s, LANES), f32),
                   jax.ShapeDtypeStruct((b, 3, N_EXPERTS, LANES), i32)],
        scratch_shapes=[pltpu.VMEM((nt, N_EXPERTS, LANES), f32),
                        pltpu.VMEM((nt, LANES, LANES), bf16),
                        pltpu.VMEM((N_EXPERTS, cap + LANES, LANES), f32),
                        pltpu.VMEM((N_EXPERTS, LANES), i32),
                        pltpu.SMEM((N_EXPERTS, LANES), i32),
                        pltpu.SemaphoreType.DMA],
        compiler_params=_params(("parallel",), VMEM_LIMIT),
        name="route",
    )(logits)


def _token_copy(src_hbm, tok, dst, r, sem):
    src = src_hbm.at[pl.ds(pl.multiple_of(tok * SUBLANES, SUBLANES), SUBLANES), :]
    return pltpu.make_async_copy(src, dst.at[pl.ds(pl.multiple_of(r * SUBLANES, SUBLANES), SUBLANES), :], sem)


def _moe_body(idx_ref, idxn_ref, gate_ref, wg_ref, wu_ref, wd_ref, hn_hbm, o_ref, xg, sem, *, cap, seq, nb):
    n = pl.program_id(0) * nb + pl.program_id(1)
    total = N_EXPERTS * nb
    slot = lax.rem(n, 2)
    other = 1 - slot

    @pl.when(n == 0)
    def _():
        def start(r, carry):
            _token_copy(hn_hbm, idx_ref[0, 0, 0, r], xg.at[slot], r, sem.at[slot]).start()
            return carry

        lax.fori_loop(0, cap, start, 0, unroll=8)

    def wait_all(buf):
        def wait(r, carry):
            _token_copy(hn_hbm, 0, xg.at[buf], r, sem.at[buf]).wait()
            return carry

        lax.fori_loop(0, cap, wait, 0, unroll=8)

    wait_all(slot)

    base_next = lax.rem(jnp.minimum(n + 1, total - 1), nb) * seq
    tm = min(cap, 256)
    ntile = wg_ref.shape[1] // LANES
    for m in range(cap // tm):
        rows = slice(m * tm, (m + 1) * tm)
        xm = jnp.concatenate(
            [xg[slot, pl.ds(m * tm * SUBLANES + k, tm, stride=SUBLANES), :] for k in range(ntile)],
            axis=1).astype(bf16)
        hg = jnp.dot(xm, wg_ref[0], preferred_element_type=f32)
        hu = jnp.dot(xm, wu_ref[0], preferred_element_type=f32)
        hid = (jax.nn.silu(hg) * hu).astype(bf16)
        y = jnp.dot(hid, wd_ref[0], preferred_element_type=f32) * gate_ref[0, 0, rows, :]
        o_ref[0, 0, rows, :] = y.astype(bf16)
        for r in range(m * tm, (m + 1) * tm):
            _token_copy(hn_hbm, base_next + idxn_ref[0, 0, 0, r], xg.at[other], r, sem.at[other]).start()
    o_ref[0, 0, cap:cap + WINDOW, :] = jnp.zeros((WINDOW, o_ref.shape[-1]), bf16)

    @pl.when(n == total - 1)
    def _():
        wait_all(other)


def _moe_ffn(hn_tiles, idx, gates, wg, wu, wd):
    b, _, cap = idx.shape
    d = wg.shape[1]
    s = hn_tiles.shape[0] * LANES // d // b
    idx4 = idx.reshape(b, N_EXPERTS, 1, cap)
    gates4 = gates.reshape(b, N_EXPERTS, cap, 1)
    def next_step(e, i):
        n1 = jnp.minimum(e * b + i + 1, N_EXPERTS * b - 1)
        return (lax.rem(n1, b), n1 // b, 0, 0)

    return pl.pallas_call(
        functools.partial(_moe_body, cap=cap, seq=s, nb=b),
        grid=(N_EXPERTS, b),
        in_specs=[
            pl.BlockSpec((1, 1, 1, cap), lambda e, i: (i, e, 0, 0), memory_space=pltpu.SMEM),
            pl.BlockSpec((1, 1, 1, cap), next_step, memory_space=pltpu.SMEM),
            pl.BlockSpec((1, 1, cap, 1), lambda e, i: (i, e, 0, 0)),
            pl.BlockSpec((1, d, D_FF), lambda e, i: (e, 0, 0)),
            pl.BlockSpec((1, d, D_FF), lambda e, i: (e, 0, 0)),
            pl.BlockSpec((1, D_FF, d), lambda e, i: (e, 0, 0)),
            pl.BlockSpec(memory_space=pl.ANY),
        ],
        out_specs=pl.BlockSpec((1, 1, cap + WINDOW, d), lambda e, i: (i, e, 0, 0)),
        out_shape=jax.ShapeDtypeStruct((b, N_EXPERTS, cap + WINDOW, d), bf16),
        scratch_shapes=[pltpu.VMEM((2, cap * d // LANES, LANES), f32), pltpu.SemaphoreType.DMA((2,))],
        compiler_params=_params(("arbitrary", "arbitrary"), VMEM_LIMIT),
        name="moe_ffn",
    )(idx4, idx4, gates4, wg, wu, wd, hn_tiles)


SLOT = 64


def _short_copy(yg_hbm, b, e, start, packed, slot, sem):
    src = yg_hbm.at[b, e, pl.ds(pl.multiple_of(start, BF16_ROWS), SLOT), :]
    return pltpu.make_async_copy(src, packed.at[slot, pl.ds(e * SLOT, SLOT), :], sem.at[0, slot])


def _full_copy(yg_hbm, b, e, start, spill, slot, sem):
    src = yg_hbm.at[b, e, pl.ds(pl.multiple_of(start, BF16_ROWS), WINDOW), :]
    return pltpu.make_async_copy(src, spill.at[slot, e], sem.at[1, slot])


def _combine_body(tab_ref, tabn_ref, lpos_ref, x_ref, g_ref, yg_hbm, o_ref, packed, spill, sem,
                  *, nb, nt, final_norm):
    b = pl.program_id(0)
    j = pl.program_id(1)
    n = b * nt + j
    slot = lax.rem(n, 2)

    def fetch(tab, bb, jj, buf):
        for e in range(N_EXPERTS):
            _short_copy(yg_hbm, bb, e, tab[0, 0, e, jj], packed, buf, sem).start()

            @pl.when(tab[0, 2, e, jj] > SLOT)
            def _(e=e):
                _full_copy(yg_hbm, bb, e, tab[0, 0, e, jj], spill, buf, sem).start()

    @pl.when(n == 0)
    def _():
        packed[...] = jnp.zeros_like(packed)
        spill[...] = jnp.zeros_like(spill)
        fetch(tab_ref, b, j, slot)

    @pl.when(n + 1 < nb * nt)
    def _():
        wrap = j + 1 == nt
        fetch(tabn_ref, jnp.where(wrap, b + 1, b), jnp.where(wrap, 0, j + 1), 1 - slot)

    lm = lpos_ref[0]
    lane = lax.broadcasted_iota(i32, (LANES, LANES), 1).astype(f32)
    per_tile = LANES // SLOT
    tiles = []
    for lt in range(N_EXPERTS // per_tile):
        hit = None
        for i in range(per_tile):
            e = lt * per_tile + i
            col = jnp.where(tab_ref[0, 2, e, j] <= SLOT, lm[:, e:e + 1], -1.0)
            col = jnp.broadcast_to(jnp.where(col >= 0.0, col + float(i * SLOT), -1.0), (LANES, LANES))
            h = col == lane
            hit = h if hit is None else jnp.logical_or(hit, h)
        tiles.append(hit.astype(bf16))
    place = jnp.concatenate(tiles, axis=1)

    for e in range(N_EXPERTS):
        _short_copy(yg_hbm, b, e, 0, packed, slot, sem).wait()
    o_ref[...] = x_ref[...] + jnp.dot(place, packed[slot], preferred_element_type=f32)

    row = lax.broadcasted_iota(i32, (LANES, WINDOW), 1).astype(f32)
    for e in range(N_EXPERTS):
        @pl.when(tab_ref[0, 2, e, j] > SLOT)
        def _(e=e):
            _full_copy(yg_hbm, b, e, 0, spill, slot, sem).wait()
            own = (lm[:, e:e + 1] == row).astype(bf16)
            o_ref[...] += jnp.dot(own, spill[slot, e], preferred_element_type=f32)

    if final_norm:
        o_ref[...] = _rms(o_ref[...], g_ref[...])


def _combine(x2d, lpos, tab, yg, nb, final_g=None):
    t, d = x2d.shape
    g = jnp.ones((1, d), f32) if final_g is None else final_g.reshape(1, d)
    nt = t // nb // LANES

    def next_batch(i, j):
        return (jnp.minimum(i + (j + 1) // nt, nb - 1), 0, 0, 0)

    return pl.pallas_call(
        functools.partial(_combine_body, nb=nb, nt=nt, final_norm=final_g is not None),
        grid=(nb, nt),
        in_specs=[
            pl.BlockSpec((1, 3, N_EXPERTS, LANES), lambda i, j: (i, 0, 0, 0), memory_space=pltpu.SMEM),
            pl.BlockSpec((1, 3, N_EXPERTS, LANES), next_batch, memory_space=pltpu.SMEM),
            pl.BlockSpec((1, LANES, LANES), lambda i, j: (i, j, 0)),
            pl.BlockSpec((LANES, d), lambda i, j: (i * nt + j, 0)),
            pl.BlockSpec((1, d), lambda i, j: (0, 0)),
            pl.BlockSpec(memory_space=pl.ANY),
        ],
        out_specs=pl.BlockSpec((LANES, d), lambda i, j: (i * nt + j, 0)),
        out_shape=jax.ShapeDtypeStruct((t, d), f32),
        scratch_shapes=[pltpu.VMEM((2, N_EXPERTS * SLOT, d), bf16), pltpu.VMEM((2, N_EXPERTS, WINDOW, d), bf16),
                        pltpu.SemaphoreType.DMA((2, 2))],
        compiler_params=_params(("arbitrary", "arbitrary")),
        name="moe_combine",
    )(tab, tab, lpos, x2d, g, yg)


def _pad_lanes(v):
    return jnp.zeros((1, LANES), f32).at[0, :v.shape[0]].set(v.astype(f32))


def _mixer(x3, norm_mix, w_in, lru_conv_w, lru_conv_b, lru_wa, lru_ba, lru_wi, lru_bi, lru_lambda,
           ssd_conv_w, ssd_conv_b, ssd_a_log, ssd_dt_bias, ssd_d):
    b, s, d = x3.shape
    x2d = x3.reshape(b * s, d)
    pad = jnp.zeros((d, LANES - 2 * SSD_HEADS), bf16)
    w_bf = jnp.concatenate([w_in.astype(bf16), pad], axis=1)
    xc, gate, z, xact, dt = _in_proj(x2d, norm_mix, w_bf, lru_conv_w, lru_conv_b, ssd_conv_w, ssd_conv_b, s)
    xc = xc.reshape(b, s, LRU_WIDTH)
    xact = xact.reshape(b, s, SSD_CONV_CH)
    dt = dt.reshape(b, s, LANES)

    w_gate = jnp.concatenate([lru_wa, lru_wi], axis=-1).astype(bf16)
    hs = _lru_scan(xc, w_gate, lru_ba, lru_bi, lru_lambda)
    alog_pad = _pad_lanes(ssd_a_log.reshape(-1))
    dtb_pad = _pad_lanes(ssd_dt_bias.reshape(-1))
    dskip_pad = _pad_lanes(ssd_d)
    ys = _ssd_scan(xact, dt, alog_pad, dtb_pad, dskip_pad)
    t = b * s
    return (hs[0].reshape(t, -1), hs[1].reshape(t, -1), gate, ys[0].reshape(t, -1), ys[1].reshape(t, -1), z)


def _layer(x3, norm_mix, w_in, lru_conv_w, lru_conv_b, lru_wa, lru_ba, lru_wi, lru_bi, lru_lambda,
           ssd_conv_w, ssd_conv_b, ssd_a_log, ssd_dt_bias, ssd_d, ssd_norm, w_out, norm_ffn, w_router,
           w_gate, w_up, w_down, final_g):
    b, s, d = x3.shape
    cap = max(1, CAPACITY_FACTOR * s // N_EXPERTS)
    assert s % LANES == 0 and cap % LANES == 0, "sequence length must give 128-aligned expert capacity"
    hf, hb, gate, yf, yb, z = _mixer(x3, norm_mix, w_in, lru_conv_w, lru_conv_b, lru_wa, lru_ba, lru_wi,
                                     lru_bi, lru_lambda, ssd_conv_w, ssd_conv_b, ssd_a_log, ssd_dt_bias, ssd_d)
    wr = jnp.zeros((d, LANES), f32).at[:, :N_EXPERTS].set(w_router)
    wr_hi = wr.astype(bf16)
    wr_lo = (wr - wr_hi.astype(f32)).astype(bf16)
    xn, hn, logits = _mix_out(x3.reshape(b * s, d), hf, hb, gate, yf, yb, z, ssd_norm, w_out.astype(bf16),
                              norm_ffn, wr_hi, wr_lo)
    idx, gates, lpos, tab = _route(logits.reshape(b, s, LANES), cap)
    yg = _moe_ffn(hn, idx, gates, w_gate.astype(bf16), w_up.astype(bf16), w_down.astype(bf16))
    return _combine(xn, lpos, tab, yg, b, final_g).reshape(b, s, d)


def kernel(x, norm_mix, w_in, lru_conv_w, lru_conv_b, lru_wa, lru_ba, lru_wi, lru_bi, lru_lambda, ssd_conv_w, ssd_conv_b, ssd_a_log, ssd_dt_bias, ssd_d, ssd_norm, w_out, norm_ffn, w_router, w_gate, w_up, w_down, norm_final):
    depth = norm_mix.shape[0]
    for l in range(depth):
        x = _layer(x, norm_mix[l], w_in[l], lru_conv_w[l], lru_conv_b[l], lru_wa[l], lru_ba[l], lru_wi[l],
                   lru_bi[l], lru_lambda[l], ssd_conv_w[l], ssd_conv_b[l], ssd_a_log[l], ssd_dt_bias[l],
                   ssd_d[l], ssd_norm[l], w_out[l], norm_ffn[l], w_router[l], w_gate[l], w_up[l], w_down[l],
                   norm_final if l == depth - 1 else None)
    return x
```

```python
import functools

import jax
import jax.numpy as jnp
from jax import lax
from jax.experimental import pallas as pl
from jax.experimental.pallas import tpu as pltpu

f32 = jnp.float32
bf16 = jnp.bfloat16
i32 = jnp.int32

D_MODEL = 1024
EPS = 1e-6
CONV_WIDTH = 4
LRU_WIDTH = 1024
LRU_HEADS = 8
LRU_BLOCK = 128
LRU_C = 8.0
SSD_WIDTH = 1024
SSD_HEADDIM = 64
SSD_HEADS = 16
SSD_GROUPS = 4
SSD_STATE = 128
SSD_GN = SSD_GROUPS * SSD_STATE
SSD_CONV_CH = SSD_WIDTH + 2 * SSD_GN
GROUP_WIDTH = SSD_WIDTH // SSD_GROUPS
N_EXPERTS = 16
CAPACITY_FACTOR = 2
D_FF = 2048

LANES = 128
SUBLANES = 8
HALO = SUBLANES
BF16_ROWS = 2 * SUBLANES
WINDOW = LANES + BF16_ROWS
VMEM_LIMIT = 56 * 1024 * 1024


def _pick(n, target):
    if n <= target:
        return n
    t = target
    while t >= LANES:
        if n % t == 0:
            return t
        t -= LANES
    return n


def _params(sem, vmem=None):
    return pltpu.CompilerParams(dimension_semantics=sem, vmem_limit_bytes=vmem)


IN_SEGMENTS = (LRU_WIDTH, LRU_WIDTH, SSD_WIDTH, SSD_CONV_CH, LANES)


def _conv_centred(pe, first, last, cw, cb):
    tm = pe.shape[0] - 2 * HALO
    before = jnp.where(first, 0.0, pe[:HALO])
    after = jnp.where(last, 0.0, pe[HALO + tm:])
    c = pe[HALO:HALO + tm]
    r8 = lax.broadcasted_iota(i32, (SUBLANES, pe.shape[1]), 0)
    back1 = pltpu.roll(c, 1, 0)
    fwd1 = pltpu.roll(c, tm - 1, 0)
    fwd2 = pltpu.roll(c, tm - 2, 0)
    back1 = jnp.concatenate([jnp.where(r8 == 0, before[HALO - 1:HALO], back1[:HALO]), back1[HALO:]], axis=0)
    fwd1 = jnp.concatenate([fwd1[:tm - HALO], jnp.where(r8 == HALO - 1, after[0:1], fwd1[tm - HALO:])], axis=0)
    tail2 = jnp.where(r8 == HALO - 2, after[0:1], jnp.where(r8 == HALO - 1, after[1:2], fwd2[tm - HALO:]))
    fwd2 = jnp.concatenate([fwd2[:tm - HALO], tail2], axis=0)
    return cb + back1 * cw[0:1] + c * cw[1:2] + fwd1 * cw[2:3] + fwd2 * cw[3:4]


def _in_proj_body(prev_ref, x_ref, next_ref, g_ref, w_ref, lcw_ref, lcb_ref, scw_ref, scb_ref,
                  xc_ref, gate_ref, z_ref, xa_ref, dt_ref, *, tiles_per_seq):
    pos = lax.rem(pl.program_id(0), tiles_per_seq)
    first = pos == 0
    last = pos == tiles_per_seq - 1
    tm = x_ref.shape[0]
    xe = jnp.concatenate([prev_ref[...], x_ref[...], next_ref[...]], axis=0)
    ms = jnp.mean(xe * xe, axis=-1, keepdims=True)
    hn = (xe * lax.rsqrt(ms + EPS) * g_ref[...]).astype(bf16)
    hc = hn[HALO:HALO + tm]
    o0, o1, o2, o3 = LRU_WIDTH, 2 * LRU_WIDTH, 2 * LRU_WIDTH + SSD_WIDTH, 2 * LRU_WIDTH + SSD_WIDTH + SSD_CONV_CH
    pe = jnp.dot(hn, w_ref[:, :o0], preferred_element_type=f32)
    xc_ref[...] = _conv_centred(pe, first, last, lcw_ref[...], lcb_ref[...]).astype(bf16)
    gate_ref[...] = jnp.dot(hc, w_ref[:, o0:o1], preferred_element_type=f32).astype(bf16)
    z_ref[...] = jnp.dot(hc, w_ref[:, o1:o2], preferred_element_type=f32).astype(bf16)
    pe = jnp.dot(hn, w_ref[:, o2:o3], preferred_element_type=f32)
    xa_ref[...] = jax.nn.silu(_conv_centred(pe, first, last, scw_ref[...], scb_ref[...])).astype(bf16)
    dt_ref[...] = jnp.dot(hc, w_ref[:, o3:], preferred_element_type=f32)


def _in_proj(x2d, g, w_bf16, lru_cw, lru_cb, ssd_cw, ssd_cb, seq):
    t, d = x2d.shape
    n = w_bf16.shape[1]
    tm = _pick(seq, 512)
    per = tm // HALO
    last_blk = t // HALO - 1
    full = lambda shape: pl.BlockSpec(shape, lambda i: (0,) * len(shape))
    return pl.pallas_call(
        functools.partial(_in_proj_body, tiles_per_seq=seq // tm),
        grid=(t // tm,),
        in_specs=[
            pl.BlockSpec((HALO, d), lambda i: (jnp.maximum(i * per - 1, 0), 0)),
            pl.BlockSpec((tm, d), lambda i: (i, 0)),
            pl.BlockSpec((HALO, d), lambda i: (jnp.minimum((i + 1) * per, last_blk), 0)),
            full((1, d)),
            pl.BlockSpec((d, n), lambda i: (0, 0), pipeline_mode=pl.Buffered(1)),
            full((CONV_WIDTH, LRU_WIDTH)), full((1, LRU_WIDTH)),
            full((CONV_WIDTH, SSD_CONV_CH)), full((1, SSD_CONV_CH)),
        ],
        out_specs=[pl.BlockSpec((tm, w), lambda i: (i, 0)) for w in IN_SEGMENTS],
        out_shape=[jax.ShapeDtypeStruct((t, w), dt) for w, dt in zip(IN_SEGMENTS, (bf16, bf16, bf16, bf16, f32))],
        compiler_params=_params(("parallel",), VMEM_LIMIT),
        name="in_proj",
    )(x2d, x2d, x2d, g.reshape(1, d), w_bf16, lru_cw, lru_cb.reshape(1, -1), ssd_cw, ssd_cb.reshape(1, -1))


def _scan_rows8(a, u, reverse):
    r = lax.broadcasted_iota(i32, a.shape, 1)
    for k in (1, 2, 4):
        shift = (SUBLANES - k) if reverse else k
        valid = (r < SUBLANES - k) if reverse else (r >= k)
        a_s = pltpu.roll(a, shift, 1)
        u_s = pltpu.roll(u, shift, 1)
        u = jnp.where(valid, a * u_s + u, u)
        a = jnp.where(valid, a * a_s, a)
    return a, u


def _lru_gates(x_ref, w_ref, ba_ref, bi_ref, lam_ref, a_scr, u_scr, d):
    tc = x_ref.shape[1]
    groups = tc // SUBLANES
    xc = x_ref[0]
    sp = jax.nn.softplus(-lam_ref[d:d + 1, :])
    for h in range(LRU_HEADS):
        sl = slice(h * LRU_BLOCK, (h + 1) * LRU_BLOCK)
        pre = jnp.dot(xc[:, sl], w_ref[d, h], preferred_element_type=f32)
        xh = xc[:, sl].astype(f32)
        r = jax.nn.sigmoid(pre[:, :LRU_BLOCK] + ba_ref[d:d + 1, sl])
        gi = jax.nn.sigmoid(pre[:, LRU_BLOCK:] + bi_ref[d:d + 1, sl])
        log_a = (-LRU_C) * r * sp[:, sl]
        a = jnp.exp(log_a)
        u = jnp.sqrt(1.0 - a * a) * (gi * xh)
        a3, u3 = _scan_rows8(a.reshape(groups, SUBLANES, LRU_BLOCK),
                             u.reshape(groups, SUBLANES, LRU_BLOCK), bool(d))
        a_scr[d, :, sl] = a3.reshape(tc, LRU_BLOCK)
        u_scr[d, :, sl] = u3.reshape(tc, LRU_BLOCK)


def _lru_body(xf_ref, xb_ref, w_ref, ba_ref, bi_ref, lam_ref, of_ref, ob_ref, carry_ref, a_scr, u_scr):
    tc = xf_ref.shape[1]
    npair = tc // (2 * SUBLANES)

    @pl.when(pl.program_id(1) == 0)
    def _():
        carry_ref[...] = jnp.zeros_like(carry_ref)

    _lru_gates(xf_ref, w_ref, ba_ref, bi_ref, lam_ref, a_scr, u_scr, 0)
    _lru_gates(xb_ref, w_ref, ba_ref, bi_ref, lam_ref, a_scr, u_scr, 1)

    def advance(d, o_ref, row, carry):
        halves = [None, None]
        for k in ((1, 0) if d else (0, 1)):
            rows = pl.ds(pl.multiple_of(row + k * SUBLANES, SUBLANES), SUBLANES)
            hblk = u_scr[d, rows, :] + a_scr[d, rows, :] * carry
            halves[k] = hblk
            carry = hblk[0:1] if d else hblk[SUBLANES - 1:SUBLANES]
        o_ref[0, pl.ds(row, 2 * SUBLANES), :] = jnp.concatenate(halves, axis=0).astype(bf16)
        return carry

    def step(g, carries):
        cf = advance(0, of_ref, pl.multiple_of(g * 2 * SUBLANES, 2 * SUBLANES), carries[0])
        cb = advance(1, ob_ref, pl.multiple_of((npair - 1 - g) * 2 * SUBLANES, 2 * SUBLANES), carries[1])
        return cf, cb

    cf, cb = lax.fori_loop(0, npair, step, (carry_ref[0:1, :], carry_ref[1:2, :]), unroll=2)
    carry_ref[0:1, :] = cf
    carry_ref[1:2, :] = cb


def _lru_scan(xc, w_gate, ba, bi, lam):
    b, s, w = xc.shape
    tc = _pick(s, 256)
    nc = s // tc
    full = lambda shape: pl.BlockSpec(shape, lambda b_, c_: (0,) * len(shape))
    fwd = pl.BlockSpec((1, tc, w), lambda b_, c_: (b_, c_, 0))
    bwd = pl.BlockSpec((1, tc, w), lambda b_, c_: (b_, nc - 1 - c_, 0))
    return pl.pallas_call(
        _lru_body,
        grid=(b, nc),
        in_specs=[fwd, bwd, full((2, LRU_HEADS, LRU_BLOCK, 2 * LRU_BLOCK)), full((2, w)), full((2, w)), full((2, w))],
        out_specs=[fwd, bwd],
        out_shape=[jax.ShapeDtypeStruct((b, s, w), bf16)] * 2,
        scratch_shapes=[pltpu.VMEM((2, w), f32), pltpu.VMEM((2, tc, w), f32), pltpu.VMEM((2, tc, w), f32)],
        compiler_params=_params(("parallel", "arbitrary")),
        name="lru_scan",
    )(xc, xc, w_gate, ba, bi, lam)


def _expand_heads(arr, base):
    rows = arr.shape[0]
    lane = lax.broadcasted_iota(i32, (rows, LANES), 1)
    tiles = []
    for k in range(SSD_HEADS // 2):
        c0 = arr[:, base + 2 * k:base + 2 * k + 1]
        c1 = arr[:, base + 2 * k + 1:base + 2 * k + 2]
        tiles.append(jnp.where(lane < SSD_HEADDIM, c0, c1))
    return jnp.concatenate(tiles, axis=1)


def _cumsum_rows(x, reverse):
    n = x.shape[0]
    r = lax.broadcasted_iota(i32, x.shape, 0)
    k = 1
    while k < n:
        if reverse:
            x = x + jnp.where(r < n - k, pltpu.roll(x, n - k, 0), 0.0)
        else:
            x = x + jnp.where(r >= k, pltpu.roll(x, k, 0), 0.0)
        k *= 2
    return x


def _ssd_body(xf_ref, xb_ref, dtf_ref, dtb_in_ref, alog_ref, dtb_ref, dskip_ref, sele_ref, selc_ref,
              of_ref, ob_ref, state_ref):
    @pl.when(pl.program_id(1) == 0)
    def _():
        state_ref[...] = jnp.zeros_like(state_ref)

    _ssd_dir(xf_ref, dtf_ref, alog_ref, dtb_ref, dskip_ref, sele_ref, selc_ref, of_ref, state_ref.at[0], False)
    _ssd_dir(xb_ref, dtb_in_ref, alog_ref, dtb_ref, dskip_ref, sele_ref, selc_ref, ob_ref, state_ref.at[1], True)


def _split_bf16(x, parts):
    out = []
    for _ in range(parts):
        p = x.astype(bf16)
        out.append(p)
        x = x - p.astype(f32)
    return jnp.concatenate(out, axis=1)


def _ssd_selectors(chunk):
    lane = jnp.arange(LANES)[:, None]
    spread, cols = [], []
    for d in range(2):
        head = lane - d * SSD_HEADS
        s = (head == jnp.arange(SSD_WIDTH)[None, :] // SSD_HEADDIM).astype(bf16)
        c = (head == jnp.arange(SSD_HEADS * chunk)[None, :] // chunk).astype(bf16)
        spread.append(jnp.concatenate([s, s], axis=0))
        cols.append(jnp.concatenate([c, c], axis=0))
    return jnp.stack(spread), jnp.stack(cols)


def _ssd_dir(x_ref, dt_ref, alog_ref, dtb_ref, dskip_ref, sele_ref, selc_ref, o_ref, state_ref, reverse):
    L = x_ref.shape[1]
    base = SSD_HEADS if reverse else 0
    xact = x_ref[0]
    xs = xact[:, :SSD_WIDTH].astype(f32)

    dt = jax.nn.softplus(dt_ref[0] + dtb_ref[...])
    d_a = dt * (-jnp.exp(alog_ref[...]))
    cum = _cumsum_rows(d_a, reverse)
    edge = cum[0:1] if reverse else cum[L - 1:L]
    cum_t = cum.T

    d = 1 if reverse else 0
    stack = jnp.concatenate([_split_bf16(dt, 2), _split_bf16(jnp.exp(cum), 2), _split_bf16(jnp.exp(edge - cum), 2)],
                            axis=0)
    spread = jnp.dot(stack, sele_ref[d], preferred_element_type=f32)
    dtx = spread[:L] * xs
    e_cum = spread[L:2 * L]
    e_end = spread[2 * L:]
    cum_cols = jnp.dot(_split_bf16(cum, 2), selc_ref[d], preferred_element_type=f32)
    e_edge = _expand_heads(jnp.exp(edge), base)
    w_all = (e_end * dtx).astype(bf16)
    dtx_b = dtx.astype(bf16)

    li = lax.broadcasted_iota(i32, (L, L), 0)
    si = lax.broadcasted_iota(i32, (L, L), 1)
    tri = (si >= li) if reverse else (li >= si)
    lane = lax.broadcasted_iota(i32, (L, LANES), 1)
    lo_half = lane < SSD_HEADDIM

    outs = []
    for g in range(SSD_GROUPS):
        bsl = slice(SSD_WIDTH + g * SSD_STATE, SSD_WIDTH + (g + 1) * SSD_STATE)
        csl = slice(SSD_WIDTH + SSD_GN + g * SSD_STATE, SSD_WIDTH + SSD_GN + (g + 1) * SSD_STATE)
        gsl = slice(g * GROUP_WIDTH, (g + 1) * GROUP_WIDTH)
        bm = xact[:, bsl]
        cm = xact[:, csl]
        cb = lax.dot_general(cm, bm, (((1,), (1,)), ((), ())), preferred_element_type=f32)
        s_old = state_ref[g]
        y_off = jnp.dot(cm, s_old.astype(bf16), preferred_element_type=f32) * e_cum[:, gsl]
        tiles = []
        for p in range(2):
            tsl = slice(g * GROUP_WIDTH + p * LANES, g * GROUP_WIDTH + (p + 1) * LANES)
            ms, rhs = [], []
            for q in range(2):
                h = g * 4 + 2 * p + q
                j = base + h
                seg = cum_cols[:, h * L:(h + 1) * L] - cum_t[j:j + 1, :]
                decay = jnp.exp(jnp.where(tri, seg, -jnp.inf))
                ms.append((cb * decay).astype(bf16))
                rhs.append(jnp.where(lo_half if q == 0 else jnp.logical_not(lo_half), dtx_b[:, tsl], 0.0))
            tiles.append(jnp.dot(jnp.concatenate(ms, axis=1), jnp.concatenate(rhs, axis=0).astype(bf16),
                                 preferred_element_type=f32))
        outs.append(jnp.concatenate(tiles, axis=1) + y_off)
        upd = lax.dot_general(bm, w_all[:, gsl], (((0,), (0,)), ((), ())), preferred_element_type=f32)
        state_ref[g] = s_old * e_edge[:, gsl] + upd
    y = jnp.concatenate(outs, axis=1)
    if not reverse:
        y = y + _expand_heads(dskip_ref[...], 0) * xs
    o_ref[0] = y.astype(bf16)


def _ssd_scan(xact, dt_pad, alog_pad, dtb_pad, dskip_pad):
    b, s, w = xact.shape
    L = _pick(s, 128)
    nc = s // L
    full = lambda shape: pl.BlockSpec(shape, lambda b_, c_: (0,) * len(shape))
    fwd = lambda width: pl.BlockSpec((1, L, width), lambda b_, c_: (b_, c_, 0))
    bwd = lambda width: pl.BlockSpec((1, L, width), lambda b_, c_: (b_, nc - 1 - c_, 0))
    spread, cols = _ssd_selectors(L)
    return pl.pallas_call(
        _ssd_body,
        grid=(b, nc),
        in_specs=[fwd(w), bwd(w), fwd(LANES), bwd(LANES), full((1, LANES)), full((1, LANES)), full((1, LANES)),
                  full(spread.shape), full(cols.shape)],
        out_specs=[fwd(SSD_WIDTH), bwd(SSD_WIDTH)],
        out_shape=[jax.ShapeDtypeStruct((b, s, SSD_WIDTH), bf16)] * 2,
        scratch_shapes=[pltpu.VMEM((2, SSD_GROUPS, SSD_STATE, GROUP_WIDTH), f32)],
        compiler_params=_params(("parallel", "arbitrary")),
        name="ssd_scan",
    )(xact, xact, dt_pad, dt_pad, alog_pad, dtb_pad, dskip_pad, spread, cols)


def _rms(x, g):
    ms = jnp.mean(x * x, axis=-1, keepdims=True)
    return x * lax.rsqrt(ms + EPS) * g


def _mix_out_body(x_ref, hf_ref, hb_ref, gate_ref, yf_ref, yb_ref, z_ref, gn_ref, wo_ref,
                  gf_ref, wrh_ref, wrl_ref, xo_ref, hn_ref, lg_ref):
    up = lambda ref: ref[...].astype(f32)
    y_lru = (up(hf_ref) + up(hb_ref)) * jax.nn.gelu(up(gate_ref))
    y = (up(yf_ref) + up(yb_ref)) * jax.nn.silu(up(z_ref))
    parts = []
    for g in range(SSD_GROUPS):
        yg = y[:, g * GROUP_WIDTH:(g + 1) * GROUP_WIDTH]
        ms = jnp.mean(yg * yg, axis=-1, keepdims=True)
        parts.append(yg * lax.rsqrt(ms + EPS))
    y_ssd = jnp.concatenate(parts, axis=1) * gn_ref[...]
    mix = jnp.concatenate([y_lru, y_ssd], axis=1).astype(bf16)
    xn = x_ref[...] + jnp.dot(mix, wo_ref[...], preferred_element_type=f32)
    xo_ref[...] = xn
    hn = _rms(xn, gf_ref[...])
    tm = hn.shape[0]
    for k in range(hn.shape[1] // LANES):
        hn_ref[pl.ds(k, tm, stride=SUBLANES), :] = hn[:, k * LANES:(k + 1) * LANES]
    h_hi = hn.astype(bf16)
    h_lo = (hn - h_hi.astype(f32)).astype(bf16)
    lg = jnp.dot(h_hi, wrh_ref[...], preferred_element_type=f32)
    lg = lg + jnp.dot(h_lo, wrh_ref[...], preferred_element_type=f32)
    lg = lg + jnp.dot(h_hi, wrl_ref[...], preferred_element_type=f32)
    lg_ref[...] = lg


def _mix_out(x2d, hf, hb, gate, yf, yb, z, ssd_norm, w_out_bf16, norm_ffn, wr_hi, wr_lo):
    t, d = x2d.shape
    tm = _pick(t, 256)
    row = lambda w: pl.BlockSpec((tm, w), lambda i: (i, 0))
    full = lambda shape: pl.BlockSpec(shape, lambda i: (0,) * len(shape))
    return pl.pallas_call(
        _mix_out_body,
        grid=(t // tm,),
        in_specs=[row(d)] + [row(LRU_WIDTH)] * 3 + [row(SSD_WIDTH)] * 3 + [
            full((1, SSD_WIDTH)), full((LRU_WIDTH + SSD_WIDTH, d)), full((1, d)),
            full((d, LANES)), full((d, LANES)),
        ],
        out_specs=[row(d), pl.BlockSpec((tm * d // LANES, LANES), lambda i: (i, 0)), row(LANES)],
        out_shape=[jax.ShapeDtypeStruct((t, d), f32), jax.ShapeDtypeStruct((t * d // LANES, LANES), f32),
                   jax.ShapeDtypeStruct((t, LANES), f32)],
        compiler_params=_params(("parallel",), VMEM_LIMIT),
        name="mix_out",
    )(x2d, hf, hb, gate, yf, yb, z, ssd_norm.reshape(1, -1), w_out_bf16, norm_ffn.reshape(1, d),
      wr_hi, wr_lo)


def _tile_prefix(tiles, upper_incl, upper_strict, lane):
    incs = [jnp.dot(t.astype(bf16), upper_incl, preferred_element_type=f32) for t in tiles]
    tot = jnp.zeros((N_EXPERTS, LANES), f32)
    for j, inc in enumerate(incs):
        tot = jnp.where(lane == j, inc[:, LANES - 1:LANES], tot)
    start = jnp.dot(tot.astype(bf16), upper_strict, preferred_element_type=f32)
    return incs, tot, start


def _route_body(lg_ref, idx_ref, gate_ref, lpos_ref, tab_ref, lm_scr, vt_scr, list_scr, tabv_scr, tabs_scr, sem,
                *, cap):
    s = lg_ref.shape[1]
    nt = s // LANES
    lt = lg_ref[0].T[:N_EXPERTS]
    mx = jnp.max(lt, axis=0, keepdims=True)
    ex = jnp.exp(lt - mx)
    aff = ex / jnp.sum(ex, axis=0, keepdims=True)
    key = pltpu.bitcast(aff, i32)

    def search(i, thr):
        cand = thr | (jnp.int32(1) << (30 - i))
        cnt = jnp.sum((key >= cand).astype(f32), axis=1, keepdims=True)
        return jnp.where(cnt >= float(cap), cand, thr)

    thr = lax.fori_loop(0, 31, search, jnp.zeros((N_EXPERTS, 1), i32))
    gt = key > thr
    eq = key == thr
    need = float(cap) - jnp.sum(gt.astype(f32), axis=1, keepdims=True)

    sub_i = lax.broadcasted_iota(i32, (LANES, LANES), 0)
    lane_i = lax.broadcasted_iota(i32, (LANES, LANES), 1)
    upper_incl = (sub_i <= lane_i).astype(bf16)
    upper_strict = (sub_i < lane_i).astype(bf16)
    lane_e = lax.broadcasted_iota(i32, (N_EXPERTS, LANES), 1)
    tiles = lambda a: [a[:, j * LANES:(j + 1) * LANES] for j in range(nt)]

    eq_t = tiles(eq.astype(f32))
    incs, _, start = _tile_prefix(eq_t, upper_incl, upper_strict, lane_e)
    sel_t = []
    for j, (gtj, eqj) in enumerate(zip(tiles(gt), eq_t)):
        excl = incs[j] - eqj + start[:, j:j + 1]
        sel_t.append(jnp.logical_or(gtj, jnp.logical_and(eqj > 0.5, excl < need)).astype(f32))

    incs, tot, start = _tile_prefix(sel_t, upper_incl, upper_strict, lane_e)
    start8 = jnp.floor(start * (1.0 / BF16_ROWS)) * float(BF16_ROWS)
    tab_ref[0, 0] = start8.astype(i32)
    tab_ref[0, 1] = tot.astype(i32)
    tab_ref[0, 2] = (start - start8 + tot).astype(i32)
    tabv_scr[...] = start.astype(i32)
    to_smem = pltpu.make_async_copy(tabv_scr, tabs_scr, sem)
    to_smem.start()

    kind = lax.broadcasted_iota(i32, (SUBLANES, LANES), 0)
    tok_lane = lax.broadcasted_iota(i32, (SUBLANES, LANES), 1).astype(f32)
    fill = jnp.full((LANES - N_EXPERTS, LANES), -1.0, f32)
    zrows = jnp.zeros((LANES - 3 * N_EXPERTS - SUBLANES, LANES), f32)
    for j, a in enumerate(tiles(aff)):
        lm = jnp.where(sel_t[j] > 0.5, incs[j] - sel_t[j], -1.0)
        lm_scr[j] = lm
        shifted = jnp.where(sel_t[j] > 0.5, lm + (start[:, j:j + 1] - start8[:, j:j + 1]), -1.0)
        lpos_ref[0, pl.ds(j * LANES, LANES), :] = jnp.concatenate([shifted, fill], axis=0).T
        a_hi = a.astype(bf16).astype(f32)
        a_mid = (a - a_hi).astype(bf16).astype(f32)
        a_lo = (a - a_hi - a_mid).astype(bf16).astype(f32)
        tok = jnp.where(kind == 0, float(j), jnp.where(kind == 1, tok_lane, 0.0))
        vt_scr[j] = jnp.concatenate([a_hi, a_mid, a_lo, tok, zrows], axis=0).T.astype(bf16)

    to_smem.wait()
    rank = sub_i.astype(f32)

    def compact(j, carry):
        lm = lm_scr[j]
        vt = vt_scr[j]
        for e in range(N_EXPERTS):
            onehot = (lm[e:e + 1] == rank).astype(bf16)
            packed = jnp.dot(onehot, vt, preferred_element_type=f32)
            list_scr[e, pl.ds(tabs_scr[e, j], LANES), :] = packed
        return carry

    lax.fori_loop(0, nt, compact, 0)

    lane_c = lax.broadcasted_iota(i32, (cap, LANES), 1)
    idx_c = jnp.zeros((cap, LANES), f32)
    gate_c = jnp.zeros((cap, LANES), f32)
    for e in range(N_EXPERTS):
        rows = list_scr[e, 0:cap, :]
        g = rows[:, e:e + 1] + rows[:, N_EXPERTS + e:N_EXPERTS + e + 1] + rows[:, 2 * N_EXPERTS + e:2 * N_EXPERTS + e + 1]
        t = rows[:, 3 * N_EXPERTS:3 * N_EXPERTS + 1] * float(LANES) + rows[:, 3 * N_EXPERTS + 1:3 * N_EXPERTS + 2]
        idx_c = jnp.where(lane_c == e, t, idx_c)
        gate_c = jnp.where(lane_c == e, g, gate_c)
    idx_ref[0] = idx_c.T[:N_EXPERTS].astype(i32)
    gate_ref[0] = gate_c.T[:N_EXPERTS]


def _route(logits, cap):
    b, s, _ = logits.shape
    nt = s // LANES
    assert nt <= LANES
    return pl.pallas_call(
        functools.partial(_route_body, cap=cap),
        grid=(b,),
        in_specs=[pl.BlockSpec((1, s, LANES), lambda i: (i, 0, 0))],
        out_specs=[pl.BlockSpec((1, N_EXPERTS, cap), lambda i: (i, 0, 0)),
                   pl.BlockSpec((1, N_EXPERTS, cap), lambda i: (i, 0, 0)),
                   pl.BlockSpec((1, s, LANES), lambda i: (i, 0, 0)),
                   pl.BlockSpec((1, 3, N_EXPERTS, LANES), lambda i: (i, 0, 0, 0))],
        out_shape=[jax.ShapeDtypeStruct((b, N_EXPERTS, cap), i32),
                   jax.ShapeDtypeStruct((b, N_EXPERTS, cap), f32),
                   jax.ShapeDtypeStruct((b, s, LANES), f32),
                   jax.ShapeDtypeStruct((b, 3, N_EXPERTS, LANES), i32)],
        scratch_shapes=[pltpu.VMEM((nt, N_EXPERTS, LANES), f32),
                        pltpu.VMEM((nt, LANES, LANES), bf16),
                        pltpu.VMEM((N_EXPERTS, cap + LANES, LANES), f32),
                        pltpu.VMEM((N_EXPERTS, LANES), i32),
                        pltpu.SMEM((N_EXPERTS, LANES), i32),
                        pltpu.SemaphoreType.DMA],
        compiler_params=_params(("parallel",), VMEM_LIMIT),
        name="route",
    )(logits)


def _token_copy(src_hbm, tok, dst, r, sem):
    src = src_hbm.at[pl.ds(pl.multiple_of(tok * SUBLANES, SUBLANES), SUBLANES), :]
    return pltpu.make_async_copy(src, dst.at[pl.ds(pl.multiple_of(r * SUBLANES, SUBLANES), SUBLANES), :], sem)


def _moe_body(idx_ref, idxn_ref, gate_ref, wg_ref, wu_ref, wd_ref, hn_hbm, o_ref, xg, sem, *, cap, seq, nb):
    n = pl.program_id(0) * nb + pl.program_id(1)
    total = N_EXPERTS * nb
    slot = lax.rem(n, 2)
    other = 1 - slot

    @pl.when(n == 0)
    def _():
        def start(r, carry):
            _token_copy(hn_hbm, idx_ref[0, 0, 0, r], xg.at[slot], r, sem.at[slot]).start()
            return carry

        lax.fori_loop(0, cap, start, 0, unroll=8)

    def wait_all(buf):
        def wait(r, carry):
            _token_copy(hn_hbm, 0, xg.at[buf], r, sem.at[buf]).wait()
            return carry

        lax.fori_loop(0, cap, wait, 0, unroll=8)

    wait_all(slot)

    base_next = lax.rem(jnp.minimum(n + 1, total - 1), nb) * seq
    tm = min(cap, 256)
    ntile = wg_ref.shape[1] // LANES
    for m in range(cap // tm):
        rows = slice(m * tm, (m + 1) * tm)
        for r in range(m * tm, (m + 1) * tm):
            _token_copy(hn_hbm, base_next + idxn_ref[0, 0, 0, r], xg.at[other], r, sem.at[other]).start()
        xm = jnp.concatenate(
            [xg[slot, pl.ds(m * tm * SUBLANES + k, tm, stride=SUBLANES), :] for k in range(ntile)],
            axis=1).astype(bf16)
        hg = jnp.dot(xm, wg_ref[0], preferred_element_type=f32)
        hu = jnp.dot(xm, wu_ref[0], preferred_element_type=f32)
        hid = (jax.nn.silu(hg) * hu).astype(bf16)
        y = jnp.dot(hid, wd_ref[0], preferred_element_type=f32) * gate_ref[0, 0, rows, :]
        o_ref[0, 0, rows, :] = y.astype(bf16)
    o_ref[0, 0, cap:cap + WINDOW, :] = jnp.zeros((WINDOW, o_ref.shape[-1]), bf16)

    @pl.when(n == total - 1)
    def _():
        wait_all(other)


def _moe_ffn(hn_tiles, idx, gates, wg, wu, wd, layer):
    b, _, cap = idx.shape
    d = wg.shape[2]
    s = hn_tiles.shape[0] * LANES // d // b
    idx4 = idx.reshape(b, N_EXPERTS, 1, cap)
    gates4 = gates.reshape(b, N_EXPERTS, cap, 1)
    def next_step(e, i):
        n1 = jnp.minimum(e * b + i + 1, N_EXPERTS * b - 1)
        return (lax.rem(n1, b), n1 // b, 0, 0)

    return pl.pallas_call(
        functools.partial(_moe_body, cap=cap, seq=s, nb=b),
        grid=(N_EXPERTS, b),
        in_specs=[
            pl.BlockSpec((1, 1, 1, cap), lambda e, i: (i, e, 0, 0), memory_space=pltpu.SMEM),
            pl.BlockSpec((1, 1, 1, cap), next_step, memory_space=pltpu.SMEM),
            pl.BlockSpec((1, 1, cap, 1), lambda e, i: (i, e, 0, 0)),
            pl.BlockSpec((None, 1, d, D_FF), lambda e, i: (layer, e, 0, 0)),
            pl.BlockSpec((None, 1, d, D_FF), lambda e, i: (layer, e, 0, 0)),
            pl.BlockSpec((None, 1, D_FF, d), lambda e, i: (layer, e, 0, 0)),
            pl.BlockSpec(memory_space=pl.ANY),
        ],
        out_specs=pl.BlockSpec((1, 1, cap + WINDOW, d), lambda e, i: (i, e, 0, 0)),
        out_shape=jax.ShapeDtypeStruct((b, N_EXPERTS, cap + WINDOW, d), bf16),
        scratch_shapes=[pltpu.VMEM((2, cap * d // LANES, LANES), f32), pltpu.SemaphoreType.DMA((2,))],
        compiler_params=_params(("arbitrary", "arbitrary"), VMEM_LIMIT),
        name="moe_ffn",
    )(idx4, idx4, gates4, wg, wu, wd, hn_tiles)


SLOT = 64


def _short_copy(yg_hbm, b, e, start, packed, slot, sem):
    src = yg_hbm.at[b, e, pl.ds(pl.multiple_of(start, BF16_ROWS), SLOT), :]
    return pltpu.make_async_copy(src, packed.at[slot, pl.ds(e * SLOT, SLOT), :], sem.at[0, slot])


def _full_copy(yg_hbm, b, e, start, spill, slot, sem):
    src = yg_hbm.at[b, e, pl.ds(pl.multiple_of(start, BF16_ROWS), WINDOW), :]
    return pltpu.make_async_copy(src, spill.at[slot, e], sem.at[1, slot])


def _combine_body(tab_ref, tabn_ref, lpos_ref, x_ref, g_ref, yg_hbm, o_ref, packed, spill, sem,
                  *, nb, nt, final_norm):
    b = pl.program_id(0)
    j = pl.program_id(1)
    n = b * nt + j
    slot = lax.rem(n, 2)

    def fetch(tab, bb, jj, buf):
        for e in range(N_EXPERTS):
            _short_copy(yg_hbm, bb, e, tab[0, 0, e, jj], packed, buf, sem).start()

            @pl.when(tab[0, 2, e, jj] > SLOT)
            def _(e=e):
                _full_copy(yg_hbm, bb, e, tab[0, 0, e, jj], spill, buf, sem).start()

    @pl.when(n == 0)
    def _():
        packed[...] = jnp.zeros_like(packed)
        spill[...] = jnp.zeros_like(spill)
        fetch(tab_ref, b, j, slot)

    @pl.when(n + 1 < nb * nt)
    def _():
        wrap = j + 1 == nt
        fetch(tabn_ref, jnp.where(wrap, b + 1, b), jnp.where(wrap, 0, j + 1), 1 - slot)

    lm = lpos_ref[0]
    lane = lax.broadcasted_iota(i32, (LANES, LANES), 1).astype(f32)
    per_tile = LANES // SLOT
    tiles = []
    for lt in range(N_EXPERTS // per_tile):
        hit = None
        for i in range(per_tile):
            e = lt * per_tile + i
            col = jnp.where(tab_ref[0, 2, e, j] <= SLOT, lm[:, e:e + 1], -1.0)
            col = jnp.broadcast_to(jnp.where(col >= 0.0, col + float(i * SLOT), -1.0), (LANES, LANES))
            h = col == lane
            hit = h if hit is None else jnp.logical_or(hit, h)
        tiles.append(hit.astype(bf16))
    place = jnp.concatenate(tiles, axis=1)

    for e in range(N_EXPERTS):
        _short_copy(yg_hbm, b, e, 0, packed, slot, sem).wait()
    o_ref[...] = x_ref[...] + jnp.dot(place, packed[slot], preferred_element_type=f32)

    row = lax.broadcasted_iota(i32, (LANES, WINDOW), 1).astype(f32)
    for e in range(N_EXPERTS):
        @pl.when(tab_ref[0, 2, e, j] > SLOT)
        def _(e=e):
            _full_copy(yg_hbm, b, e, 0, spill, slot, sem).wait()
            own = (lm[:, e:e + 1] == row).astype(bf16)
            o_ref[...] += jnp.dot(own, spill[slot, e], preferred_element_type=f32)

    if final_norm:
        o_ref[...] = _rms(o_ref[...], g_ref[...])


def _combine(x2d, lpos, tab, yg, nb, final_g=None):
    t, d = x2d.shape
    g = jnp.ones((1, d), f32) if final_g is None else final_g.reshape(1, d)
    nt = t // nb // LANES

    def next_batch(i, j):
        return (jnp.minimum(i + (j + 1) // nt, nb - 1), 0, 0, 0)

    return pl.pallas_call(
        functools.partial(_combine_body, nb=nb, nt=nt, final_norm=final_g is not None),
        grid=(nb, nt),
        in_specs=[
            pl.BlockSpec((1, 3, N_EXPERTS, LANES), lambda i, j: (i, 0, 0, 0), memory_space=pltpu.SMEM),
            pl.BlockSpec((1, 3, N_EXPERTS, LANES), next_batch, memory_space=pltpu.SMEM),
            pl.BlockSpec((1, LANES, LANES), lambda i, j: (i, j, 0)),
            pl.BlockSpec((LANES, d), lambda i, j: (i * nt + j, 0)),
            pl.BlockSpec((1, d), lambda i, j: (0, 0)),
            pl.BlockSpec(memory_space=pl.ANY),
        ],
        out_specs=pl.BlockSpec((LANES, d), lambda i, j: (i * nt + j, 0)),
        out_shape=jax.ShapeDtypeStruct((t, d), f32),
        scratch_shapes=[pltpu.VMEM((2, N_EXPERTS * SLOT, d), bf16), pltpu.VMEM((2, N_EXPERTS, WINDOW, d), bf16),
                        pltpu.SemaphoreType.DMA((2, 2))],
        compiler_params=_params(("arbitrary", "arbitrary")),
        name="moe_combine",
    )(tab, tab, lpos, x2d, g, yg)


def _pad_lanes(v):
    return jnp.zeros((1, LANES), f32).at[0, :v.shape[0]].set(v.astype(f32))


def _mixer(x3, norm_mix, w_in, lru_conv_w, lru_conv_b, lru_wa, lru_ba, lru_wi, lru_bi, lru_lambda,
           ssd_conv_w, ssd_conv_b, ssd_a_log, ssd_dt_bias, ssd_d):
    b, s, d = x3.shape
    x2d = x3.reshape(b * s, d)
    pad = jnp.zeros((d, LANES - 2 * SSD_HEADS), bf16)
    w_bf = jnp.concatenate([w_in.astype(bf16), pad], axis=1)
    xc, gate, z, xact, dt = _in_proj(x2d, norm_mix, w_bf, lru_conv_w, lru_conv_b, ssd_conv_w, ssd_conv_b, s)
    xc = xc.reshape(b, s, LRU_WIDTH)
    xact = xact.reshape(b, s, SSD_CONV_CH)
    dt = dt.reshape(b, s, LANES)

    w_gate = jnp.concatenate([lru_wa, lru_wi], axis=-1).astype(bf16)
    hs = _lru_scan(xc, w_gate, lru_ba, lru_bi, lru_lambda)
    alog_pad = _pad_lanes(ssd_a_log.reshape(-1))
    dtb_pad = _pad_lanes(ssd_dt_bias.reshape(-1))
    dskip_pad = _pad_lanes(ssd_d)
    ys = _ssd_scan(xact, dt, alog_pad, dtb_pad, dskip_pad)
    t = b * s
    return (hs[0].reshape(t, -1), hs[1].reshape(t, -1), gate, ys[0].reshape(t, -1), ys[1].reshape(t, -1), z)


def _layer(x3, norm_mix, w_in, lru_conv_w, lru_conv_b, lru_wa, lru_ba, lru_wi, lru_bi, lru_lambda,
           ssd_conv_w, ssd_conv_b, ssd_a_log, ssd_dt_bias, ssd_d, ssd_norm, w_out, norm_ffn, w_router,
           expert_w, layer, final_g):
    b, s, d = x3.shape
    cap = max(1, CAPACITY_FACTOR * s // N_EXPERTS)
    assert s % LANES == 0 and cap % LANES == 0, "sequence length must give 128-aligned expert capacity"
    hf, hb, gate, yf, yb, z = _mixer(x3, norm_mix, w_in, lru_conv_w, lru_conv_b, lru_wa, lru_ba, lru_wi,
                                     lru_bi, lru_lambda, ssd_conv_w, ssd_conv_b, ssd_a_log, ssd_dt_bias, ssd_d)
    wr = jnp.zeros((d, LANES), f32).at[:, :N_EXPERTS].set(w_router)
    wr_hi = wr.astype(bf16)
    wr_lo = (wr - wr_hi.astype(f32)).astype(bf16)
    xn, hn, logits = _mix_out(x3.reshape(b * s, d), hf, hb, gate, yf, yb, z, ssd_norm, w_out.astype(bf16),
                              norm_ffn, wr_hi, wr_lo)
    idx, gates, lpos, tab = _route(logits.reshape(b, s, LANES), cap)
    yg = _moe_ffn(hn, idx, gates, *expert_w, layer)
    return _combine(xn, lpos, tab, yg, b, final_g).reshape(b, s, d)


def kernel(x, norm_mix, w_in, lru_conv_w, lru_conv_b, lru_wa, lru_ba, lru_wi, lru_bi, lru_lambda, ssd_conv_w, ssd_conv_b, ssd_a_log, ssd_dt_bias, ssd_d, ssd_norm, w_out, norm_ffn, w_router, w_gate, w_up, w_down, norm_final):
    depth = norm_mix.shape[0]
    expert_w = (w_gate.astype(bf16), w_up.astype(bf16), w_down.astype(bf16))
    for l in range(depth):
        x = _layer(x, norm_mix[l], w_in[l], lru_conv_w[l], lru_conv_b[l], lru_wa[l], lru_ba[l], lru_wi[l],
                   lru_bi[l], lru_lambda[l], ssd_conv_w[l], ssd_conv_b[l], ssd_a_log[l], ssd_dt_bias[l],
                   ssd_d[l], ssd_norm[l], w_out[l], norm_ffn[l], w_router[l], expert_w, l,
                   norm_final if l == depth - 1 else None)
    return x
```

```python
import functools

import jax
import jax.numpy as jnp
from jax import lax
from jax.experimental import pallas as pl
from jax.experimental.pallas import tpu as pltpu

f32 = jnp.float32
bf16 = jnp.bfloat16
i32 = jnp.int32

D_MODEL = 1024
EPS = 1e-6
CONV_WIDTH = 4
LRU_WIDTH = 1024
LRU_HEADS = 8
LRU_BLOCK = 128
LRU_C = 8.0
SSD_WIDTH = 1024
SSD_HEADDIM = 64
SSD_HEADS = 16
SSD_GROUPS = 4
SSD_STATE = 128
SSD_GN = SSD_GROUPS * SSD_STATE
SSD_CONV_CH = SSD_WIDTH + 2 * SSD_GN
GROUP_WIDTH = SSD_WIDTH // SSD_GROUPS
N_EXPERTS = 16
CAPACITY_FACTOR = 2
D_FF = 2048

LANES = 128
SUBLANES = 8
HALO = SUBLANES
BF16_ROWS = 2 * SUBLANES
WINDOW = LANES + BF16_ROWS
VMEM_LIMIT = 56 * 1024 * 1024


def _pick(n, target):
    if n <= target:
        return n
    t = target
    while t >= LANES:
        if n % t == 0:
            return t
        t -= LANES
    return n


def _params(sem, vmem=None):
    return pltpu.CompilerParams(dimension_semantics=sem, vmem_limit_bytes=vmem)


IN_SEGMENTS = (LRU_WIDTH, LRU_WIDTH, SSD_WIDTH, SSD_CONV_CH, LANES)


def _conv_centred(pe, first, last, cw, cb):
    tm = pe.shape[0] - 2 * HALO
    before = jnp.where(first, 0.0, pe[:HALO])
    after = jnp.where(last, 0.0, pe[HALO + tm:])
    c = pe[HALO:HALO + tm]
    r8 = lax.broadcasted_iota(i32, (SUBLANES, pe.shape[1]), 0)
    back1 = pltpu.roll(c, 1, 0)
    fwd1 = pltpu.roll(c, tm - 1, 0)
    fwd2 = pltpu.roll(c, tm - 2, 0)
    back1 = jnp.concatenate([jnp.where(r8 == 0, before[HALO - 1:HALO], back1[:HALO]), back1[HALO:]], axis=0)
    fwd1 = jnp.concatenate([fwd1[:tm - HALO], jnp.where(r8 == HALO - 1, after[0:1], fwd1[tm - HALO:])], axis=0)
    tail2 = jnp.where(r8 == HALO - 2, after[0:1], jnp.where(r8 == HALO - 1, after[1:2], fwd2[tm - HALO:]))
    fwd2 = jnp.concatenate([fwd2[:tm - HALO], tail2], axis=0)
    return cb + back1 * cw[0:1] + c * cw[1:2] + fwd1 * cw[2:3] + fwd2 * cw[3:4]


def _in_proj_body(prev_ref, x_ref, next_ref, g_ref, w_ref, lcw_ref, lcb_ref, scw_ref, scb_ref,
                  xc_ref, gate_ref, z_ref, xa_ref, dt_ref, *, tiles_per_seq):
    pos = lax.rem(pl.program_id(0), tiles_per_seq)
    first = pos == 0
    last = pos == tiles_per_seq - 1
    tm = x_ref.shape[0]
    xe = jnp.concatenate([prev_ref[...], x_ref[...], next_ref[...]], axis=0)
    ms = jnp.mean(xe * xe, axis=-1, keepdims=True)
    hn = (xe * lax.rsqrt(ms + EPS) * g_ref[...]).astype(bf16)
    hc = hn[HALO:HALO + tm]
    o0, o1, o2, o3 = LRU_WIDTH, 2 * LRU_WIDTH, 2 * LRU_WIDTH + SSD_WIDTH, 2 * LRU_WIDTH + SSD_WIDTH + SSD_CONV_CH
    pe = jnp.dot(hn, w_ref[:, :o0], preferred_element_type=f32)
    xc_ref[...] = _conv_centred(pe, first, last, lcw_ref[...], lcb_ref[...]).astype(bf16)
    gate_ref[...] = jnp.dot(hc, w_ref[:, o0:o1], preferred_element_type=f32).astype(bf16)
    z_ref[...] = jnp.dot(hc, w_ref[:, o1:o2], preferred_element_type=f32).astype(bf16)
    pe = jnp.dot(hn, w_ref[:, o2:o3], preferred_element_type=f32)
    xa_ref[...] = jax.nn.silu(_conv_centred(pe, first, last, scw_ref[...], scb_ref[...])).astype(bf16)
    dt_ref[...] = jnp.dot(hc, w_ref[:, o3:], preferred_element_type=f32)


def _in_proj(x2d, g, w_bf16, lru_cw, lru_cb, ssd_cw, ssd_cb, seq):
    t, d = x2d.shape
    n = w_bf16.shape[1]
    tm = _pick(seq, 512)
    per = tm // HALO
    last_blk = t // HALO - 1
    full = lambda shape: pl.BlockSpec(shape, lambda i: (0,) * len(shape))
    return pl.pallas_call(
        functools.partial(_in_proj_body, tiles_per_seq=seq // tm),
        grid=(t // tm,),
        in_specs=[
            pl.BlockSpec((HALO, d), lambda i: (jnp.maximum(i * per - 1, 0), 0)),
            pl.BlockSpec((tm, d), lambda i: (i, 0)),
            pl.BlockSpec((HALO, d), lambda i: (jnp.minimum((i + 1) * per, last_blk), 0)),
            full((1, d)),
            pl.BlockSpec((d, n), lambda i: (0, 0), pipeline_mode=pl.Buffered(1)),
            full((CONV_WIDTH, LRU_WIDTH)), full((1, LRU_WIDTH)),
            full((CONV_WIDTH, SSD_CONV_CH)), full((1, SSD_CONV_CH)),
        ],
        out_specs=[pl.BlockSpec((tm, w), lambda i: (i, 0)) for w in IN_SEGMENTS],
        out_shape=[jax.ShapeDtypeStruct((t, w), dt) for w, dt in zip(IN_SEGMENTS, (bf16, bf16, bf16, bf16, f32))],
        compiler_params=_params(("parallel",), VMEM_LIMIT),
        name="in_proj",
    )(x2d, x2d, x2d, g.reshape(1, d), w_bf16, lru_cw, lru_cb.reshape(1, -1), ssd_cw, ssd_cb.reshape(1, -1))


def _lru_gates(x_ref, perm_ref, w_ref, ba_ref, bi_ref, lam_ref, a_scr, u_scr, d):
    xc = jnp.dot(perm_ref[0], x_ref[0], preferred_element_type=f32).astype(bf16)
    sp = jax.nn.softplus(-lam_ref[d:d + 1, :])
    for h in range(LRU_HEADS):
        sl = slice(h * LRU_BLOCK, (h + 1) * LRU_BLOCK)
        pre = jnp.dot(xc[:, sl], w_ref[d, h], preferred_element_type=f32)
        xh = xc[:, sl].astype(f32)
        r = jax.nn.sigmoid(pre[:, :LRU_BLOCK] + ba_ref[d:d + 1, sl])
        gi = jax.nn.sigmoid(pre[:, LRU_BLOCK:] + bi_ref[d:d + 1, sl])
        log_a = (-LRU_C) * r * sp[:, sl]
        a = jnp.exp(log_a)
        u = jnp.sqrt(1.0 - a * a) * (gi * xh)
        a_scr[d, h] = a
        u_scr[d, h] = u


def _lru_direction(d, perm_ref, o_ref, carry_ref, a_scr, u_scr):
    tc = o_ref.shape[1]
    sub = tc // SUBLANES
    heads = range(LRU_HEADS)

    def block(j):
        jj = (sub - 1 - j) if d else j
        return pl.ds(pl.multiple_of(jj * SUBLANES, SUBLANES), SUBLANES)

    def local(j, state):
        rows = block(j)
        out = []
        for h in heads:
            a = a_scr[d, h, rows, :]
            out += [a * state[2 * h] + u_scr[d, h, rows, :], a * state[2 * h + 1]]
        return tuple(out)

    init = []
    for h in heads:
        init += [jnp.zeros((SUBLANES, LRU_BLOCK), f32), jnp.ones((SUBLANES, LRU_BLOCK), f32)]
    ends = lax.fori_loop(0, sub, local, tuple(init), unroll=True)

    sub_i = lax.broadcasted_iota(i32, (SUBLANES, LRU_BLOCK), 0)
    carries = []
    for h in heads:
        sl = slice(h * LRU_BLOCK, (h + 1) * LRU_BLOCK)
        c = carry_ref[d:d + 1, sl]
        cin = jnp.zeros((SUBLANES, LRU_BLOCK), f32)
        for s in (reversed(range(SUBLANES)) if d else range(SUBLANES)):
            cin = jnp.where(sub_i == s, c, cin)
            c = ends[2 * h][s:s + 1] + ends[2 * h + 1][s:s + 1] * c
        carry_ref[d:d + 1, sl] = c
        carries.append(cin)

    def final(j, state):
        rows = block(j)
        out = []
        for h in heads:
            hcur = a_scr[d, h, rows, :] * state[h] + u_scr[d, h, rows, :]
            u_scr[d, h, rows, :] = hcur
            out.append(hcur)
        return tuple(out)

    lax.fori_loop(0, sub, final, tuple(carries), unroll=True)

    hp = jnp.concatenate([u_scr[d, h].astype(bf16) for h in heads], axis=1)
    o_ref[0] = jnp.dot(perm_ref[1], hp, preferred_element_type=f32).astype(bf16)


def _lru_body(xf_ref, xb_ref, perm_ref, w_ref, ba_ref, bi_ref, lam_ref, of_ref, ob_ref, carry_ref, a_scr, u_scr):
    @pl.when(pl.program_id(1) == 0)
    def _():
        carry_ref[...] = jnp.zeros_like(carry_ref)

    _lru_gates(xf_ref, perm_ref, w_ref, ba_ref, bi_ref, lam_ref, a_scr, u_scr, 0)
    _lru_gates(xb_ref, perm_ref, w_ref, ba_ref, bi_ref, lam_ref, a_scr, u_scr, 1)
    _lru_direction(0, perm_ref, of_ref, carry_ref, a_scr, u_scr)
    _lru_direction(1, perm_ref, ob_ref, carry_ref, a_scr, u_scr)


def _lru_scan(xc, w_gate, ba, bi, lam):
    b, s, w = xc.shape
    tc = _pick(s, 256)
    nc = s // tc
    full = lambda shape: pl.BlockSpec(shape, lambda b_, c_: (0,) * len(shape))
    fwd = pl.BlockSpec((1, tc, w), lambda b_, c_: (b_, c_, 0))
    bwd = pl.BlockSpec((1, tc, w), lambda b_, c_: (b_, nc - 1 - c_, 0))
    p = jnp.arange(tc)
    to_strided = ((p % SUBLANES) * (tc // SUBLANES) + p // SUBLANES)[:, None] == jnp.arange(tc)[None, :]
    perm = jnp.stack([to_strided, to_strided.T]).astype(bf16)
    return pl.pallas_call(
        _lru_body,
        grid=(b, nc),
        in_specs=[fwd, bwd, full((2, tc, tc)), full((2, LRU_HEADS, LRU_BLOCK, 2 * LRU_BLOCK)),
                  full((2, w)), full((2, w)), full((2, w))],
        out_specs=[fwd, bwd],
        out_shape=[jax.ShapeDtypeStruct((b, s, w), bf16)] * 2,
        scratch_shapes=[pltpu.VMEM((2, w), f32), pltpu.VMEM((2, LRU_HEADS, tc, LRU_BLOCK), f32),
                        pltpu.VMEM((2, LRU_HEADS, tc, LRU_BLOCK), f32)],
        compiler_params=_params(("parallel", "arbitrary")),
        name="lru_scan",
    )(xc, xc, perm, w_gate, ba, bi, lam)


def _expand_heads(arr, base):
    rows = arr.shape[0]
    lane = lax.broadcasted_iota(i32, (rows, LANES), 1)
    tiles = []
    for k in range(SSD_HEADS // 2):
        c0 = arr[:, base + 2 * k:base + 2 * k + 1]
        c1 = arr[:, base + 2 * k + 1:base + 2 * k + 2]
        tiles.append(jnp.where(lane < SSD_HEADDIM, c0, c1))
    return jnp.concatenate(tiles, axis=1)


def _cumsum_rows(x, reverse):
    n = x.shape[0]
    r = lax.broadcasted_iota(i32, x.shape, 0)
    k = 1
    while k < n:
        if reverse:
            x = x + jnp.where(r < n - k, pltpu.roll(x, n - k, 0), 0.0)
        else:
            x = x + jnp.where(r >= k, pltpu.roll(x, k, 0), 0.0)
        k *= 2
    return x


def _ssd_body(xf_ref, xb_ref, dtf_ref, dtb_in_ref, alog_ref, dtb_ref, dskip_ref, sele_ref, selc_ref,
              of_ref, ob_ref, state_ref):
    @pl.when(pl.program_id(1) == 0)
    def _():
        state_ref[...] = jnp.zeros_like(state_ref)

    _ssd_dir(xf_ref, dtf_ref, alog_ref, dtb_ref, dskip_ref, sele_ref, selc_ref, of_ref, state_ref.at[0], False)
    _ssd_dir(xb_ref, dtb_in_ref, alog_ref, dtb_ref, dskip_ref, sele_ref, selc_ref, ob_ref, state_ref.at[1], True)


def _split_bf16(x, parts):
    out = []
    for _ in range(parts):
        p = x.astype(bf16)
        out.append(p)
        x = x - p.astype(f32)
    return jnp.concatenate(out, axis=1)


def _ssd_selectors(chunk):
    lane = jnp.arange(LANES)[:, None]
    spread, cols = [], []
    for d in range(2):
        head = lane - d * SSD_HEADS
        s = (head == jnp.arange(SSD_WIDTH)[None, :] // SSD_HEADDIM).astype(bf16)
        c = (head == jnp.arange(SSD_HEADS * chunk)[None, :] // chunk).astype(bf16)
        spread.append(jnp.concatenate([s, s], axis=0))
        cols.append(jnp.concatenate([c, c], axis=0))
    return jnp.stack(spread), jnp.stack(cols)


def _ssd_dir(x_ref, dt_ref, alog_ref, dtb_ref, dskip_ref, sele_ref, selc_ref, o_ref, state_ref, reverse):
    L = x_ref.shape[1]
    base = SSD_HEADS if reverse else 0
    xact = x_ref[0]
    xs = xact[:, :SSD_WIDTH].astype(f32)

    dt = jax.nn.softplus(dt_ref[0] + dtb_ref[...])
    d_a = dt * (-jnp.exp(alog_ref[...]))
    cum = _cumsum_rows(d_a, reverse)
    edge = cum[0:1] if reverse else cum[L - 1:L]
    cum_t = cum.T

    d = 1 if reverse else 0
    stack = jnp.concatenate([_split_bf16(dt, 2), _split_bf16(jnp.exp(cum), 2), _split_bf16(jnp.exp(edge - cum), 2)],
                            axis=0)
    spread = jnp.dot(stack, sele_ref[d], preferred_element_type=f32)
    dtx = spread[:L] * xs
    e_cum = spread[L:2 * L]
    e_end = spread[2 * L:]
    cum_cols = jnp.dot(_split_bf16(cum, 2), selc_ref[d], preferred_element_type=f32)
    e_edge = _expand_heads(jnp.exp(edge), base)
    w_all = (e_end * dtx).astype(bf16)
    dtx_b = dtx.astype(bf16)

    li = lax.broadcasted_iota(i32, (L, L), 0)
    si = lax.broadcasted_iota(i32, (L, L), 1)
    tri = (si >= li) if reverse else (li >= si)
    lane = lax.broadcasted_iota(i32, (L, LANES), 1)
    lo_half = lane < SSD_HEADDIM

    outs = []
    for g in range(SSD_GROUPS):
        bsl = slice(SSD_WIDTH + g * SSD_STATE, SSD_WIDTH + (g + 1) * SSD_STATE)
        csl = slice(SSD_WIDTH + SSD_GN + g * SSD_STATE, SSD_WIDTH + SSD_GN + (g + 1) * SSD_STATE)
        gsl = slice(g * GROUP_WIDTH, (g + 1) * GROUP_WIDTH)
        bm = xact[:, bsl]
        cm = xact[:, csl]
        cb = lax.dot_general(cm, bm, (((1,), (1,)), ((), ())), preferred_element_type=f32)
        s_old = state_ref[g]
        y_off = jnp.dot(cm, s_old.astype(bf16), preferred_element_type=f32) * e_cum[:, gsl]
        tiles = []
        for p in range(2):
            tsl = slice(g * GROUP_WIDTH + p * LANES, g * GROUP_WIDTH + (p + 1) * LANES)
            ms, rhs = [], []
            for q in range(2):
                h = g * 4 + 2 * p + q
                j = base + h
                seg = cum_cols[:, h * L:(h + 1) * L] - cum_t[j:j + 1, :]
                decay = jnp.exp(jnp.where(tri, seg, -jnp.inf))
                ms.append((cb * decay).astype(bf16))
                rhs.append(jnp.where(lo_half if q == 0 else jnp.logical_not(lo_half), dtx_b[:, tsl], 0.0))
            tiles.append(jnp.dot(jnp.concatenate(ms, axis=1), jnp.concatenate(rhs, axis=0).astype(bf16),
                                 preferred_element_type=f32))
        outs.append(jnp.concatenate(tiles, axis=1) + y_off)
        upd = lax.dot_general(bm, w_all[:, gsl], (((0,), (0,)), ((), ())), preferred_element_type=f32)
        state_ref[g] = s_old * e_edge[:, gsl] + upd
    y = jnp.concatenate(outs, axis=1)
    if not reverse:
        y = y + _expand_heads(dskip_ref[...], 0) * xs
    o_ref[0] = y.astype(bf16)


def _ssd_scan(xact, dt_pad, alog_pad, dtb_pad, dskip_pad):
    b, s, w = xact.shape
    L = _pick(s, 128)
    nc = s // L
    full = lambda shape: pl.BlockSpec(shape, lambda b_, c_: (0,) * len(shape))
    fwd = lambda width: pl.BlockSpec((1, L, width), lambda b_, c_: (b_, c_, 0))
    bwd = lambda width: pl.BlockSpec((1, L, width), lambda b_, c_: (b_, nc - 1 - c_, 0))
    spread, cols = _ssd_selectors(L)
    return pl.pallas_call(
        _ssd_body,
        grid=(b, nc),
        in_specs=[fwd(w), bwd(w), fwd(LANES), bwd(LANES), full((1, LANES)), full((1, LANES)), full((1, LANES)),
                  full(spread.shape), full(cols.shape)],
        out_specs=[fwd(SSD_WIDTH), bwd(SSD_WIDTH)],
        out_shape=[jax.ShapeDtypeStruct((b, s, SSD_WIDTH), bf16)] * 2,
        scratch_shapes=[pltpu.VMEM((2, SSD_GROUPS, SSD_STATE, GROUP_WIDTH), f32)],
        compiler_params=_params(("parallel", "arbitrary")),
        name="ssd_scan",
    )(xact, xact, dt_pad, dt_pad, alog_pad, dtb_pad, dskip_pad, spread, cols)


def _rms(x, g):
    ms = jnp.mean(x * x, axis=-1, keepdims=True)
    return x * lax.rsqrt(ms + EPS) * g


def _mix_out_body(x_ref, hf_ref, hb_ref, gate_ref, yf_ref, yb_ref, z_ref, gn_ref, wo_ref,
                  gf_ref, wrh_ref, wrl_ref, xo_ref, hn_ref, lg_ref):
    up = lambda ref: ref[...].astype(f32)
    y_lru = (up(hf_ref) + up(hb_ref)) * jax.nn.gelu(up(gate_ref))
    y = (up(yf_ref) + up(yb_ref)) * jax.nn.silu(up(z_ref))
    parts = []
    for g in range(SSD_GROUPS):
        yg = y[:, g * GROUP_WIDTH:(g + 1) * GROUP_WIDTH]
        ms = jnp.mean(yg * yg, axis=-1, keepdims=True)
        parts.append(yg * lax.rsqrt(ms + EPS))
    y_ssd = jnp.concatenate(parts, axis=1) * gn_ref[...]
    mix = jnp.concatenate([y_lru, y_ssd], axis=1).astype(bf16)
    xn = x_ref[...] + jnp.dot(mix, wo_ref[...], preferred_element_type=f32)
    xo_ref[...] = xn
    hn = _rms(xn, gf_ref[...])
    tm = hn.shape[0]
    for k in range(hn.shape[1] // LANES):
        hn_ref[pl.ds(k, tm, stride=SUBLANES), :] = hn[:, k * LANES:(k + 1) * LANES]
    h_hi = hn.astype(bf16)
    h_lo = (hn - h_hi.astype(f32)).astype(bf16)
    lg = jnp.dot(h_hi, wrh_ref[...], preferred_element_type=f32)
    lg = lg + jnp.dot(h_lo, wrh_ref[...], preferred_element_type=f32)
    lg = lg + jnp.dot(h_hi, wrl_ref[...], preferred_element_type=f32)
    lg_ref[...] = lg


def _mix_out(x2d, hf, hb, gate, yf, yb, z, ssd_norm, w_out_bf16, norm_ffn, wr_hi, wr_lo):
    t, d = x2d.shape
    tm = _pick(t, 256)
    row = lambda w: pl.BlockSpec((tm, w), lambda i: (i, 0))
    full = lambda shape: pl.BlockSpec(shape, lambda i: (0,) * len(shape))
    return pl.pallas_call(
        _mix_out_body,
        grid=(t // tm,),
        in_specs=[row(d)] + [row(LRU_WIDTH)] * 3 + [row(SSD_WIDTH)] * 3 + [
            full((1, SSD_WIDTH)), full((LRU_WIDTH + SSD_WIDTH, d)), full((1, d)),
            full((d, LANES)), full((d, LANES)),
        ],
        out_specs=[row(d), pl.BlockSpec((tm * d // LANES, LANES), lambda i: (i, 0)), row(LANES)],
        out_shape=[jax.ShapeDtypeStruct((t, d), f32), jax.ShapeDtypeStruct((t * d // LANES, LANES), f32),
                   jax.ShapeDtypeStruct((t, LANES), f32)],
        compiler_params=_params(("parallel",), VMEM_LIMIT),
        name="mix_out",
    )(x2d, hf, hb, gate, yf, yb, z, ssd_norm.reshape(1, -1), w_out_bf16, norm_ffn.reshape(1, d),
      wr_hi, wr_lo)


def _tile_prefix(tiles, upper_incl, upper_strict, lane):
    incs = [jnp.dot(t.astype(bf16), upper_incl, preferred_element_type=f32) for t in tiles]
    tot = jnp.zeros((N_EXPERTS, LANES), f32)
    for j, inc in enumerate(incs):
        tot = jnp.where(lane == j, inc[:, LANES - 1:LANES], tot)
    start = jnp.dot(tot.astype(bf16), upper_strict, preferred_element_type=f32)
    return incs, tot, start


def _route_body(lg_ref, idx_ref, gate_ref, lpos_ref, tab_ref, lm_scr, vt_scr, list_scr, tabv_scr, tabs_scr, sem,
                *, cap):
    s = lg_ref.shape[1]
    nt = s // LANES
    lt = lg_ref[0].T[:N_EXPERTS]
    mx = jnp.max(lt, axis=0, keepdims=True)
    ex = jnp.exp(lt - mx)
    aff = ex / jnp.sum(ex, axis=0, keepdims=True)
    key = pltpu.bitcast(aff, i32)

    def search(i, thr):
        cand = thr | (jnp.int32(1) << (30 - i))
        cnt = jnp.sum((key >= cand).astype(f32), axis=1, keepdims=True)
        return jnp.where(cnt >= float(cap), cand, thr)

    thr = lax.fori_loop(0, 31, search, jnp.zeros((N_EXPERTS, 1), i32))
    gt = key > thr
    eq = key == thr
    need = float(cap) - jnp.sum(gt.astype(f32), axis=1, keepdims=True)

    sub_i = lax.broadcasted_iota(i32, (LANES, LANES), 0)
    lane_i = lax.broadcasted_iota(i32, (LANES, LANES), 1)
    upper_incl = (sub_i <= lane_i).astype(bf16)
    upper_strict = (sub_i < lane_i).astype(bf16)
    lane_e = lax.broadcasted_iota(i32, (N_EXPERTS, LANES), 1)
    tiles = lambda a: [a[:, j * LANES:(j + 1) * LANES] for j in range(nt)]

    eq_t = tiles(eq.astype(f32))
    incs, _, start = _tile_prefix(eq_t, upper_incl, upper_strict, lane_e)
    sel_t = []
    for j, (gtj, eqj) in enumerate(zip(tiles(gt), eq_t)):
        excl = incs[j] - eqj + start[:, j:j + 1]
        sel_t.append(jnp.logical_or(gtj, jnp.logical_and(eqj > 0.5, excl < need)).astype(f32))

    incs, tot, start = _tile_prefix(sel_t, upper_incl, upper_strict, lane_e)
    start8 = jnp.floor(start * (1.0 / BF16_ROWS)) * float(BF16_ROWS)
    tab_ref[0, 0] = start8.astype(i32)
    tab_ref[0, 1] = tot.astype(i32)
    tab_ref[0, 2] = (start - start8 + tot).astype(i32)
    tabv_scr[...] = start.astype(i32)
    to_smem = pltpu.make_async_copy(tabv_scr, tabs_scr, sem)
    to_smem.start()

    kind = lax.broadcasted_iota(i32, (SUBLANES, LANES), 0)
    tok_lane = lax.broadcasted_iota(i32, (SUBLANES, LANES), 1).astype(f32)
    fill = jnp.full((LANES - N_EXPERTS, LANES), -1.0, f32)
    zrows = jnp.zeros((LANES - 3 * N_EXPERTS - SUBLANES, LANES), f32)
    for j, a in enumerate(tiles(aff)):
        lm = jnp.where(sel_t[j] > 0.5, incs[j] - sel_t[j], -1.0)
        lm_scr[j] = lm
        shifted = jnp.where(sel_t[j] > 0.5, lm + (start[:, j:j + 1] - start8[:, j:j + 1]), -1.0)
        lpos_ref[0, pl.ds(j * LANES, LANES), :] = jnp.concatenate([shifted, fill], axis=0).T
        a_hi = a.astype(bf16).astype(f32)
        a_mid = (a - a_hi).astype(bf16).astype(f32)
        a_lo = (a - a_hi - a_mid).astype(bf16).astype(f32)
        tok = jnp.where(kind == 0, float(j), jnp.where(kind == 1, tok_lane, 0.0))
        vt_scr[j] = jnp.concatenate([a_hi, a_mid, a_lo, tok, zrows], axis=0).T.astype(bf16)

    to_smem.wait()
    rank = sub_i.astype(f32)

    def compact(j, carry):
        lm = lm_scr[j]
        vt = vt_scr[j]
        for e in range(N_EXPERTS):
            onehot = (lm[e:e + 1] == rank).astype(bf16)
            packed = jnp.dot(onehot, vt, preferred_element_type=f32)
            list_scr[e, pl.ds(tabs_scr[e, j], LANES), :] = packed
        return carry

    lax.fori_loop(0, nt, compact, 0)

    lane_c = lax.broadcasted_iota(i32, (cap, LANES), 1)
    idx_c = jnp.zeros((cap, LANES), f32)
    gate_c = jnp.zeros((cap, LANES), f32)
    for e in range(N_EXPERTS):
        rows = list_scr[e, 0:cap, :]
        g = rows[:, e:e + 1] + rows[:, N_EXPERTS + e:N_EXPERTS + e + 1] + rows[:, 2 * N_EXPERTS + e:2 * N_EXPERTS + e + 1]
        t = rows[:, 3 * N_EXPERTS:3 * N_EXPERTS + 1] * float(LANES) + rows[:, 3 * N_EXPERTS + 1:3 * N_EXPERTS + 2]
        idx_c = jnp.where(lane_c == e, t, idx_c)
        gate_c = jnp.where(lane_c == e, g, gate_c)
    idx_ref[0] = idx_c.T[:N_EXPERTS].astype(i32)
    gate_ref[0] = gate_c.T[:N_EXPERTS]


def _route(logits, cap):
    b, s, _ = logits.shape
    nt = s // LANES
    assert nt <= LANES
    return pl.pallas_call(
        functools.partial(_route_body, cap=cap),
        grid=(b,),
        in_specs=[pl.BlockSpec((1, s, LANES), lambda i: (i, 0, 0))],
        out_specs=[pl.BlockSpec((1, N_EXPERTS, cap), lambda i: (i, 0, 0)),
                   pl.BlockSpec((1, N_EXPERTS, cap), lambda i: (i, 0, 0)),
                   pl.BlockSpec((1, s, LANES), lambda i: (i, 0, 0)),
                   pl.BlockSpec((1, 3, N_EXPERTS, LANES), lambda i: (i, 0, 0, 0))],
        out_shape=[jax.ShapeDtypeStruct((b, N_EXPERTS, cap), i32),
                   jax.ShapeDtypeStruct((b, N_EXPERTS, cap), f32),
                   jax.ShapeDtypeStruct((b, s, LANES), f32),
                   jax.ShapeDtypeStruct((b, 3, N_EXPERTS, LANES), i32)],
        scratch_shapes=[pltpu.VMEM((nt, N_EXPERTS, LANES), f32),
                        pltpu.VMEM((nt, LANES, LANES), bf16),
                        pltpu.VMEM((N_EXPERTS, cap + LANES, LANES), f32),
                        pltpu.VMEM((N_EXPERTS, LANES), i32),
                        pltpu.SMEM((N_EXPERTS, LANES), i32),
                        pltpu.SemaphoreType.DMA],
        compiler_params=_params(("parallel",), VMEM_LIMIT),
        name="route",
    )(logits)


def _token_copy(src_hbm, tok, dst, r, sem):
    src = src_hbm.at[pl.ds(pl.multiple_of(tok * SUBLANES, SUBLANES), SUBLANES), :]
    return pltpu.make_async_copy(src, dst.at[pl.ds(pl.multiple_of(r * SUBLANES, SUBLANES), SUBLANES), :], sem)


def _moe_body(idx_ref, idxn_ref, gate_ref, wg_ref, wu_ref, wd_ref, hn_hbm, o_ref, xg, sem, *, cap, seq, nb):
    n = pl.program_id(0) * nb + pl.program_id(1)
    total = N_EXPERTS * nb
    slot = lax.rem(n, 2)
    other = 1 - slot

    @pl.when(n == 0)
    def _():
        def start(r, carry):
            _token_copy(hn_hbm, idx_ref[0, 0, 0, r], xg.at[slot], r, sem.at[slot]).start()
            return carry

        lax.fori_loop(0, cap, start, 0, unroll=8)

    def wait_all(buf):
        def wait(r, carry):
            _token_copy(hn_hbm, 0, xg.at[buf], r, sem.at[buf]).wait()
            return carry

        lax.fori_loop(0, cap, wait, 0, unroll=8)

    wait_all(slot)

    base_next = lax.rem(jnp.minimum(n + 1, total - 1), nb) * seq
    tm = min(cap, 256)
    ntile = wg_ref.shape[1] // LANES
    for m in range(cap // tm):
        rows = slice(m * tm, (m + 1) * tm)
        for r in range(m * tm, (m + 1) * tm):
            _token_copy(hn_hbm, base_next + idxn_ref[0, 0, 0, r], xg.at[other], r, sem.at[other]).start()
        xm = jnp.concatenate(
            [xg[slot, pl.ds(m * tm * SUBLANES + k, tm, stride=SUBLANES), :] for k in range(ntile)],
            axis=1).astype(bf16)
        hg = jnp.dot(xm, wg_ref[0], preferred_element_type=f32)
        hu = jnp.dot(xm, wu_ref[0], preferred_element_type=f32)
        hid = (jax.nn.silu(hg) * hu).astype(bf16)
        y = jnp.dot(hid, wd_ref[0], preferred_element_type=f32) * gate_ref[0, 0, rows, :]
        o_ref[0, 0, rows, :] = y.astype(bf16)
    o_ref[0, 0, cap:cap + WINDOW, :] = jnp.zeros((WINDOW, o_ref.shape[-1]), bf16)

    @pl.when(n == total - 1)
    def _():
        wait_all(other)


def _moe_ffn(hn_tiles, idx, gates, wg, wu, wd, layer):
    b, _, cap = idx.shape
    d = wg.shape[2]
    s = hn_tiles.shape[0] * LANES // d // b
    idx4 = idx.reshape(b, N_EXPERTS, 1, cap)
    gates4 = gates.reshape(b, N_EXPERTS, cap, 1)
    def next_step(e, i):
        n1 = jnp.minimum(e * b + i + 1, N_EXPERTS * b - 1)
        return (lax.rem(n1, b), n1 // b, 0, 0)

    return pl.pallas_call(
        functools.partial(_moe_body, cap=cap, seq=s, nb=b),
        grid=(N_EXPERTS, b),
        in_specs=[
            pl.BlockSpec((1, 1, 1, cap), lambda e, i: (i, e, 0, 0), memory_space=pltpu.SMEM),
            pl.BlockSpec((1, 1, 1, cap), next_step, memory_space=pltpu.SMEM),
            pl.BlockSpec((1, 1, cap, 1), lambda e, i: (i, e, 0, 0)),
            pl.BlockSpec((None, 1, d, D_FF), lambda e, i: (layer, e, 0, 0)),
            pl.BlockSpec((None, 1, d, D_FF), lambda e, i: (layer, e, 0, 0)),
            pl.BlockSpec((None, 1, D_FF, d), lambda e, i: (layer, e, 0, 0)),
            pl.BlockSpec(memory_space=pl.ANY),
        ],
        out_specs=pl.BlockSpec((1, 1, cap + WINDOW, d), lambda e, i: (i, e, 0, 0)),
        out_shape=jax.ShapeDtypeStruct((b, N_EXPERTS, cap + WINDOW, d), bf16),
        scratch_shapes=[pltpu.VMEM((2, cap * d // LANES, LANES), f32), pltpu.SemaphoreType.DMA((2,))],
        compiler_params=_params(("arbitrary", "arbitrary"), VMEM_LIMIT),
        name="moe_ffn",
    )(idx4, idx4, gates4, wg, wu, wd, hn_tiles)


SLOT = 64


def _short_copy(yg_hbm, b, e, start, packed, slot, sem):
    src = yg_hbm.at[b, e, pl.ds(pl.multiple_of(start, BF16_ROWS), SLOT), :]
    return pltpu.make_async_copy(src, packed.at[slot, pl.ds(e * SLOT, SLOT), :], sem.at[0, slot])


def _full_copy(yg_hbm, b, e, start, spill, slot, sem):
    src = yg_hbm.at[b, e, pl.ds(pl.multiple_of(start, BF16_ROWS), WINDOW), :]
    return pltpu.make_async_copy(src, spill.at[slot, e], sem.at[1, slot])


def _combine_body(tab_ref, tabn_ref, lpos_ref, x_ref, g_ref, yg_hbm, o_ref, packed, spill, sem,
                  *, nb, nt, final_norm):
    b = pl.program_id(0)
    j = pl.program_id(1)
    n = b * nt + j
    slot = lax.rem(n, 2)

    def fetch(tab, bb, jj, buf):
        for e in range(N_EXPERTS):
            _short_copy(yg_hbm, bb, e, tab[0, 0, e, jj], packed, buf, sem).start()

            @pl.when(tab[0, 2, e, jj] > SLOT)
            def _(e=e):
                _full_copy(yg_hbm, bb, e, tab[0, 0, e, jj], spill, buf, sem).start()

    @pl.when(n == 0)
    def _():
        packed[...] = jnp.zeros_like(packed)
        spill[...] = jnp.zeros_like(spill)
        fetch(tab_ref, b, j, slot)

    @pl.when(n + 1 < nb * nt)
    def _():
        wrap = j + 1 == nt
        fetch(tabn_ref, jnp.where(wrap, b + 1, b), jnp.where(wrap, 0, j + 1), 1 - slot)

    lm = lpos_ref[0]
    lane = lax.broadcasted_iota(i32, (LANES, LANES), 1).astype(f32)
    per_tile = LANES // SLOT
    tiles = []
    for lt in range(N_EXPERTS // per_tile):
        hit = None
        for i in range(per_tile):
            e = lt * per_tile + i
            col = jnp.where(tab_ref[0, 2, e, j] <= SLOT, lm[:, e:e + 1], -1.0)
            col = jnp.broadcast_to(jnp.where(col >= 0.0, col + float(i * SLOT), -1.0), (LANES, LANES))
            h = col == lane
            hit = h if hit is None else jnp.logical_or(hit, h)
        tiles.append(hit.astype(bf16))
    place = jnp.concatenate(tiles, axis=1)

    for e in range(N_EXPERTS):
        _short_copy(yg_hbm, b, e, 0, packed, slot, sem).wait()
    o_ref[...] = x_ref[...] + jnp.dot(place, packed[slot], preferred_element_type=f32)

    row = lax.broadcasted_iota(i32, (LANES, WINDOW), 1).astype(f32)
    for e in range(N_EXPERTS):
        @pl.when(tab_ref[0, 2, e, j] > SLOT)
        def _(e=e):
            _full_copy(yg_hbm, b, e, 0, spill, slot, sem).wait()
            own = (lm[:, e:e + 1] == row).astype(bf16)
            o_ref[...] += jnp.dot(own, spill[slot, e], preferred_element_type=f32)

    if final_norm:
        o_ref[...] = _rms(o_ref[...], g_ref[...])


def _combine(x2d, lpos, tab, yg, nb, final_g=None):
    t, d = x2d.shape
    g = jnp.ones((1, d), f32) if final_g is None else final_g.reshape(1, d)
    nt = t // nb // LANES

    def next_batch(i, j):
        return (jnp.minimum(i + (j + 1) // nt, nb - 1), 0, 0, 0)

    return pl.pallas_call(
        functools.partial(_combine_body, nb=nb, nt=nt, final_norm=final_g is not None),
        grid=(nb, nt),
        in_specs=[
            pl.BlockSpec((1, 3, N_EXPERTS, LANES), lambda i, j: (i, 0, 0, 0), memory_space=pltpu.SMEM),
            pl.BlockSpec((1, 3, N_EXPERTS, LANES), next_batch, memory_space=pltpu.SMEM),
            pl.BlockSpec((1, LANES, LANES), lambda i, j: (i, j, 0)),
            pl.BlockSpec((LANES, d), lambda i, j: (i * nt + j, 0)),
            pl.BlockSpec((1, d), lambda i, j: (0, 0)),
            pl.BlockSpec(memory_space=pl.ANY),
        ],
        out_specs=pl.BlockSpec((LANES, d), lambda i, j: (i * nt + j, 0)),
        out_shape=jax.ShapeDtypeStruct((t, d), f32),
        scratch_shapes=[pltpu.VMEM((2, N_EXPERTS * SLOT, d), bf16), pltpu.VMEM((2, N_EXPERTS, WINDOW, d), bf16),
                        pltpu.SemaphoreType.DMA((2, 2))],
        compiler_params=_params(("arbitrary", "arbitrary")),
        name="moe_combine",
    )(tab, tab, lpos, x2d, g, yg)


def _pad_lanes(v):
    return jnp.zeros((1, LANES), f32).at[0, :v.shape[0]].set(v.astype(f32))


def _mixer(x3, norm_mix, w_in, lru_conv_w, lru_conv_b, lru_wa, lru_ba, lru_wi, lru_bi, lru_lambda,
           ssd_conv_w, ssd_conv_b, ssd_a_log, ssd_dt_bias, ssd_d):
    b, s, d = x3.shape
    x2d = x3.reshape(b * s, d)
    pad = jnp.zeros((d, LANES - 2 * SSD_HEADS), bf16)
    w_bf = jnp.concatenate([w_in.astype(bf16), pad], axis=1)
    xc, gate, z, xact, dt = _in_proj(x2d, norm_mix, w_bf, lru_conv_w, lru_conv_b, ssd_conv_w, ssd_conv_b, s)
    xc = xc.reshape(b, s, LRU_WIDTH)
    xact = xact.reshape(b, s, SSD_CONV_CH)
    dt = dt.reshape(b, s, LANES)

    w_gate = jnp.concatenate([lru_wa, lru_wi], axis=-1).astype(bf16)
    hs = _lru_scan(xc, w_gate, lru_ba, lru_bi, lru_lambda)
    alog_pad = _pad_lanes(ssd_a_log.reshape(-1))
    dtb_pad = _pad_lanes(ssd_dt_bias.reshape(-1))
    dskip_pad = _pad_lanes(ssd_d)
    ys = _ssd_scan(xact, dt, alog_pad, dtb_pad, dskip_pad)
    t = b * s
    return (hs[0].reshape(t, -1), hs[1].reshape(t, -1), gate, ys[0].reshape(t, -1), ys[1].reshape(t, -1), z)


def _layer(x3, norm_mix, w_in, lru_conv_w, lru_conv_b, lru_wa, lru_ba, lru_wi, lru_bi, lru_lambda,
           ssd_conv_w, ssd_conv_b, ssd_a_log, ssd_dt_bias, ssd_d, ssd_norm, w_out, norm_ffn, w_router,
           expert_w, layer, final_g):
    b, s, d = x3.shape
    cap = max(1, CAPACITY_FACTOR * s // N_EXPERTS)
    assert s % LANES == 0 and cap % LANES == 0, "sequence length must give 128-aligned expert capacity"
    hf, hb, gate, yf, yb, z = _mixer(x3, norm_mix, w_in, lru_conv_w, lru_conv_b, lru_wa, lru_ba, lru_wi,
                                     lru_bi, lru_lambda, ssd_conv_w, ssd_conv_b, ssd_a_log, ssd_dt_bias, ssd_d)
    wr = jnp.zeros((d, LANES), f32).at[:, :N_EXPERTS].set(w_router)
    wr_hi = wr.astype(bf16)
    wr_lo = (wr - wr_hi.astype(f32)).astype(bf16)
    xn, hn, logits = _mix_out(x3.reshape(b * s, d), hf, hb, gate, yf, yb, z, ssd_norm, w_out.astype(bf16),
                              norm_ffn, wr_hi, wr_lo)
    idx, gates, lpos, tab = _route(logits.reshape(b, s, LANES), cap)
    yg = _moe_ffn(hn, idx, gates, *expert_w, layer)
    return _combine(xn, lpos, tab, yg, b, final_g).reshape(b, s, d)


def kernel(x, norm_mix, w_in, lru_conv_w, lru_conv_b, lru_wa, lru_ba, lru_wi, lru_bi, lru_lambda, ssd_conv_w, ssd_conv_b, ssd_a_log, ssd_dt_bias, ssd_d, ssd_norm, w_out, norm_ffn, w_router, w_gate, w_up, w_down, norm_final):
    depth = norm_mix.shape[0]
    expert_w = (w_gate.astype(bf16), w_up.astype(bf16), w_down.astype(bf16))
    for l in range(depth):
        x = _layer(x, norm_mix[l], w_in[l], lru_conv_w[l], lru_conv_b[l], lru_wa[l], lru_ba[l], lru_wi[l],
                   lru_bi[l], lru_lambda[l], ssd_conv_w[l], ssd_conv_b[l], ssd_a_log[l], ssd_dt_bias[l],
                   ssd_d[l], ssd_norm[l], w_out[l], norm_ffn[l], w_router[l], expert_w, l,
                   norm_final if l == depth - 1 else None)
    return x
```

```python
import functools

import jax
import jax.numpy as jnp
from jax import lax
from jax.experimental import pallas as pl
from jax.experimental.pallas import tpu as pltpu

f32 = jnp.float32
bf16 = jnp.bfloat16
i32 = jnp.int32

D_MODEL = 1024
EPS = 1e-6
CONV_WIDTH = 4
LRU_WIDTH = 1024
LRU_HEADS = 8
LRU_BLOCK = 128
LRU_C = 8.0
SSD_WIDTH = 1024
SSD_HEADDIM = 64
SSD_HEADS = 16
SSD_GROUPS = 4
SSD_STATE = 128
SSD_GN = SSD_GROUPS * SSD_STATE
SSD_CONV_CH = SSD_WIDTH + 2 * SSD_GN
GROUP_WIDTH = SSD_WIDTH // SSD_GROUPS
N_EXPERTS = 16
CAPACITY_FACTOR = 2
D_FF = 2048

LANES = 128
SUBLANES = 8
HALO = SUBLANES
BF16_ROWS = 2 * SUBLANES
WINDOW = LANES + BF16_ROWS
VMEM_LIMIT = 56 * 1024 * 1024


def _pick(n, target):
    if n <= target:
        return n
    t = target
    while t >= LANES:
        if n % t == 0:
            return t
        t -= LANES
    return n


def _params(sem, vmem=None):
    return pltpu.CompilerParams(dimension_semantics=sem, vmem_limit_bytes=vmem)


IN_SEGMENTS = (LRU_WIDTH, LRU_WIDTH, SSD_WIDTH, SSD_CONV_CH, LANES)


def _conv_centred(pe, first, last, cw, cb):
    tm = pe.shape[0] - 2 * HALO
    before = jnp.where(first, 0.0, pe[:HALO])
    after = jnp.where(last, 0.0, pe[HALO + tm:])
    c = pe[HALO:HALO + tm]
    r8 = lax.broadcasted_iota(i32, (SUBLANES, pe.shape[1]), 0)
    back1 = pltpu.roll(c, 1, 0)
    fwd1 = pltpu.roll(c, tm - 1, 0)
    fwd2 = pltpu.roll(c, tm - 2, 0)
    back1 = jnp.concatenate([jnp.where(r8 == 0, before[HALO - 1:HALO], back1[:HALO]), back1[HALO:]], axis=0)
    fwd1 = jnp.concatenate([fwd1[:tm - HALO], jnp.where(r8 == HALO - 1, after[0:1], fwd1[tm - HALO:])], axis=0)
    tail2 = jnp.where(r8 == HALO - 2, after[0:1], jnp.where(r8 == HALO - 1, after[1:2], fwd2[tm - HALO:]))
    fwd2 = jnp.concatenate([fwd2[:tm - HALO], tail2], axis=0)
    return cb + back1 * cw[0:1] + c * cw[1:2] + fwd1 * cw[2:3] + fwd2 * cw[3:4]


def _in_proj_body(prev_ref, x_ref, next_ref, g_ref, w_ref, lcw_ref, lcb_ref, scw_ref, scb_ref,
                  xc_ref, gate_ref, z_ref, xa_ref, dt_ref, *, tiles_per_seq):
    pos = lax.rem(pl.program_id(0), tiles_per_seq)
    first = pos == 0
    last = pos == tiles_per_seq - 1
    tm = x_ref.shape[0]
    xe = jnp.concatenate([prev_ref[...], x_ref[...], next_ref[...]], axis=0)
    ms = jnp.mean(xe * xe, axis=-1, keepdims=True)
    hn = (xe * lax.rsqrt(ms + EPS) * g_ref[...]).astype(bf16)
    hc = hn[HALO:HALO + tm]
    o0, o1, o2, o3 = LRU_WIDTH, 2 * LRU_WIDTH, 2 * LRU_WIDTH + SSD_WIDTH, 2 * LRU_WIDTH + SSD_WIDTH + SSD_CONV_CH
    pe = jnp.dot(hn, w_ref[:, :o0], preferred_element_type=f32)
    xc_ref[...] = _conv_centred(pe, first, last, lcw_ref[...], lcb_ref[...]).astype(bf16)
    gate_ref[...] = jnp.dot(hc, w_ref[:, o0:o1], preferred_element_type=f32).astype(bf16)
    z_ref[...] = jnp.dot(hc, w_ref[:, o1:o2], preferred_element_type=f32).astype(bf16)
    pe = jnp.dot(hn, w_ref[:, o2:o3], preferred_element_type=f32)
    xa_ref[...] = jax.nn.silu(_conv_centred(pe, first, last, scw_ref[...], scb_ref[...])).astype(bf16)
    dt_ref[...] = jnp.dot(hc, w_ref[:, o3:], preferred_element_type=f32)


def _in_proj(x2d, g, w_bf16, lru_cw, lru_cb, ssd_cw, ssd_cb, seq):
    t, d = x2d.shape
    n = w_bf16.shape[1]
    tm = _pick(seq, 512)
    per = tm // HALO
    last_blk = t // HALO - 1
    full = lambda shape: pl.BlockSpec(shape, lambda i: (0,) * len(shape))
    return pl.pallas_call(
        functools.partial(_in_proj_body, tiles_per_seq=seq // tm),
        grid=(t // tm,),
        in_specs=[
            pl.BlockSpec((HALO, d), lambda i: (jnp.maximum(i * per - 1, 0), 0)),
            pl.BlockSpec((tm, d), lambda i: (i, 0)),
            pl.BlockSpec((HALO, d), lambda i: (jnp.minimum((i + 1) * per, last_blk), 0)),
            full((1, d)),
            pl.BlockSpec((d, n), lambda i: (0, 0), pipeline_mode=pl.Buffered(1)),
            full((CONV_WIDTH, LRU_WIDTH)), full((1, LRU_WIDTH)),
            full((CONV_WIDTH, SSD_CONV_CH)), full((1, SSD_CONV_CH)),
        ],
        out_specs=[pl.BlockSpec((tm, w), lambda i: (i, 0)) for w in IN_SEGMENTS],
        out_shape=[jax.ShapeDtypeStruct((t, w), dt) for w, dt in zip(IN_SEGMENTS, (bf16, bf16, bf16, bf16, f32))],
        compiler_params=_params(("parallel",), VMEM_LIMIT),
        name="in_proj",
    )(x2d, x2d, x2d, g.reshape(1, d), w_bf16, lru_cw, lru_cb.reshape(1, -1), ssd_cw, ssd_cb.reshape(1, -1))


def _lru_gates(x_ref, perm_ref, w_ref, ba_ref, bi_ref, lam_ref, a_scr, u_scr, d):
    xc = jnp.dot(perm_ref[0], x_ref[0], preferred_element_type=f32).astype(bf16)
    sp = jax.nn.softplus(-lam_ref[d:d + 1, :])
    for h in range(LRU_HEADS):
        sl = slice(h * LRU_BLOCK, (h + 1) * LRU_BLOCK)
        pre = jnp.dot(xc[:, sl], w_ref[d, h], preferred_element_type=f32)
        xh = xc[:, sl].astype(f32)
        r = jax.nn.sigmoid(pre[:, :LRU_BLOCK] + ba_ref[d:d + 1, sl])
        gi = jax.nn.sigmoid(pre[:, LRU_BLOCK:] + bi_ref[d:d + 1, sl])
        log_a = (-LRU_C) * r * sp[:, sl]
        a = jnp.exp(log_a)
        u = jnp.sqrt(1.0 - a * a) * (gi * xh)
        a_scr[d, h] = a
        u_scr[d, h] = u


def _lru_direction(d, perm_ref, o_ref, carry_ref, a_scr, u_scr):
    tc = o_ref.shape[1]
    sub = tc // SUBLANES
    heads = range(LRU_HEADS)

    def block(j):
        jj = (sub - 1 - j) if d else j
        return pl.ds(pl.multiple_of(jj * SUBLANES, SUBLANES), SUBLANES)

    def local(j, state):
        rows = block(j)
        out = []
        for h in heads:
            a = a_scr[d, h, rows, :]
            out += [a * state[2 * h] + u_scr[d, h, rows, :], a * state[2 * h + 1]]
        return tuple(out)

    init = []
    for h in heads:
        init += [jnp.zeros((SUBLANES, LRU_BLOCK), f32), jnp.ones((SUBLANES, LRU_BLOCK), f32)]
    ends = lax.fori_loop(0, sub, local, tuple(init), unroll=True)

    sub_i = lax.broadcasted_iota(i32, (SUBLANES, LRU_BLOCK), 0)
    carries = []
    for h in heads:
        sl = slice(h * LRU_BLOCK, (h + 1) * LRU_BLOCK)
        c = carry_ref[d:d + 1, sl]
        cin = jnp.zeros((SUBLANES, LRU_BLOCK), f32)
        for s in (reversed(range(SUBLANES)) if d else range(SUBLANES)):
            cin = jnp.where(sub_i == s, c, cin)
            c = ends[2 * h][s:s + 1] + ends[2 * h + 1][s:s + 1] * c
        carry_ref[d:d + 1, sl] = c
        carries.append(cin)

    def final(j, state):
        rows = block(j)
        out = []
        for h in heads:
            hcur = a_scr[d, h, rows, :] * state[h] + u_scr[d, h, rows, :]
            u_scr[d, h, rows, :] = hcur
            out.append(hcur)
        return tuple(out)

    lax.fori_loop(0, sub, final, tuple(carries), unroll=True)

    hp = jnp.concatenate([u_scr[d, h].astype(bf16) for h in heads], axis=1)
    o_ref[0] = jnp.dot(perm_ref[1], hp, preferred_element_type=f32).astype(bf16)


def _lru_body(xf_ref, xb_ref, perm_ref, w_ref, ba_ref, bi_ref, lam_ref, of_ref, ob_ref, carry_ref, a_scr, u_scr):
    @pl.when(pl.program_id(1) == 0)
    def _():
        carry_ref[...] = jnp.zeros_like(carry_ref)

    _lru_gates(xf_ref, perm_ref, w_ref, ba_ref, bi_ref, lam_ref, a_scr, u_scr, 0)
    _lru_gates(xb_ref, perm_ref, w_ref, ba_ref, bi_ref, lam_ref, a_scr, u_scr, 1)
    _lru_direction(0, perm_ref, of_ref, carry_ref, a_scr, u_scr)
    _lru_direction(1, perm_ref, ob_ref, carry_ref, a_scr, u_scr)


def _lru_scan(xc, w_gate, ba, bi, lam):
    b, s, w = xc.shape
    tc = _pick(s, 256)
    nc = s // tc
    full = lambda shape: pl.BlockSpec(shape, lambda b_, c_: (0,) * len(shape))
    fwd = pl.BlockSpec((1, tc, w), lambda b_, c_: (b_, c_, 0))
    bwd = pl.BlockSpec((1, tc, w), lambda b_, c_: (b_, nc - 1 - c_, 0))
    p = jnp.arange(tc)
    to_strided = ((p % SUBLANES) * (tc // SUBLANES) + p // SUBLANES)[:, None] == jnp.arange(tc)[None, :]
    perm = jnp.stack([to_strided, to_strided.T]).astype(bf16)
    return pl.pallas_call(
        _lru_body,
        grid=(b, nc),
        in_specs=[fwd, bwd, full((2, tc, tc)), full((2, LRU_HEADS, LRU_BLOCK, 2 * LRU_BLOCK)),
                  full((2, w)), full((2, w)), full((2, w))],
        out_specs=[fwd, bwd],
        out_shape=[jax.ShapeDtypeStruct((b, s, w), bf16)] * 2,
        scratch_shapes=[pltpu.VMEM((2, w), f32), pltpu.VMEM((2, LRU_HEADS, tc, LRU_BLOCK), f32),
                        pltpu.VMEM((2, LRU_HEADS, tc, LRU_BLOCK), f32)],
        compiler_params=_params(("parallel", "arbitrary")),
        name="lru_scan",
    )(xc, xc, perm, w_gate, ba, bi, lam)


def _expand_heads(arr, base):
    rows = arr.shape[0]
    lane = lax.broadcasted_iota(i32, (rows, LANES), 1)
    tiles = []
    for k in range(SSD_HEADS // 2):
        c0 = arr[:, base + 2 * k:base + 2 * k + 1]
        c1 = arr[:, base + 2 * k + 1:base + 2 * k + 2]
        tiles.append(jnp.where(lane < SSD_HEADDIM, c0, c1))
    return jnp.concatenate(tiles, axis=1)


def _cumsum_rows(x, reverse):
    n = x.shape[0]
    r = lax.broadcasted_iota(i32, x.shape, 0)
    k = 1
    while k < n:
        if reverse:
            x = x + jnp.where(r < n - k, pltpu.roll(x, n - k, 0), 0.0)
        else:
            x = x + jnp.where(r >= k, pltpu.roll(x, k, 0), 0.0)
        k *= 2
    return x


def _ssd_body(xf_ref, xb_ref, dtf_ref, dtb_in_ref, alog_ref, dtb_ref, dskip_ref, sele_ref, selc_ref,
              of_ref, ob_ref, state_ref):
    @pl.when(pl.program_id(1) == 0)
    def _():
        state_ref[...] = jnp.zeros_like(state_ref)

    _ssd_dir(xf_ref, dtf_ref, alog_ref, dtb_ref, dskip_ref, sele_ref, selc_ref, of_ref, state_ref.at[0], False)
    _ssd_dir(xb_ref, dtb_in_ref, alog_ref, dtb_ref, dskip_ref, sele_ref, selc_ref, ob_ref, state_ref.at[1], True)


def _split_bf16(x, parts):
    out = []
    for _ in range(parts):
        p = x.astype(bf16)
        out.append(p)
        x = x - p.astype(f32)
    return jnp.concatenate(out, axis=1)


def _ssd_selectors(chunk):
    lane = jnp.arange(LANES)[:, None]
    spread, cols = [], []
    for d in range(2):
        head = lane - d * SSD_HEADS
        s = (head == jnp.arange(SSD_WIDTH)[None, :] // SSD_HEADDIM).astype(bf16)
        c = (head == jnp.arange(SSD_HEADS * chunk)[None, :] // chunk).astype(bf16)
        spread.append(jnp.concatenate([s, s], axis=0))
        cols.append(jnp.concatenate([c, c], axis=0))
    return jnp.stack(spread), jnp.stack(cols)


def _ssd_dir(x_ref, dt_ref, alog_ref, dtb_ref, dskip_ref, sele_ref, selc_ref, o_ref, state_ref, reverse):
    L = x_ref.shape[1]
    base = SSD_HEADS if reverse else 0
    xact = x_ref[0]
    xs = xact[:, :SSD_WIDTH].astype(f32)

    dt = jax.nn.softplus(dt_ref[0] + dtb_ref[...])
    d_a = dt * (-jnp.exp(alog_ref[...]))
    cum = _cumsum_rows(d_a, reverse)
    edge = cum[0:1] if reverse else cum[L - 1:L]
    cum_t = cum.T

    d = 1 if reverse else 0
    stack = jnp.concatenate([_split_bf16(dt, 2), _split_bf16(jnp.exp(cum), 2), _split_bf16(jnp.exp(edge - cum), 2)],
                            axis=0)
    spread = jnp.dot(stack, sele_ref[d], preferred_element_type=f32)
    dtx = spread[:L] * xs
    e_cum = spread[L:2 * L]
    e_end = spread[2 * L:]
    cum_cols = jnp.dot(_split_bf16(cum, 2), selc_ref[d], preferred_element_type=f32)
    e_edge = _expand_heads(jnp.exp(edge), base)
    w_all = (e_end * dtx).astype(bf16)
    dtx_b = dtx.astype(bf16)

    li = lax.broadcasted_iota(i32, (L, L), 0)
    si = lax.broadcasted_iota(i32, (L, L), 1)
    tri = (si >= li) if reverse else (li >= si)
    lane = lax.broadcasted_iota(i32, (L, LANES), 1)
    lo_half = lane < SSD_HEADDIM

    outs = []
    for g in range(SSD_GROUPS):
        bsl = slice(SSD_WIDTH + g * SSD_STATE, SSD_WIDTH + (g + 1) * SSD_STATE)
        csl = slice(SSD_WIDTH + SSD_GN + g * SSD_STATE, SSD_WIDTH + SSD_GN + (g + 1) * SSD_STATE)
        gsl = slice(g * GROUP_WIDTH, (g + 1) * GROUP_WIDTH)
        bm = xact[:, bsl]
        cm = xact[:, csl]
        cb = lax.dot_general(cm, bm, (((1,), (1,)), ((), ())), preferred_element_type=f32)
        s_old = state_ref[g]
        y_off = jnp.dot(cm, s_old.astype(bf16), preferred_element_type=f32) * e_cum[:, gsl]
        tiles = []
        for p in range(2):
            tsl = slice(g * GROUP_WIDTH + p * LANES, g * GROUP_WIDTH + (p + 1) * LANES)
            ms, rhs = [], []
            for q in range(2):
                h = g * 4 + 2 * p + q
                j = base + h
                seg = cum_cols[:, h * L:(h + 1) * L] - cum_t[j:j + 1, :]
                decay = jnp.exp(jnp.where(tri, seg, -jnp.inf))
                ms.append((cb * decay).astype(bf16))
                rhs.append(jnp.where(lo_half if q == 0 else jnp.logical_not(lo_half), dtx_b[:, tsl], 0.0))
            tiles.append(jnp.dot(jnp.concatenate(ms, axis=1), jnp.concatenate(rhs, axis=0).astype(bf16),
                                 preferred_element_type=f32))
        outs.append(jnp.concatenate(tiles, axis=1) + y_off)
        upd = lax.dot_general(bm, w_all[:, gsl], (((0,), (0,)), ((), ())), preferred_element_type=f32)
        state_ref[g] = s_old * e_edge[:, gsl] + upd
    y = jnp.concatenate(outs, axis=1)
    if not reverse:
        y = y + _expand_heads(dskip_ref[...], 0) * xs
    o_ref[0] = y.astype(bf16)


def _ssd_scan(xact, dt_pad, alog_pad, dtb_pad, dskip_pad):
    b, s, w = xact.shape
    L = _pick(s, 128)
    nc = s // L
    full = lambda shape: pl.BlockSpec(shape, lambda b_, c_: (0,) * len(shape))
    fwd = lambda width: pl.BlockSpec((1, L, width), lambda b_, c_: (b_, c_, 0))
    bwd = lambda width: pl.BlockSpec((1, L, width), lambda b_, c_: (b_, nc - 1 - c_, 0))
    spread, cols = _ssd_selectors(L)
    return pl.pallas_call(
        _ssd_body,
        grid=(b, nc),
        in_specs=[fwd(w), bwd(w), fwd(LANES), bwd(LANES), full((1, LANES)), full((1, LANES)), full((1, LANES)),
                  full(spread.shape), full(cols.shape)],
        out_specs=[fwd(SSD_WIDTH), bwd(SSD_WIDTH)],
        out_shape=[jax.ShapeDtypeStruct((b, s, SSD_WIDTH), bf16)] * 2,
        scratch_shapes=[pltpu.VMEM((2, SSD_GROUPS, SSD_STATE, GROUP_WIDTH), f32)],
        compiler_params=_params(("parallel", "arbitrary")),
        name="ssd_scan",
    )(xact, xact, dt_pad, dt_pad, alog_pad, dtb_pad, dskip_pad, spread, cols)


def _rms(x, g):
    ms = jnp.mean(x * x, axis=-1, keepdims=True)
    return x * lax.rsqrt(ms + EPS) * g


def _mix_out_body(x_ref, hf_ref, hb_ref, gate_ref, yf_ref, yb_ref, z_ref, gn_ref, wo_ref,
                  gf_ref, wrh_ref, wrl_ref, xo_ref, hn_ref, lg_ref):
    up = lambda ref: ref[...].astype(f32)
    y_lru = (up(hf_ref) + up(hb_ref)) * jax.nn.gelu(up(gate_ref))
    y = (up(yf_ref) + up(yb_ref)) * jax.nn.silu(up(z_ref))
    parts = []
    for g in range(SSD_GROUPS):
        yg = y[:, g * GROUP_WIDTH:(g + 1) * GROUP_WIDTH]
        ms = jnp.mean(yg * yg, axis=-1, keepdims=True)
        parts.append(yg * lax.rsqrt(ms + EPS))
    y_ssd = jnp.concatenate(parts, axis=1) * gn_ref[...]
    mix = jnp.concatenate([y_lru, y_ssd], axis=1).astype(bf16)
    xn = x_ref[...] + jnp.dot(mix, wo_ref[...], preferred_element_type=f32)
    xo_ref[...] = xn
    hn = _rms(xn, gf_ref[...])
    tm = hn.shape[0]
    for k in range(hn.shape[1] // LANES):
        hn_ref[pl.ds(k, tm, stride=SUBLANES), :] = hn[:, k * LANES:(k + 1) * LANES]
    h_hi = hn.astype(bf16)
    h_lo = (hn - h_hi.astype(f32)).astype(bf16)
    lg = jnp.dot(h_hi, wrh_ref[...], preferred_element_type=f32)
    lg = lg + jnp.dot(h_lo, wrh_ref[...], preferred_element_type=f32)
    lg = lg + jnp.dot(h_hi, wrl_ref[...], preferred_element_type=f32)
    lg_ref[...] = lg


def _mix_out(x2d, hf, hb, gate, yf, yb, z, ssd_norm, w_out_bf16, norm_ffn, wr_hi, wr_lo):
    t, d = x2d.shape
    tm = _pick(t, 256)
    row = lambda w: pl.BlockSpec((tm, w), lambda i: (i, 0))
    full = lambda shape: pl.BlockSpec(shape, lambda i: (0,) * len(shape))
    return pl.pallas_call(
        _mix_out_body,
        grid=(t // tm,),
        in_specs=[row(d)] + [row(LRU_WIDTH)] * 3 + [row(SSD_WIDTH)] * 3 + [
            full((1, SSD_WIDTH)), full((LRU_WIDTH + SSD_WIDTH, d)), full((1, d)),
            full((d, LANES)), full((d, LANES)),
        ],
        out_specs=[row(d), pl.BlockSpec((tm * d // LANES, LANES), lambda i: (i, 0)), row(LANES)],
        out_shape=[jax.ShapeDtypeStruct((t, d), f32), jax.ShapeDtypeStruct((t * d // LANES, LANES), f32),
                   jax.ShapeDtypeStruct((t, LANES), f32)],
        compiler_params=_params(("parallel",), VMEM_LIMIT),
        name="mix_out",
    )(x2d, hf, hb, gate, yf, yb, z, ssd_norm.reshape(1, -1), w_out_bf16, norm_ffn.reshape(1, d),
      wr_hi, wr_lo)


def _tile_prefix(tiles, upper_incl, upper_strict, lane):
    incs = [jnp.dot(t.astype(bf16), upper_incl, preferred_element_type=f32) for t in tiles]
    tot = jnp.zeros((N_EXPERTS, LANES), f32)
    for j, inc in enumerate(incs):
        tot = jnp.where(lane == j, inc[:, LANES - 1:LANES], tot)
    start = jnp.dot(tot.astype(bf16), upper_strict, preferred_element_type=f32)
    return incs, tot, start


def _route_body(lg_ref, idx_ref, gate_ref, lpos_ref, tab_ref, lm_scr, vt_scr, list_scr, tabv_scr, tabs_scr, sem,
                *, cap):
    s = lg_ref.shape[1]
    nt = s // LANES
    lt = lg_ref[0].T[:N_EXPERTS]
    mx = jnp.max(lt, axis=0, keepdims=True)
    ex = jnp.exp(lt - mx)
    aff = ex / jnp.sum(ex, axis=0, keepdims=True)
    key = pltpu.bitcast(aff, i32)

    def search(i, thr):
        cand = thr | (jnp.int32(1) << (30 - i))
        cnt = jnp.sum((key >= cand).astype(f32), axis=1, keepdims=True)
        return jnp.where(cnt >= float(cap), cand, thr)

    thr = lax.fori_loop(0, 31, search, jnp.zeros((N_EXPERTS, 1), i32))
    gt = key > thr
    eq = key == thr
    need = float(cap) - jnp.sum(gt.astype(f32), axis=1, keepdims=True)

    sub_i = lax.broadcasted_iota(i32, (LANES, LANES), 0)
    lane_i = lax.broadcasted_iota(i32, (LANES, LANES), 1)
    upper_incl = (sub_i <= lane_i).astype(bf16)
    upper_strict = (sub_i < lane_i).astype(bf16)
    lane_e = lax.broadcasted_iota(i32, (N_EXPERTS, LANES), 1)
    tiles = lambda a: [a[:, j * LANES:(j + 1) * LANES] for j in range(nt)]

    eq_t = tiles(eq.astype(f32))
    incs, _, start = _tile_prefix(eq_t, upper_incl, upper_strict, lane_e)
    sel_t = []
    for j, (gtj, eqj) in enumerate(zip(tiles(gt), eq_t)):
        excl = incs[j] - eqj + start[:, j:j + 1]
        sel_t.append(jnp.logical_or(gtj, jnp.logical_and(eqj > 0.5, excl < need)).astype(f32))

    incs, tot, start = _tile_prefix(sel_t, upper_incl, upper_strict, lane_e)
    start8 = jnp.floor(start * (1.0 / BF16_ROWS)) * float(BF16_ROWS)
    tab_ref[0, 0] = start8.astype(i32)
    tab_ref[0, 1] = tot.astype(i32)
    rows_needed = start - start8 + tot
    tab_ref[0, 2] = rows_needed.astype(i32)
    tab_ref[0, 3] = jnp.broadcast_to(jnp.max(rows_needed, axis=0, keepdims=True), rows_needed.shape).astype(i32)
    tabv_scr[...] = start.astype(i32)
    to_smem = pltpu.make_async_copy(tabv_scr, tabs_scr, sem)
    to_smem.start()

    kind = lax.broadcasted_iota(i32, (SUBLANES, LANES), 0)
    tok_lane = lax.broadcasted_iota(i32, (SUBLANES, LANES), 1).astype(f32)
    fill = jnp.full((LANES - N_EXPERTS, LANES), -1.0, f32)
    zrows = jnp.zeros((LANES - 3 * N_EXPERTS - SUBLANES, LANES), f32)
    for j, a in enumerate(tiles(aff)):
        lm = jnp.where(sel_t[j] > 0.5, incs[j] - sel_t[j], -1.0)
        lm_scr[j] = lm
        shifted = jnp.where(sel_t[j] > 0.5, lm + (start[:, j:j + 1] - start8[:, j:j + 1]), -1.0)
        lpos_ref[0, pl.ds(j * LANES, LANES), :] = jnp.concatenate([shifted, fill], axis=0).T
        a_hi = a.astype(bf16).astype(f32)
        a_mid = (a - a_hi).astype(bf16).astype(f32)
        a_lo = (a - a_hi - a_mid).astype(bf16).astype(f32)
        tok = jnp.where(kind == 0, float(j), jnp.where(kind == 1, tok_lane, 0.0))
        vt_scr[j] = jnp.concatenate([a_hi, a_mid, a_lo, tok, zrows], axis=0).T.astype(bf16)

    to_smem.wait()
    rank = sub_i.astype(f32)

    def compact(j, carry):
        lm = lm_scr[j]
        vt = vt_scr[j]
        for e in range(N_EXPERTS):
            onehot = (lm[e:e + 1] == rank).astype(bf16)
            packed = jnp.dot(onehot, vt, preferred_element_type=f32)
            list_scr[e, pl.ds(tabs_scr[e, j], LANES), :] = packed
        return carry

    lax.fori_loop(0, nt, compact, 0)

    lane_c = lax.broadcasted_iota(i32, (cap, LANES), 1)
    idx_c = jnp.zeros((cap, LANES), f32)
    gate_c = jnp.zeros((cap, LANES), f32)
    for e in range(N_EXPERTS):
        rows = list_scr[e, 0:cap, :]
        g = rows[:, e:e + 1] + rows[:, N_EXPERTS + e:N_EXPERTS + e + 1] + rows[:, 2 * N_EXPERTS + e:2 * N_EXPERTS + e + 1]
        t = rows[:, 3 * N_EXPERTS:3 * N_EXPERTS + 1] * float(LANES) + rows[:, 3 * N_EXPERTS + 1:3 * N_EXPERTS + 2]
        idx_c = jnp.where(lane_c == e, t, idx_c)
        gate_c = jnp.where(lane_c == e, g, gate_c)
    idx_ref[0] = idx_c.T[:N_EXPERTS].astype(i32)
    gate_ref[0] = gate_c.T[:N_EXPERTS]


def _route(logits, cap):
    b, s, _ = logits.shape
    nt = s // LANES
    assert nt <= LANES
    return pl.pallas_call(
        functools.partial(_route_body, cap=cap),
        grid=(b,),
        in_specs=[pl.BlockSpec((1, s, LANES), lambda i: (i, 0, 0))],
        out_specs=[pl.BlockSpec((1, N_EXPERTS, cap), lambda i: (i, 0, 0)),
                   pl.BlockSpec((1, N_EXPERTS, cap), lambda i: (i, 0, 0)),
                   pl.BlockSpec((1, s, LANES), lambda i: (i, 0, 0)),
                   pl.BlockSpec((1, 4, N_EXPERTS, LANES), lambda i: (i, 0, 0, 0))],
        out_shape=[jax.ShapeDtypeStruct((b, N_EXPERTS, cap), i32),
                   jax.ShapeDtypeStruct((b, N_EXPERTS, cap), f32),
                   jax.ShapeDtypeStruct((b, s, LANES), f32),
                   jax.ShapeDtypeStruct((b, 4, N_EXPERTS, LANES), i32)],
        scratch_shapes=[pltpu.VMEM((nt, N_EXPERTS, LANES), f32),
                        pltpu.VMEM((nt, LANES, LANES), bf16),
                        pltpu.VMEM((N_EXPERTS, cap + LANES, LANES), f32),
                        pltpu.VMEM((N_EXPERTS, LANES), i32),
                        pltpu.SMEM((N_EXPERTS, LANES), i32),
                        pltpu.SemaphoreType.DMA],
        compiler_params=_params(("parallel",), VMEM_LIMIT),
        name="route",
    )(logits)


def _token_copy(src_hbm, tok, dst, r, sem):
    src = src_hbm.at[pl.ds(pl.multiple_of(tok * SUBLANES, SUBLANES), SUBLANES), :]
    return pltpu.make_async_copy(src, dst.at[pl.ds(pl.multiple_of(r * SUBLANES, SUBLANES), SUBLANES), :], sem)


def _moe_body(idx_ref, idxn_ref, gate_ref, wg_ref, wu_ref, wd_ref, hn_hbm, o_ref, xg, sem, *, cap, seq, nb):
    n = pl.program_id(0) * nb + pl.program_id(1)
    total = N_EXPERTS * nb
    slot = lax.rem(n, 2)
    other = 1 - slot

    @pl.when(n == 0)
    def _():
        def start(r, carry):
            _token_copy(hn_hbm, idx_ref[0, 0, 0, r], xg.at[slot], r, sem.at[slot]).start()
            return carry

        lax.fori_loop(0, cap, start, 0, unroll=8)

    def wait_all(buf):
        pltpu.make_async_copy(hn_hbm.at[pl.ds(0, cap * SUBLANES), :], xg.at[buf], sem.at[buf]).wait()

    wait_all(slot)

    base_next = lax.rem(jnp.minimum(n + 1, total - 1), nb) * seq
    tm = min(cap, 256)
    ntile = wg_ref.shape[1] // LANES
    for m in range(cap // tm):
        rows = slice(m * tm, (m + 1) * tm)
        for r in range(m * tm, (m + 1) * tm):
            _token_copy(hn_hbm, base_next + idxn_ref[0, 0, 0, r], xg.at[other], r, sem.at[other]).start()
        xm = jnp.concatenate(
            [xg[slot, pl.ds(m * tm * SUBLANES + k, tm, stride=SUBLANES), :] for k in range(ntile)],
            axis=1).astype(bf16)
        hg = jnp.dot(xm, wg_ref[0], preferred_element_type=f32)
        hu = jnp.dot(xm, wu_ref[0], preferred_element_type=f32)
        hid = (jax.nn.silu(hg) * hu).astype(bf16)
        y = jnp.dot(hid, wd_ref[0], preferred_element_type=f32) * gate_ref[0, 0, rows, :]
        o_ref[0, 0, rows, :] = y.astype(bf16)
    o_ref[0, 0, cap:cap + WINDOW, :] = jnp.zeros((WINDOW, o_ref.shape[-1]), bf16)

    @pl.when(n == total - 1)
    def _():
        wait_all(other)


def _moe_ffn(hn_tiles, idx, gates, wg, wu, wd, layer):
    b, _, cap = idx.shape
    d = wg.shape[2]
    s = hn_tiles.shape[0] * LANES // d // b
    idx4 = idx.reshape(b, N_EXPERTS, 1, cap)
    gates4 = gates.reshape(b, N_EXPERTS, cap, 1)
    def next_step(e, i):
        n1 = jnp.minimum(e * b + i + 1, N_EXPERTS * b - 1)
        return (lax.rem(n1, b), n1 // b, 0, 0)

    return pl.pallas_call(
        functools.partial(_moe_body, cap=cap, seq=s, nb=b),
        grid=(N_EXPERTS, b),
        in_specs=[
            pl.BlockSpec((1, 1, 1, cap), lambda e, i: (i, e, 0, 0), memory_space=pltpu.SMEM),
            pl.BlockSpec((1, 1, 1, cap), next_step, memory_space=pltpu.SMEM),
            pl.BlockSpec((1, 1, cap, 1), lambda e, i: (i, e, 0, 0)),
            pl.BlockSpec((None, 1, d, D_FF), lambda e, i: (layer, e, 0, 0)),
            pl.BlockSpec((None, 1, d, D_FF), lambda e, i: (layer, e, 0, 0)),
            pl.BlockSpec((None, 1, D_FF, d), lambda e, i: (layer, e, 0, 0)),
            pl.BlockSpec(memory_space=pl.ANY),
        ],
        out_specs=pl.BlockSpec((1, 1, cap + WINDOW, d), lambda e, i: (i, e, 0, 0)),
        out_shape=jax.ShapeDtypeStruct((b, N_EXPERTS, cap + WINDOW, d), bf16),
        scratch_shapes=[pltpu.VMEM((2, cap * d // LANES, LANES), f32), pltpu.SemaphoreType.DMA((2,))],
        compiler_params=_params(("arbitrary", "arbitrary"), VMEM_LIMIT),
        name="moe_ffn",
    )(idx4, idx4, gates4, wg, wu, wd, hn_tiles)


SLOT = 64


def _short_copy(yg_hbm, b, e, start, packed, slot, sem):
    src = yg_hbm.at[b, e, pl.ds(pl.multiple_of(start, BF16_ROWS), SLOT), :]
    return pltpu.make_async_copy(src, packed.at[slot, pl.ds(e * SLOT, SLOT), :], sem.at[0, slot])


def _full_copy(yg_hbm, b, e, start, spill, slot, sem):
    src = yg_hbm.at[b, e, pl.ds(pl.multiple_of(start, BF16_ROWS), WINDOW), :]
    return pltpu.make_async_copy(src, spill.at[slot, e], sem.at[1, slot])


def _combine_body(tab_ref, tabn_ref, lpos_ref, x_ref, g_ref, yg_hbm, o_ref, packed, spill, sem,
                  *, nb, nt, final_norm):
    b = pl.program_id(0)
    j = pl.program_id(1)
    n = b * nt + j
    slot = lax.rem(n, 2)

    def fetch(tab, bb, jj, buf):
        for e in range(N_EXPERTS):
            _short_copy(yg_hbm, bb, e, tab[0, 0, e, jj], packed, buf, sem).start()

        @pl.when(tab[0, 3, 0, jj] > SLOT)
        def _():
            for e in range(N_EXPERTS):
                @pl.when(tab[0, 2, e, jj] > SLOT)
                def _(e=e):
                    _full_copy(yg_hbm, bb, e, tab[0, 0, e, jj], spill, buf, sem).start()

    @pl.when(n == 0)
    def _():
        packed[...] = jnp.zeros_like(packed)
        spill[...] = jnp.zeros_like(spill)
        fetch(tab_ref, b, j, slot)

    @pl.when(n + 1 < nb * nt)
    def _():
        wrap = j + 1 == nt
        fetch(tabn_ref, jnp.where(wrap, b + 1, b), jnp.where(wrap, 0, j + 1), 1 - slot)

    lm = lpos_ref[0]
    lane = lax.broadcasted_iota(i32, (LANES, LANES), 1).astype(f32)
    per_tile = LANES // SLOT
    tiles = []
    for lt in range(N_EXPERTS // per_tile):
        hit = None
        for i in range(per_tile):
            e = lt * per_tile + i
            col = jnp.where(tab_ref[0, 2, e, j] <= SLOT, lm[:, e:e + 1], -1.0)
            col = jnp.broadcast_to(jnp.where(col >= 0.0, col + float(i * SLOT), -1.0), (LANES, LANES))
            h = col == lane
            hit = h if hit is None else jnp.logical_or(hit, h)
        tiles.append(hit.astype(bf16))
    place = jnp.concatenate(tiles, axis=1)

    for e in range(N_EXPERTS):
        _short_copy(yg_hbm, b, e, 0, packed, slot, sem).wait()
    o_ref[...] = x_ref[...] + jnp.dot(place, packed[slot], preferred_element_type=f32)

    @pl.when(tab_ref[0, 3, 0, j] > SLOT)
    def _():
        row = lax.broadcasted_iota(i32, (LANES, WINDOW), 1).astype(f32)
        for e in range(N_EXPERTS):
            @pl.when(tab_ref[0, 2, e, j] > SLOT)
            def _(e=e):
                _full_copy(yg_hbm, b, e, 0, spill, slot, sem).wait()
                own = (lm[:, e:e + 1] == row).astype(bf16)
                o_ref[...] += jnp.dot(own, spill[slot, e], preferred_element_type=f32)

    if final_norm:
        o_ref[...] = _rms(o_ref[...], g_ref[...])


def _combine(x2d, lpos, tab, yg, nb, final_g=None):
    t, d = x2d.shape
    g = jnp.ones((1, d), f32) if final_g is None else final_g.reshape(1, d)
    nt = t // nb // LANES

    def next_batch(i, j):
        return (jnp.minimum(i + (j + 1) // nt, nb - 1), 0, 0, 0)

    return pl.pallas_call(
        functools.partial(_combine_body, nb=nb, nt=nt, final_norm=final_g is not None),
        grid=(nb, nt),
        in_specs=[
            pl.BlockSpec((1, 4, N_EXPERTS, LANES), lambda i, j: (i, 0, 0, 0), memory_space=pltpu.SMEM),
            pl.BlockSpec((1, 4, N_EXPERTS, LANES), next_batch, memory_space=pltpu.SMEM),
            pl.BlockSpec((1, LANES, LANES), lambda i, j: (i, j, 0)),
            pl.BlockSpec((LANES, d), lambda i, j: (i * nt + j, 0)),
            pl.BlockSpec((1, d), lambda i, j: (0, 0)),
            pl.BlockSpec(memory_space=pl.ANY),
        ],
        out_specs=pl.BlockSpec((LANES, d), lambda i, j: (i * nt + j, 0)),
        out_shape=jax.ShapeDtypeStruct((t, d), f32),
        scratch_shapes=[pltpu.VMEM((2, N_EXPERTS * SLOT, d), bf16), pltpu.VMEM((2, N_EXPERTS, WINDOW, d), bf16),
                        pltpu.SemaphoreType.DMA((2, 2))],
        compiler_params=_params(("arbitrary", "arbitrary")),
        name="moe_combine",
    )(tab, tab, lpos, x2d, g, yg)


def _pad_lanes(v):
    return jnp.zeros((1, LANES), f32).at[0, :v.shape[0]].set(v.astype(f32))


def _mixer(x3, norm_mix, w_in, lru_conv_w, lru_conv_b, lru_wa, lru_ba, lru_wi, lru_bi, lru_lambda,
           ssd_conv_w, ssd_conv_b, ssd_a_log, ssd_dt_bias, ssd_d):
    b, s, d = x3.shape
    x2d = x3.reshape(b * s, d)
    pad = jnp.zeros((d, LANES - 2 * SSD_HEADS), bf16)
    w_bf = jnp.concatenate([w_in.astype(bf16), pad], axis=1)
    xc, gate, z, xact, dt = _in_proj(x2d, norm_mix, w_bf, lru_conv_w, lru_conv_b, ssd_conv_w, ssd_conv_b, s)
    xc = xc.reshape(b, s, LRU_WIDTH)
    xact = xact.reshape(b, s, SSD_CONV_CH)
    dt = dt.reshape(b, s, LANES)

    w_gate = jnp.concatenate([lru_wa, lru_wi], axis=-1).astype(bf16)
    hs = _lru_scan(xc, w_gate, lru_ba, lru_bi, lru_lambda)
    alog_pad = _pad_lanes(ssd_a_log.reshape(-1))
    dtb_pad = _pad_lanes(ssd_dt_bias.reshape(-1))
    dskip_pad = _pad_lanes(ssd_d)
    ys = _ssd_scan(xact, dt, alog_pad, dtb_pad, dskip_pad)
    t = b * s
    return (hs[0].reshape(t, -1), hs[1].reshape(t, -1), gate, ys[0].reshape(t, -1), ys[1].reshape(t, -1), z)


def _layer(x3, norm_mix, w_in, lru_conv_w, lru_conv_b, lru_wa, lru_ba, lru_wi, lru_bi, lru_lambda,
           ssd_conv_w, ssd_conv_b, ssd_a_log, ssd_dt_bias, ssd_d, ssd_norm, w_out, norm_ffn, w_router,
           expert_w, layer, final_g):
    b, s, d = x3.shape
    cap = max(1, CAPACITY_FACTOR * s // N_EXPERTS)
    assert s % LANES == 0 and cap % LANES == 0, "sequence length must give 128-aligned expert capacity"
    hf, hb, gate, yf, yb, z = _mixer(x3, norm_mix, w_in, lru_conv_w, lru_conv_b, lru_wa, lru_ba, lru_wi,
                                     lru_bi, lru_lambda, ssd_conv_w, ssd_conv_b, ssd_a_log, ssd_dt_bias, ssd_d)
    wr = jnp.zeros((d, LANES), f32).at[:, :N_EXPERTS].set(w_router)
    wr_hi = wr.astype(bf16)
    wr_lo = (wr - wr_hi.astype(f32)).astype(bf16)
    xn, hn, logits = _mix_out(x3.reshape(b * s, d), hf, hb, gate, yf, yb, z, ssd_norm, w_out.astype(bf16),
                              norm_ffn, wr_hi, wr_lo)
    idx, gates, lpos, tab = _route(logits.reshape(b, s, LANES), cap)
    yg = _moe_ffn(hn, idx, gates, *expert_w, layer)
    return _combine(xn, lpos, tab, yg, b, final_g).reshape(b, s, d)


def kernel(x, norm_mix, w_in, lru_conv_w, lru_conv_b, lru_wa, lru_ba, lru_wi, lru_bi, lru_lambda, ssd_conv_w, ssd_conv_b, ssd_a_log, ssd_dt_bias, ssd_d, ssd_norm, w_out, norm_ffn, w_router, w_gate, w_up, w_down, norm_final):
    depth = norm_mix.shape[0]
    expert_w = (w_gate.astype(bf16), w_up.astype(bf16), w_down.astype(bf16))
    for l in range(depth):
        x = _layer(x, norm_mix[l], w_in[l], lru_conv_w[l], lru_conv_b[l], lru_wa[l], lru_ba[l], lru_wi[l],
                   lru_bi[l], lru_lambda[l], ssd_conv_w[l], ssd_conv_b[l], ssd_a_log[l], ssd_dt_bias[l],
                   ssd_d[l], ssd_norm[l], w_out[l], norm_ffn[l], w_router[l], expert_w, l,
                   norm_final if l == depth - 1 else None)
    return x
```

```python
import functools

import jax
import jax.numpy as jnp
from jax import lax
from jax.experimental import pallas as pl
from jax.experimental.pallas import tpu as pltpu

f32 = jnp.float32
bf16 = jnp.bfloat16
i32 = jnp.int32

D_MODEL = 1024
EPS = 1e-6
CONV_WIDTH = 4
LRU_WIDTH = 1024
LRU_HEADS = 8
LRU_BLOCK = 128
LRU_C = 8.0
SSD_WIDTH = 1024
SSD_HEADDIM = 64
SSD_HEADS = 16
SSD_GROUPS = 4
SSD_STATE = 128
SSD_CHUNK = 128
SSD_GN = SSD_GROUPS * SSD_STATE
SSD_CONV_CH = SSD_WIDTH + 2 * SSD_GN
GROUP_WIDTH = SSD_WIDTH // SSD_GROUPS
N_EXPERTS = 16
CAPACITY_FACTOR = 2
D_FF = 2048

LANES = 128
SUBLANES = 8
HALO = SUBLANES
BF16_ROWS = 2 * SUBLANES
WINDOW = LANES + BF16_ROWS
VMEM_LIMIT = 56 * 1024 * 1024


def _pick(n, target):
    if n <= target:
        return n
    t = target
    while t >= LANES:
        if n % t == 0:
            return t
        t -= LANES
    return n


def _params(sem, vmem=None):
    return pltpu.CompilerParams(dimension_semantics=sem, vmem_limit_bytes=vmem)


IN_SEGMENTS = (LRU_WIDTH, LRU_WIDTH, SSD_WIDTH, SSD_CONV_CH, LANES)


def _conv_centred(pe, first, last, cw, cb):
    tm = pe.shape[0] - 2 * HALO
    before = jnp.where(first, 0.0, pe[:HALO])
    after = jnp.where(last, 0.0, pe[HALO + tm:])
    c = pe[HALO:HALO + tm]
    r8 = lax.broadcasted_iota(i32, (SUBLANES, pe.shape[1]), 0)
    back1 = pltpu.roll(c, 1, 0)
    fwd1 = pltpu.roll(c, tm - 1, 0)
    fwd2 = pltpu.roll(c, tm - 2, 0)
    back1 = jnp.concatenate([jnp.where(r8 == 0, before[HALO - 1:HALO], back1[:HALO]), back1[HALO:]], axis=0)
    fwd1 = jnp.concatenate([fwd1[:tm - HALO], jnp.where(r8 == HALO - 1, after[0:1], fwd1[tm - HALO:])], axis=0)
    tail2 = jnp.where(r8 == HALO - 2, after[0:1], jnp.where(r8 == HALO - 1, after[1:2], fwd2[tm - HALO:]))
    fwd2 = jnp.concatenate([fwd2[:tm - HALO], tail2], axis=0)
    return cb + back1 * cw[0:1] + c * cw[1:2] + fwd1 * cw[2:3] + fwd2 * cw[3:4]


def _in_proj_body(prev_ref, x_ref, next_ref, g_ref, w_ref, lcw_ref, lcb_ref, scw_ref, scb_ref,
                  xc_ref, gate_ref, z_ref, xa_ref, dt_ref, *, tiles_per_seq):
    pos = lax.rem(pl.program_id(0), tiles_per_seq)
    first = pos == 0
    last = pos == tiles_per_seq - 1
    tm = x_ref.shape[0]
    xe = jnp.concatenate([prev_ref[...], x_ref[...], next_ref[...]], axis=0)
    ms = jnp.mean(xe * xe, axis=-1, keepdims=True)
    hn = (xe * lax.rsqrt(ms + EPS) * g_ref[...]).astype(bf16)
    hc = hn[HALO:HALO + tm]
    o0, o1, o2, o3 = LRU_WIDTH, 2 * LRU_WIDTH, 2 * LRU_WIDTH + SSD_WIDTH, 2 * LRU_WIDTH + SSD_WIDTH + SSD_CONV_CH
    pe = jnp.dot(hn, w_ref[:, :o0], preferred_element_type=f32)
    xc_ref[...] = _conv_centred(pe, first, last, lcw_ref[...], lcb_ref[...]).astype(bf16)
    gate_ref[...] = jnp.dot(hc, w_ref[:, o0:o1], preferred_element_type=f32).astype(bf16)
    z_ref[...] = jnp.dot(hc, w_ref[:, o1:o2], preferred_element_type=f32).astype(bf16)
    pe = jnp.dot(hn, w_ref[:, o2:o3], preferred_element_type=f32)
    xa_ref[...] = jax.nn.silu(_conv_centred(pe, first, last, scw_ref[...], scb_ref[...])).astype(bf16)
    dt_ref[...] = jnp.dot(hc, w_ref[:, o3:], preferred_element_type=f32)


def _in_proj(x2d, g, w_bf16, lru_cw, lru_cb, ssd_cw, ssd_cb, seq):
    t, d = x2d.shape
    n = w_bf16.shape[1]
    tm = _pick(seq, 512)
    per = tm // HALO
    last_blk = t // HALO - 1
    full = lambda shape: pl.BlockSpec(shape, lambda i: (0,) * len(shape))
    return pl.pallas_call(
        functools.partial(_in_proj_body, tiles_per_seq=seq // tm),
        grid=(t // tm,),
        in_specs=[
            pl.BlockSpec((HALO, d), lambda i: (jnp.maximum(i * per - 1, 0), 0)),
            pl.BlockSpec((tm, d), lambda i: (i, 0)),
            pl.BlockSpec((HALO, d), lambda i: (jnp.minimum((i + 1) * per, last_blk), 0)),
            full((1, d)),
            pl.BlockSpec((d, n), lambda i: (0, 0), pipeline_mode=pl.Buffered(1)),
            full((CONV_WIDTH, LRU_WIDTH)), full((1, LRU_WIDTH)),
            full((CONV_WIDTH, SSD_CONV_CH)), full((1, SSD_CONV_CH)),
        ],
        out_specs=[pl.BlockSpec((tm, w), lambda i: (i, 0)) for w in IN_SEGMENTS],
        out_shape=[jax.ShapeDtypeStruct((t, w), dt) for w, dt in zip(IN_SEGMENTS, (bf16, bf16, bf16, bf16, f32))],
        compiler_params=_params(("parallel",), VMEM_LIMIT),
        name="in_proj",
    )(x2d, x2d, x2d, g.reshape(1, d), w_bf16, lru_cw, lru_cb.reshape(1, -1), ssd_cw, ssd_cb.reshape(1, -1))


def _lru_gates(x_ref, perm_ref, w_ref, ba_ref, bi_ref, lam_ref, a_scr, u_scr, d):
    xc = jnp.dot(perm_ref[0], x_ref[0], preferred_element_type=f32).astype(bf16)
    sp = jax.nn.softplus(-lam_ref[d:d + 1, :])
    for h in range(LRU_HEADS):
        sl = slice(h * LRU_BLOCK, (h + 1) * LRU_BLOCK)
        pre = jnp.dot(xc[:, sl], w_ref[d, h], preferred_element_type=f32)
        xh = xc[:, sl].astype(f32)
        r = jax.nn.sigmoid(pre[:, :LRU_BLOCK] + ba_ref[d:d + 1, sl])
        gi = jax.nn.sigmoid(pre[:, LRU_BLOCK:] + bi_ref[d:d + 1, sl])
        log_a = (-LRU_C) * r * sp[:, sl]
        a = jnp.exp(log_a)
        u = jnp.sqrt(1.0 - a * a) * (gi * xh)
        a_scr[d, h] = a
        u_scr[d, h] = u


def _lru_direction(d, perm_ref, o_ref, carry_ref, a_scr, u_scr):
    tc = o_ref.shape[1]
    sub = tc // SUBLANES
    heads = range(LRU_HEADS)

    def block(j):
        jj = (sub - 1 - j) if d else j
        return pl.ds(pl.multiple_of(jj * SUBLANES, SUBLANES), SUBLANES)

    def local(j, state):
        rows = block(j)
        out = []
        for h in heads:
            a = a_scr[d, h, rows, :]
            out += [a * state[2 * h] + u_scr[d, h, rows, :], a * state[2 * h + 1]]
        return tuple(out)

    init = []
    for h in heads:
        init += [jnp.zeros((SUBLANES, LRU_BLOCK), f32), jnp.ones((SUBLANES, LRU_BLOCK), f32)]
    ends = lax.fori_loop(0, sub, local, tuple(init), unroll=True)

    sub_i = lax.broadcasted_iota(i32, (SUBLANES, LRU_BLOCK), 0)
    carries = []
    for h in heads:
        sl = slice(h * LRU_BLOCK, (h + 1) * LRU_BLOCK)
        c = carry_ref[d:d + 1, sl]
        cin = jnp.zeros((SUBLANES, LRU_BLOCK), f32)
        for s in (reversed(range(SUBLANES)) if d else range(SUBLANES)):
            cin = jnp.where(sub_i == s, c, cin)
            c = ends[2 * h][s:s + 1] + ends[2 * h + 1][s:s + 1] * c
        carry_ref[d:d + 1, sl] = c
        carries.append(cin)

    def final(j, state):
        rows = block(j)
        out = []
        for h in heads:
            hcur = a_scr[d, h, rows, :] * state[h] + u_scr[d, h, rows, :]
            u_scr[d, h, rows, :] = hcur
            out.append(hcur)
        return tuple(out)

    lax.fori_loop(0, sub, final, tuple(carries), unroll=True)

    hp = jnp.concatenate([u_scr[d, h].astype(bf16) for h in heads], axis=1)
    o_ref[0] = jnp.dot(perm_ref[1], hp, preferred_element_type=f32).astype(bf16)


def _lru_body(xf_ref, xb_ref, perm_ref, w_ref, ba_ref, bi_ref, lam_ref, of_ref, ob_ref, carry_ref, a_scr, u_scr):
    @pl.when(pl.program_id(1) == 0)
    def _():
        carry_ref[...] = jnp.zeros_like(carry_ref)

    _lru_gates(xf_ref, perm_ref, w_ref, ba_ref, bi_ref, lam_ref, a_scr, u_scr, 0)
    _lru_gates(xb_ref, perm_ref, w_ref, ba_ref, bi_ref, lam_ref, a_scr, u_scr, 1)
    _lru_direction(0, perm_ref, of_ref, carry_ref, a_scr, u_scr)
    _lru_direction(1, perm_ref, ob_ref, carry_ref, a_scr, u_scr)


def _lru_scan(xc, w_gate, ba, bi, lam):
    b, s, w = xc.shape
    tc = _pick(s, 256)
    nc = s // tc
    full = lambda shape: pl.BlockSpec(shape, lambda b_, c_: (0,) * len(shape))
    fwd = pl.BlockSpec((1, tc, w), lambda b_, c_: (b_, c_, 0))
    bwd = pl.BlockSpec((1, tc, w), lambda b_, c_: (b_, nc - 1 - c_, 0))
    p = jnp.arange(tc)
    to_strided = ((p % SUBLANES) * (tc // SUBLANES) + p // SUBLANES)[:, None] == jnp.arange(tc)[None, :]
    perm = jnp.stack([to_strided, to_strided.T]).astype(bf16)
    return pl.pallas_call(
        _lru_body,
        grid=(b, nc),
        in_specs=[fwd, bwd, full((2, tc, tc)), full((2, LRU_HEADS, LRU_BLOCK, 2 * LRU_BLOCK)),
                  full((2, w)), full((2, w)), full((2, w))],
        out_specs=[fwd, bwd],
        out_shape=[jax.ShapeDtypeStruct((b, s, w), bf16)] * 2,
        scratch_shapes=[pltpu.VMEM((2, w), f32), pltpu.VMEM((2, LRU_HEADS, tc, LRU_BLOCK), f32),
                        pltpu.VMEM((2, LRU_HEADS, tc, LRU_BLOCK), f32)],
        compiler_params=_params(("parallel", "arbitrary")),
        name="lru_scan",
    )(xc, xc, perm, w_gate, ba, bi, lam)


def _expand_heads(arr, base):
    rows = arr.shape[0]
    lane = lax.broadcasted_iota(i32, (rows, LANES), 1)
    tiles = []
    for k in range(SSD_HEADS // 2):
        c0 = arr[:, base + 2 * k:base + 2 * k + 1]
        c1 = arr[:, base + 2 * k + 1:base + 2 * k + 2]
        tiles.append(jnp.where(lane < SSD_HEADDIM, c0, c1))
    return jnp.concatenate(tiles, axis=1)


def _cumsum_rows(x, reverse):
    n = x.shape[0]
    r = lax.broadcasted_iota(i32, x.shape, 0)
    k = 1
    while k < n:
        if reverse:
            x = x + jnp.where(r < n - k, pltpu.roll(x, n - k, 0), 0.0)
        else:
            x = x + jnp.where(r >= k, pltpu.roll(x, k, 0), 0.0)
        k *= 2
    return x


def _ssd_body(xf_ref, xb_ref, dtf_ref, dtb_in_ref, alog_ref, dtb_ref, dskip_ref, sele_ref, selc_ref,
              of_ref, ob_ref, state_ref):
    @pl.when(pl.program_id(1) == 0)
    def _():
        state_ref[...] = jnp.zeros_like(state_ref)

    per_step = xf_ref.shape[1] // SSD_CHUNK
    for k in range(per_step):
        rows_f = slice(k * SSD_CHUNK, (k + 1) * SSD_CHUNK)
        rows_b = slice((per_step - 1 - k) * SSD_CHUNK, (per_step - k) * SSD_CHUNK)
        _ssd_dir(xf_ref, dtf_ref, rows_f, alog_ref, dtb_ref, dskip_ref, sele_ref, selc_ref, of_ref,
                 state_ref.at[0], False)
        _ssd_dir(xb_ref, dtb_in_ref, rows_b, alog_ref, dtb_ref, dskip_ref, sele_ref, selc_ref, ob_ref,
                 state_ref.at[1], True)


def _split_bf16(x, parts):
    out = []
    for _ in range(parts):
        p = x.astype(bf16)
        out.append(p)
        x = x - p.astype(f32)
    return jnp.concatenate(out, axis=1)


def _ssd_selectors(chunk):
    lane = jnp.arange(LANES)[:, None]
    spread, cols = [], []
    for d in range(2):
        head = lane - d * SSD_HEADS
        s = (head == jnp.arange(SSD_WIDTH)[None, :] // SSD_HEADDIM).astype(bf16)
        c = (head == jnp.arange(SSD_HEADS * chunk)[None, :] // chunk).astype(bf16)
        spread.append(jnp.concatenate([s, s], axis=0))
        cols.append(jnp.concatenate([c, c], axis=0))
    return jnp.stack(spread), jnp.stack(cols)


def _ssd_dir(x_ref, dt_ref, rows, alog_ref, dtb_ref, dskip_ref, sele_ref, selc_ref, o_ref, state_ref, reverse):
    L = SSD_CHUNK
    base = SSD_HEADS if reverse else 0
    xact = x_ref[0, rows, :]
    xs = xact[:, :SSD_WIDTH].astype(f32)

    dt = jax.nn.softplus(dt_ref[0, rows, :] + dtb_ref[...])
    d_a = dt * (-jnp.exp(alog_ref[...]))
    cum = _cumsum_rows(d_a, reverse)
    edge = cum[0:1] if reverse else cum[L - 1:L]
    cum_t = cum.T

    d = 1 if reverse else 0
    stack = jnp.concatenate([_split_bf16(dt, 2), _split_bf16(jnp.exp(cum), 2), _split_bf16(jnp.exp(edge - cum), 2)],
                            axis=0)
    spread = jnp.dot(stack, sele_ref[d], preferred_element_type=f32)
    dtx = spread[:L] * xs
    e_cum = spread[L:2 * L]
    e_end = spread[2 * L:]
    cum_cols = jnp.dot(_split_bf16(cum, 2), selc_ref[d], preferred_element_type=f32)
    e_edge = _expand_heads(jnp.exp(edge), base)
    w_all = (e_end * dtx).astype(bf16)
    dtx_b = dtx.astype(bf16)

    li = lax.broadcasted_iota(i32, (L, L), 0)
    si = lax.broadcasted_iota(i32, (L, L), 1)
    tri = (si >= li) if reverse else (li >= si)
    lane = lax.broadcasted_iota(i32, (L, LANES), 1)
    lo_half = lane < SSD_HEADDIM

    outs = []
    for g in range(SSD_GROUPS):
        bsl = slice(SSD_WIDTH + g * SSD_STATE, SSD_WIDTH + (g + 1) * SSD_STATE)
        csl = slice(SSD_WIDTH + SSD_GN + g * SSD_STATE, SSD_WIDTH + SSD_GN + (g + 1) * SSD_STATE)
        gsl = slice(g * GROUP_WIDTH, (g + 1) * GROUP_WIDTH)
        bm = xact[:, bsl]
        cm = xact[:, csl]
        cb = lax.dot_general(cm, bm, (((1,), (1,)), ((), ())), preferred_element_type=f32)
        s_old = state_ref[g]
        y_off = jnp.dot(cm, s_old.astype(bf16), preferred_element_type=f32) * e_cum[:, gsl]
        tiles = []
        for p in range(2):
            tsl = slice(g * GROUP_WIDTH + p * LANES, g * GROUP_WIDTH + (p + 1) * LANES)
            ms, rhs = [], []
            for q in range(2):
                h = g * 4 + 2 * p + q
                j = base + h
                seg = cum_cols[:, h * L:(h + 1) * L] - cum_t[j:j + 1, :]
                decay = jnp.exp(jnp.where(tri, seg, -jnp.inf))
                ms.append((cb * decay).astype(bf16))
                rhs.append(jnp.where(lo_half if q == 0 else jnp.logical_not(lo_half), dtx_b[:, tsl], 0.0))
            tiles.append(jnp.dot(jnp.concatenate(ms, axis=1), jnp.concatenate(rhs, axis=0).astype(bf16),
                                 preferred_element_type=f32))
        outs.append(jnp.concatenate(tiles, axis=1) + y_off)
        upd = lax.dot_general(bm, w_all[:, gsl], (((0,), (0,)), ((), ())), preferred_element_type=f32)
        state_ref[g] = s_old * e_edge[:, gsl] + upd
    y = jnp.concatenate(outs, axis=1)
    if not reverse:
        y = y + _expand_heads(dskip_ref[...], 0) * xs
    o_ref[0, rows, :] = y.astype(bf16)


def _ssd_scan(xact, dt_pad, alog_pad, dtb_pad, dskip_pad):
    b, s, w = xact.shape
    L = _pick(s, 4 * SSD_CHUNK)
    assert L % SSD_CHUNK == 0
    nc = s // L
    full = lambda shape: pl.BlockSpec(shape, lambda b_, c_: (0,) * len(shape))
    fwd = lambda width: pl.BlockSpec((1, L, width), lambda b_, c_: (b_, c_, 0))
    bwd = lambda width: pl.BlockSpec((1, L, width), lambda b_, c_: (b_, nc - 1 - c_, 0))
    spread, cols = _ssd_selectors(SSD_CHUNK)
    return pl.pallas_call(
        _ssd_body,
        grid=(b, nc),
        in_specs=[fwd(w), bwd(w), fwd(LANES), bwd(LANES), full((1, LANES)), full((1, LANES)), full((1, LANES)),
                  full(spread.shape), full(cols.shape)],
        out_specs=[fwd(SSD_WIDTH), bwd(SSD_WIDTH)],
        out_shape=[jax.ShapeDtypeStruct((b, s, SSD_WIDTH), bf16)] * 2,
        scratch_shapes=[pltpu.VMEM((2, SSD_GROUPS, SSD_STATE, GROUP_WIDTH), f32)],
        compiler_params=_params(("parallel", "arbitrary")),
        name="ssd_scan",
    )(xact, xact, dt_pad, dt_pad, alog_pad, dtb_pad, dskip_pad, spread, cols)


def _rms(x, g):
    ms = jnp.mean(x * x, axis=-1, keepdims=True)
    return x * lax.rsqrt(ms + EPS) * g


def _mix_out_body(x_ref, hf_ref, hb_ref, gate_ref, yf_ref, yb_ref, z_ref, gn_ref, wo_ref,
                  gf_ref, wrh_ref, wrl_ref, xo_ref, hn_ref, lg_ref):
    up = lambda ref: ref[...].astype(f32)
    y_lru = (up(hf_ref) + up(hb_ref)) * jax.nn.gelu(up(gate_ref))
    y = (up(yf_ref) + up(yb_ref)) * jax.nn.silu(up(z_ref))
    parts = []
    for g in range(SSD_GROUPS):
        yg = y[:, g * GROUP_WIDTH:(g + 1) * GROUP_WIDTH]
        ms = jnp.mean(yg * yg, axis=-1, keepdims=True)
        parts.append(yg * lax.rsqrt(ms + EPS))
    y_ssd = jnp.concatenate(parts, axis=1) * gn_ref[...]
    mix = jnp.concatenate([y_lru, y_ssd], axis=1).astype(bf16)
    xn = x_ref[...] + jnp.dot(mix, wo_ref[...], preferred_element_type=f32)
    xo_ref[...] = xn
    hn = _rms(xn, gf_ref[...])
    tm = hn.shape[0]
    for k in range(hn.shape[1] // LANES):
        hn_ref[pl.ds(k, tm, stride=SUBLANES), :] = hn[:, k * LANES:(k + 1) * LANES]
    h_hi = hn.astype(bf16)
    h_lo = (hn - h_hi.astype(f32)).astype(bf16)
    lg = jnp.dot(h_hi, wrh_ref[...], preferred_element_type=f32)
    lg = lg + jnp.dot(h_lo, wrh_ref[...], preferred_element_type=f32)
    lg = lg + jnp.dot(h_hi, wrl_ref[...], preferred_element_type=f32)
    lg_ref[...] = lg


def _mix_out(x2d, hf, hb, gate, yf, yb, z, ssd_norm, w_out_bf16, norm_ffn, wr_hi, wr_lo):
    t, d = x2d.shape
    tm = _pick(t, 256)
    row = lambda w: pl.BlockSpec((tm, w), lambda i: (i, 0))
    full = lambda shape: pl.BlockSpec(shape, lambda i: (0,) * len(shape))
    return pl.pallas_call(
        _mix_out_body,
        grid=(t // tm,),
        in_specs=[row(d)] + [row(LRU_WIDTH)] * 3 + [row(SSD_WIDTH)] * 3 + [
            full((1, SSD_WIDTH)), full((LRU_WIDTH + SSD_WIDTH, d)), full((1, d)),
            full((d, LANES)), full((d, LANES)),
        ],
        out_specs=[row(d), pl.BlockSpec((tm * d // LANES, LANES), lambda i: (i, 0)), row(LANES)],
        out_shape=[jax.ShapeDtypeStruct((t, d), f32), jax.ShapeDtypeStruct((t * d // LANES, LANES), f32),
                   jax.ShapeDtypeStruct((t, LANES), f32)],
        compiler_params=_params(("parallel",), VMEM_LIMIT),
        name="mix_out",
    )(x2d, hf, hb, gate, yf, yb, z, ssd_norm.reshape(1, -1), w_out_bf16, norm_ffn.reshape(1, d),
      wr_hi, wr_lo)


def _tile_prefix(tiles, upper_incl, upper_strict, lane):
    incs = [jnp.dot(t.astype(bf16), upper_incl, preferred_element_type=f32) for t in tiles]
    tot = jnp.zeros((N_EXPERTS, LANES), f32)
    for j, inc in enumerate(incs):
        tot = jnp.where(lane == j, inc[:, LANES - 1:LANES], tot)
    start = jnp.dot(tot.astype(bf16), upper_strict, preferred_element_type=f32)
    return incs, tot, start


def _route_body(lg_ref, idx_ref, gate_ref, lpos_ref, tab_ref, lm_scr, vt_scr, list_scr, tabv_scr, tabs_scr, sem,
                *, cap):
    s = lg_ref.shape[1]
    nt = s // LANES
    lt = lg_ref[0].T[:N_EXPERTS]
    mx = jnp.max(lt, axis=0, keepdims=True)
    ex = jnp.exp(lt - mx)
    aff = ex / jnp.sum(ex, axis=0, keepdims=True)
    key = pltpu.bitcast(aff, i32)

    def search(i, thr):
        cand = thr | (jnp.int32(1) << (30 - i))
        cnt = jnp.sum((key >= cand).astype(f32), axis=1, keepdims=True)
        return jnp.where(cnt >= float(cap), cand, thr)

    thr = lax.fori_loop(0, 31, search, jnp.zeros((N_EXPERTS, 1), i32))
    gt = key > thr
    eq = key == thr
    need = float(cap) - jnp.sum(gt.astype(f32), axis=1, keepdims=True)

    sub_i = lax.broadcasted_iota(i32, (LANES, LANES), 0)
    lane_i = lax.broadcasted_iota(i32, (LANES, LANES), 1)
    upper_incl = (sub_i <= lane_i).astype(bf16)
    upper_strict = (sub_i < lane_i).astype(bf16)
    lane_e = lax.broadcasted_iota(i32, (N_EXPERTS, LANES), 1)
    tiles = lambda a: [a[:, j * LANES:(j + 1) * LANES] for j in range(nt)]

    eq_t = tiles(eq.astype(f32))
    incs, _, start = _tile_prefix(eq_t, upper_incl, upper_strict, lane_e)
    sel_t = []
    for j, (gtj, eqj) in enumerate(zip(tiles(gt), eq_t)):
        excl = incs[j] - eqj + start[:, j:j + 1]
        sel_t.append(jnp.logical_or(gtj, jnp.logical_and(eqj > 0.5, excl < need)).astype(f32))

    incs, tot, start = _tile_prefix(sel_t, upper_incl, upper_strict, lane_e)
    start8 = jnp.floor(start * (1.0 / BF16_ROWS)) * float(BF16_ROWS)
    tab_ref[0, 0] = start8.astype(i32)
    tab_ref[0, 1] = tot.astype(i32)
    rows_needed = start - start8 + tot
    tab_ref[0, 2] = rows_needed.astype(i32)
    tab_ref[0, 3] = jnp.broadcast_to(jnp.max(rows_needed, axis=0, keepdims=True), rows_needed.shape).astype(i32)
    tabv_scr[...] = start.astype(i32)
    to_smem = pltpu.make_async_copy(tabv_scr, tabs_scr, sem)
    to_smem.start()

    kind = lax.broadcasted_iota(i32, (SUBLANES, LANES), 0)
    tok_lane = lax.broadcasted_iota(i32, (SUBLANES, LANES), 1).astype(f32)
    fill = jnp.full((LANES - N_EXPERTS, LANES), -1.0, f32)
    zrows = jnp.zeros((LANES - 3 * N_EXPERTS - SUBLANES, LANES), f32)
    for j, a in enumerate(tiles(aff)):
        lm = jnp.where(sel_t[j] > 0.5, incs[j] - sel_t[j], -1.0)
        lm_scr[j] = lm
        shifted = jnp.where(sel_t[j] > 0.5, lm + (start[:, j:j + 1] - start8[:, j:j + 1]), -1.0)
        lpos_ref[0, pl.ds(j * LANES, LANES), :] = jnp.concatenate([shifted, fill], axis=0).T
        a_hi = a.astype(bf16).astype(f32)
        a_mid = (a - a_hi).astype(bf16).astype(f32)
        a_lo = (a - a_hi - a_mid).astype(bf16).astype(f32)
        tok = jnp.where(kind == 0, float(j), jnp.where(kind == 1, tok_lane, 0.0))
        vt_scr[j] = jnp.concatenate([a_hi, a_mid, a_lo, tok, zrows], axis=0).T.astype(bf16)

    to_smem.wait()
    rank = sub_i.astype(f32)

    def compact(j, carry):
        lm = lm_scr[j]
        vt = vt_scr[j]
        for e in range(N_EXPERTS):
            onehot = (lm[e:e + 1] == rank).astype(bf16)
            packed = jnp.dot(onehot, vt, preferred_element_type=f32)
            list_scr[e, pl.ds(tabs_scr[e, j], LANES), :] = packed
        return carry

    lax.fori_loop(0, nt, compact, 0)

    lane_c = lax.broadcasted_iota(i32, (cap, LANES), 1)
    idx_c = jnp.zeros((cap, LANES), f32)
    gate_c = jnp.zeros((cap, LANES), f32)
    for e in range(N_EXPERTS):
        rows = list_scr[e, 0:cap, :]
        g = rows[:, e:e + 1] + rows[:, N_EXPERTS + e:N_EXPERTS + e + 1] + rows[:, 2 * N_EXPERTS + e:2 * N_EXPERTS + e + 1]
        t = rows[:, 3 * N_EXPERTS:3 * N_EXPERTS + 1] * float(LANES) + rows[:, 3 * N_EXPERTS + 1:3 * N_EXPERTS + 2]
        idx_c = jnp.where(lane_c == e, t, idx_c)
        gate_c = jnp.where(lane_c == e, g, gate_c)
    idx_ref[0] = idx_c.T[:N_EXPERTS].astype(i32)
    gate_ref[0] = gate_c.T[:N_EXPERTS]


def _route(logits, cap):
    b, s, _ = logits.shape
    nt = s // LANES
    assert nt <= LANES
    return pl.pallas_call(
        functools.partial(_route_body, cap=cap),
        grid=(b,),
        in_specs=[pl.BlockSpec((1, s, LANES), lambda i: (i, 0, 0))],
        out_specs=[pl.BlockSpec((1, N_EXPERTS, cap), lambda i: (i, 0, 0)),
                   pl.BlockSpec((1, N_EXPERTS, cap), lambda i: (i, 0, 0)),
                   pl.BlockSpec((1, s, LANES), lambda i: (i, 0, 0)),
                   pl.BlockSpec((1, 4, N_EXPERTS, LANES), lambda i: (i, 0, 0, 0))],
        out_shape=[jax.ShapeDtypeStruct((b, N_EXPERTS, cap), i32),
                   jax.ShapeDtypeStruct((b, N_EXPERTS, cap), f32),
                   jax.ShapeDtypeStruct((b, s, LANES), f32),
                   jax.ShapeDtypeStruct((b, 4, N_EXPERTS, LANES), i32)],
        scratch_shapes=[pltpu.VMEM((nt, N_EXPERTS, LANES), f32),
                        pltpu.VMEM((nt, LANES, LANES), bf16),
                        pltpu.VMEM((N_EXPERTS, cap + LANES, LANES), f32),
                        pltpu.VMEM((N_EXPERTS, LANES), i32),
                        pltpu.SMEM((N_EXPERTS, LANES), i32),
                        pltpu.SemaphoreType.DMA],
        compiler_params=_params(("parallel",), VMEM_LIMIT),
        name="route",
    )(logits)


def _token_copy(src_hbm, tok, dst, r, sem):
    src = src_hbm.at[pl.ds(pl.multiple_of(tok * SUBLANES, SUBLANES), SUBLANES), :]
    return pltpu.make_async_copy(src, dst.at[pl.ds(pl.multiple_of(r * SUBLANES, SUBLANES), SUBLANES), :], sem)


def _moe_body(idx_ref, idxn_ref, gate_ref, wg_ref, wu_ref, wd_ref, hn_hbm, o_ref, xg, sem, *, cap, seq, nb):
    n = pl.program_id(0) * nb + pl.program_id(1)
    total = N_EXPERTS * nb
    slot = lax.rem(n, 2)
    other = 1 - slot

    @pl.when(n == 0)
    def _():
        def start(r, carry):
            _token_copy(hn_hbm, idx_ref[0, 0, 0, r], xg.at[slot], r, sem.at[slot]).start()
            return carry

        lax.fori_loop(0, cap, start, 0, unroll=8)

    def wait_all(buf):
        pltpu.make_async_copy(hn_hbm.at[pl.ds(0, cap * SUBLANES), :], xg.at[buf], sem.at[buf]).wait()

    wait_all(slot)

    base_next = lax.rem(jnp.minimum(n + 1, total - 1), nb) * seq
    tm = min(cap, 256)
    ntile = wg_ref.shape[1] // LANES
    for m in range(cap // tm):
        rows = slice(m * tm, (m + 1) * tm)
        for r in range(m * tm, (m + 1) * tm):
            _token_copy(hn_hbm, base_next + idxn_ref[0, 0, 0, r], xg.at[other], r, sem.at[other]).start()
        xm = jnp.concatenate(
            [xg[slot, pl.ds(m * tm * SUBLANES + k, tm, stride=SUBLANES), :] for k in range(ntile)],
            axis=1).astype(bf16)
        hg = jnp.dot(xm, wg_ref[0], preferred_element_type=f32)
        hu = jnp.dot(xm, wu_ref[0], preferred_element_type=f32)
        hid = (jax.nn.silu(hg) * hu).astype(bf16)
        y = jnp.dot(hid, wd_ref[0], preferred_element_type=f32) * gate_ref[0, 0, rows, :]
        o_ref[0, 0, rows, :] = y.astype(bf16)
    o_ref[0, 0, cap:cap + WINDOW, :] = jnp.zeros((WINDOW, o_ref.shape[-1]), bf16)

    @pl.when(n == total - 1)
    def _():
        wait_all(other)


def _moe_ffn(hn_tiles, idx, gates, wg, wu, wd, layer):
    b, _, cap = idx.shape
    d = wg.shape[2]
    s = hn_tiles.shape[0] * LANES // d // b
    idx4 = idx.reshape(b, N_EXPERTS, 1, cap)
    gates4 = gates.reshape(b, N_EXPERTS, cap, 1)
    def next_step(e, i):
        n1 = jnp.minimum(e * b + i + 1, N_EXPERTS * b - 1)
        return (lax.rem(n1, b), n1 // b, 0, 0)

    return pl.pallas_call(
        functools.partial(_moe_body, cap=cap, seq=s, nb=b),
        grid=(N_EXPERTS, b),
        in_specs=[
            pl.BlockSpec((1, 1, 1, cap), lambda e, i: (i, e, 0, 0), memory_space=pltpu.SMEM),
            pl.BlockSpec((1, 1, 1, cap), next_step, memory_space=pltpu.SMEM),
            pl.BlockSpec((1, 1, cap, 1), lambda e, i: (i, e, 0, 0)),
            pl.BlockSpec((None, 1, d, D_FF), lambda e, i: (layer, e, 0, 0)),
            pl.BlockSpec((None, 1, d, D_FF), lambda e, i: (layer, e, 0, 0)),
            pl.BlockSpec((None, 1, D_FF, d), lambda e, i: (layer, e, 0, 0)),
            pl.BlockSpec(memory_space=pl.ANY),
        ],
        out_specs=pl.BlockSpec((1, 1, cap + WINDOW, d), lambda e, i: (i, e, 0, 0)),
        out_shape=jax.ShapeDtypeStruct((b, N_EXPERTS, cap + WINDOW, d), bf16),
        scratch_shapes=[pltpu.VMEM((2, cap * d // LANES, LANES), f32), pltpu.SemaphoreType.DMA((2,))],
        compiler_params=_params(("arbitrary", "arbitrary"), VMEM_LIMIT),
        name="moe_ffn",
    )(idx4, idx4, gates4, wg, wu, wd, hn_tiles)


SLOT = 64


def _short_copy(yg_hbm, b, e, start, packed, slot, sem):
    src = yg_hbm.at[b, e, pl.ds(pl.multiple_of(start, BF16_ROWS), SLOT), :]
    return pltpu.make_async_copy(src, packed.at[slot, pl.ds(e * SLOT, SLOT), :], sem.at[0, slot])


def _full_copy(yg_hbm, b, e, start, spill, slot, sem):
    src = yg_hbm.at[b, e, pl.ds(pl.multiple_of(start, BF16_ROWS), WINDOW), :]
    return pltpu.make_async_copy(src, spill.at[slot, e], sem.at[1, slot])


def _combine_body(tab_ref, tabn_ref, lpos_ref, x_ref, g_ref, yg_hbm, o_ref, packed, spill, sem,
                  *, nb, nt, final_norm):
    b = pl.program_id(0)
    j = pl.program_id(1)
    n = b * nt + j
    slot = lax.rem(n, 2)

    def fetch(tab, bb, jj, buf):
        for e in range(N_EXPERTS):
            _short_copy(yg_hbm, bb, e, tab[0, 0, e, jj], packed, buf, sem).start()

        @pl.when(tab[0, 3, 0, jj] > SLOT)
        def _():
            for e in range(N_EXPERTS):
                @pl.when(tab[0, 2, e, jj] > SLOT)
                def _(e=e):
                    _full_copy(yg_hbm, bb, e, tab[0, 0, e, jj], spill, buf, sem).start()

    @pl.when(n == 0)
    def _():
        packed[...] = jnp.zeros_like(packed)
        spill[...] = jnp.zeros_like(spill)
        fetch(tab_ref, b, j, slot)

    @pl.when(n + 1 < nb * nt)
    def _():
        wrap = j + 1 == nt
        fetch(tabn_ref, jnp.where(wrap, b + 1, b), jnp.where(wrap, 0, j + 1), 1 - slot)

    lm = lpos_ref[0]
    lane = lax.broadcasted_iota(i32, (LANES, LANES), 1).astype(f32)
    per_tile = LANES // SLOT
    tiles = []
    for lt in range(N_EXPERTS // per_tile):
        hit = None
        for i in range(per_tile):
            e = lt * per_tile + i
            col = jnp.where(tab_ref[0, 2, e, j] <= SLOT, lm[:, e:e + 1], -1.0)
            col = jnp.broadcast_to(jnp.where(col >= 0.0, col + float(i * SLOT), -1.0), (LANES, LANES))
            h = col == lane
            hit = h if hit is None else jnp.logical_or(hit, h)
        tiles.append(hit.astype(bf16))
    place = jnp.concatenate(tiles, axis=1)

    pltpu.make_async_copy(packed.at[1 - slot], packed.at[slot], sem.at[0, slot]).wait()
    o_ref[...] = x_ref[...] + jnp.dot(place, packed[slot], preferred_element_type=f32)

    @pl.when(tab_ref[0, 3, 0, j] > SLOT)
    def _():
        row = lax.broadcasted_iota(i32, (LANES, WINDOW), 1).astype(f32)
        for e in range(N_EXPERTS):
            @pl.when(tab_ref[0, 2, e, j] > SLOT)
            def _(e=e):
                _full_copy(yg_hbm, b, e, 0, spill, slot, sem).wait()
                own = (lm[:, e:e + 1] == row).astype(bf16)
                o_ref[...] += jnp.dot(own, spill[slot, e], preferred_element_type=f32)

    if final_norm:
        o_ref[...] = _rms(o_ref[...], g_ref[...])


def _combine(x2d, lpos, tab, yg, nb, final_g=None):
    t, d = x2d.shape
    g = jnp.ones((1, d), f32) if final_g is None else final_g.reshape(1, d)
    nt = t // nb // LANES

    def next_batch(i, j):
        return (jnp.minimum(i + (j + 1) // nt, nb - 1), 0, 0, 0)

    return pl.pallas_call(
        functools.partial(_combine_body, nb=nb, nt=nt, final_norm=final_g is not None),
        grid=(nb, nt),
        in_specs=[
            pl.BlockSpec((1, 4, N_EXPERTS, LANES), lambda i, j: (i, 0, 0, 0), memory_space=pltpu.SMEM),
            pl.BlockSpec((1, 4, N_EXPERTS, LANES), next_batch, memory_space=pltpu.SMEM),
            pl.BlockSpec((1, LANES, LANES), lambda i, j: (i, j, 0)),
            pl.BlockSpec((LANES, d), lambda i, j: (i * nt + j, 0)),
            pl.BlockSpec((1, d), lambda i, j: (0, 0)),
            pl.BlockSpec(memory_space=pl.ANY),
        ],
        out_specs=pl.BlockSpec((LANES, d), lambda i, j: (i * nt + j, 0)),
        out_shape=jax.ShapeDtypeStruct((t, d), f32),
        scratch_shapes=[pltpu.VMEM((2, N_EXPERTS * SLOT, d), bf16), pltpu.VMEM((2, N_EXPERTS, WINDOW, d), bf16),
                        pltpu.SemaphoreType.DMA((2, 2))],
        compiler_params=_params(("arbitrary", "arbitrary")),
        name="moe_combine",
    )(tab, tab, lpos, x2d, g, yg)


def _pad_lanes(v):
    return jnp.zeros((1, LANES), f32).at[0, :v.shape[0]].set(v.astype(f32))


def _mixer(x3, norm_mix, w_in, lru_conv_w, lru_conv_b, lru_wa, lru_ba, lru_wi, lru_bi, lru_lambda,
           ssd_conv_w, ssd_conv_b, ssd_a_log, ssd_dt_bias, ssd_d):
    b, s, d = x3.shape
    x2d = x3.reshape(b * s, d)
    pad = jnp.zeros((d, LANES - 2 * SSD_HEADS), bf16)
    w_bf = jnp.concatenate([w_in.astype(bf16), pad], axis=1)
    xc, gate, z, xact, dt = _in_proj(x2d, norm_mix, w_bf, lru_conv_w, lru_conv_b, ssd_conv_w, ssd_conv_b, s)
    xc = xc.reshape(b, s, LRU_WIDTH)
    xact = xact.reshape(b, s, SSD_CONV_CH)
    dt = dt.reshape(b, s, LANES)

    w_gate = jnp.concatenate([lru_wa, lru_wi], axis=-1).astype(bf16)
    hs = _lru_scan(xc, w_gate, lru_ba, lru_bi, lru_lambda)
    alog_pad = _pad_lanes(ssd_a_log.reshape(-1))
    dtb_pad = _pad_lanes(ssd_dt_bias.reshape(-1))
    dskip_pad = _pad_lanes(ssd_d)
    ys = _ssd_scan(xact, dt, alog_pad, dtb_pad, dskip_pad)
    t = b * s
    return (hs[0].reshape(t, -1), hs[1].reshape(t, -1), gate, ys[0].reshape(t, -1), ys[1].reshape(t, -1), z)


def _layer(x3, norm_mix, w_in, lru_conv_w, lru_conv_b, lru_wa, lru_ba, lru_wi, lru_bi, lru_lambda,
           ssd_conv_w, ssd_conv_b, ssd_a_log, ssd_dt_bias, ssd_d, ssd_norm, w_out, norm_ffn, w_router,
           expert_w, layer, final_g):
    b, s, d = x3.shape
    cap = max(1, CAPACITY_FACTOR * s // N_EXPERTS)
    assert s % LANES == 0 and cap % LANES == 0, "sequence length must give 128-aligned expert capacity"
    hf, hb, gate, yf, yb, z = _mixer(x3, norm_mix, w_in, lru_conv_w, lru_conv_b, lru_wa, lru_ba, lru_wi,
                                     lru_bi, lru_lambda, ssd_conv_w, ssd_conv_b, ssd_a_log, ssd_dt_bias, ssd_d)
    wr = jnp.zeros((d, LANES), f32).at[:, :N_EXPERTS].set(w_router)
    wr_hi = wr.astype(bf16)
    wr_lo = (wr - wr_hi.astype(f32)).astype(bf16)
    xn, hn, logits = _mix_out(x3.reshape(b * s, d), hf, hb, gate, yf, yb, z, ssd_norm, w_out.astype(bf16),
                              norm_ffn, wr_hi, wr_lo)
    idx, gates, lpos, tab = _route(logits.reshape(b, s, LANES), cap)
    yg = _moe_ffn(hn, idx, gates, *expert_w, layer)
    return _combine(xn, lpos, tab, yg, b, final_g).reshape(b, s, d)


def kernel(x, norm_mix, w_in, lru_conv_w, lru_conv_b, lru_wa, lru_ba, lru_wi, lru_bi, lru_lambda, ssd_conv_w, ssd_conv_b, ssd_a_log, ssd_dt_bias, ssd_d, ssd_norm, w_out, norm_ffn, w_router, w_gate, w_up, w_down, norm_final):
    depth = norm_mix.shape[0]
    expert_w = (w_gate.astype(bf16), w_up.astype(bf16), w_down.astype(bf16))
    for l in range(depth):
        x = _layer(x, norm_mix[l], w_in[l], lru_conv_w[l], lru_conv_b[l], lru_wa[l], lru_ba[l], lru_wi[l],
                   lru_bi[l], lru_lambda[l], ssd_conv_w[l], ssd_conv_b[l], ssd_a_log[l], ssd_dt_bias[l],
                   ssd_d[l], ssd_norm[l], w_out[l], norm_ffn[l], w_router[l], expert_w, l,
                   norm_final if l == depth - 1 else None)
    return x
```

```python
import functools

import jax
import jax.numpy as jnp
from jax import lax
from jax.experimental import pallas as pl
from jax.experimental.pallas import tpu as pltpu

f32 = jnp.float32
bf16 = jnp.bfloat16
i32 = jnp.int32

D_MODEL = 1024
EPS = 1e-6
CONV_WIDTH = 4
LRU_WIDTH = 1024
LRU_HEADS = 8
LRU_BLOCK = 128
LRU_C = 8.0
SSD_WIDTH = 1024
SSD_HEADDIM = 64
SSD_HEADS = 16
SSD_GROUPS = 4
SSD_STATE = 128
SSD_CHUNK = 128
SSD_GN = SSD_GROUPS * SSD_STATE
SSD_CONV_CH = SSD_WIDTH + 2 * SSD_GN
GROUP_WIDTH = SSD_WIDTH // SSD_GROUPS
N_EXPERTS = 16
CAPACITY_FACTOR = 2
D_FF = 2048

LANES = 128
SUBLANES = 8
HALO = SUBLANES
BF16_ROWS = 2 * SUBLANES
WINDOW = LANES + BF16_ROWS
VMEM_LIMIT = 56 * 1024 * 1024
MIX_ROWS = 256


def _pick(n, target):
    if n <= target:
        return n
    t = target
    while t >= LANES:
        if n % t == 0:
            return t
        t -= LANES
    return n


def _params(sem, vmem=None):
    return pltpu.CompilerParams(dimension_semantics=sem, vmem_limit_bytes=vmem)


IN_SEGMENTS = (LRU_WIDTH, LRU_WIDTH, SSD_WIDTH, SSD_CONV_CH, LANES)


def _conv_centred(pe, first, last, cw, cb):
    tm = pe.shape[0] - 2 * HALO
    before = jnp.where(first, 0.0, pe[:HALO])
    after = jnp.where(last, 0.0, pe[HALO + tm:])
    c = pe[HALO:HALO + tm]
    r8 = lax.broadcasted_iota(i32, (SUBLANES, pe.shape[1]), 0)
    back1 = pltpu.roll(c, 1, 0)
    fwd1 = pltpu.roll(c, tm - 1, 0)
    fwd2 = pltpu.roll(c, tm - 2, 0)
    back1 = jnp.concatenate([jnp.where(r8 == 0, before[HALO - 1:HALO], back1[:HALO]), back1[HALO:]], axis=0)
    fwd1 = jnp.concatenate([fwd1[:tm - HALO], jnp.where(r8 == HALO - 1, after[0:1], fwd1[tm - HALO:])], axis=0)
    tail2 = jnp.where(r8 == HALO - 2, after[0:1], jnp.where(r8 == HALO - 1, after[1:2], fwd2[tm - HALO:]))
    fwd2 = jnp.concatenate([fwd2[:tm - HALO], tail2], axis=0)
    return cb + back1 * cw[0:1] + c * cw[1:2] + fwd1 * cw[2:3] + fwd2 * cw[3:4]


def _in_proj_body(prev_ref, x_ref, next_ref, g_ref, w_ref, lcw_ref, lcb_ref, scw_ref, scb_ref,
                  xc_ref, gate_ref, z_ref, xa_ref, dt_ref, *, tiles_per_seq):
    pos = lax.rem(pl.program_id(0), tiles_per_seq)
    first = pos == 0
    last = pos == tiles_per_seq - 1
    tm = x_ref.shape[0]
    xe = jnp.concatenate([prev_ref[...], x_ref[...], next_ref[...]], axis=0)
    ms = jnp.mean(xe * xe, axis=-1, keepdims=True)
    hn = (xe * lax.rsqrt(ms + EPS) * g_ref[...]).astype(bf16)
    hc = hn[HALO:HALO + tm]
    o0, o1, o2, o3 = LRU_WIDTH, 2 * LRU_WIDTH, 2 * LRU_WIDTH + SSD_WIDTH, 2 * LRU_WIDTH + SSD_WIDTH + SSD_CONV_CH
    pe = jnp.dot(hn, w_ref[:, :o0], preferred_element_type=f32)
    xc_ref[...] = _conv_centred(pe, first, last, lcw_ref[...], lcb_ref[...]).astype(bf16)
    gate_ref[...] = jnp.dot(hc, w_ref[:, o0:o1], preferred_element_type=f32).astype(bf16)
    z_ref[...] = jnp.dot(hc, w_ref[:, o1:o2], preferred_element_type=f32).astype(bf16)
    pe = jnp.dot(hn, w_ref[:, o2:o3], preferred_element_type=f32)
    xa_ref[...] = jax.nn.silu(_conv_centred(pe, first, last, scw_ref[...], scb_ref[...])).astype(bf16)
    dt_ref[...] = jnp.dot(hc, w_ref[:, o3:], preferred_element_type=f32)


def _in_proj(x2d, g, w_bf16, lru_cw, lru_cb, ssd_cw, ssd_cb, seq):
    t, d = x2d.shape
    n = w_bf16.shape[1]
    tm = _pick(seq, 512)
    per = tm // HALO
    last_blk = t // HALO - 1
    full = lambda shape: pl.BlockSpec(shape, lambda i: (0,) * len(shape))
    return pl.pallas_call(
        functools.partial(_in_proj_body, tiles_per_seq=seq // tm),
        grid=(t // tm,),
        in_specs=[
            pl.BlockSpec((HALO, d), lambda i: (jnp.maximum(i * per - 1, 0), 0)),
            pl.BlockSpec((tm, d), lambda i: (i, 0)),
            pl.BlockSpec((HALO, d), lambda i: (jnp.minimum((i + 1) * per, last_blk), 0)),
            full((1, d)),
            pl.BlockSpec((d, n), lambda i: (0, 0), pipeline_mode=pl.Buffered(1)),
            full((CONV_WIDTH, LRU_WIDTH)), full((1, LRU_WIDTH)),
            full((CONV_WIDTH, SSD_CONV_CH)), full((1, SSD_CONV_CH)),
        ],
        out_specs=[pl.BlockSpec((tm, w), lambda i: (i, 0)) for w in IN_SEGMENTS],
        out_shape=[jax.ShapeDtypeStruct((t, w), dt) for w, dt in zip(IN_SEGMENTS, (bf16, bf16, bf16, bf16, f32))],
        compiler_params=_params(("parallel",), VMEM_LIMIT),
        name="in_proj",
    )(x2d, x2d, x2d, g.reshape(1, d), w_bf16, lru_cw, lru_cb.reshape(1, -1), ssd_cw, ssd_cb.reshape(1, -1))


def _lru_gates(x_ref, perm_ref, w_ref, ba_ref, bi_ref, lam_ref, a_scr, u_scr, d):
    xc = jnp.dot(perm_ref[0], x_ref[0], preferred_element_type=f32).astype(bf16)
    sp = jax.nn.softplus(-lam_ref[d:d + 1, :])
    for h in range(LRU_HEADS):
        sl = slice(h * LRU_BLOCK, (h + 1) * LRU_BLOCK)
        pre = jnp.dot(xc[:, sl], w_ref[d, h], preferred_element_type=f32)
        xh = xc[:, sl].astype(f32)
        r = jax.nn.sigmoid(pre[:, :LRU_BLOCK] + ba_ref[d:d + 1, sl])
        gi = jax.nn.sigmoid(pre[:, LRU_BLOCK:] + bi_ref[d:d + 1, sl])
        log_a = (-LRU_C) * r * sp[:, sl]
        a = jnp.exp(log_a)
        u = jnp.sqrt(1.0 - a * a) * (gi * xh)
        a_scr[d, h] = a
        u_scr[d, h] = u


def _lru_direction(d, perm_ref, o_ref, carry_ref, a_scr, u_scr):
    tc = o_ref.shape[1]
    sub = tc // SUBLANES
    heads = range(LRU_HEADS)

    def block(j):
        jj = (sub - 1 - j) if d else j
        return pl.ds(pl.multiple_of(jj * SUBLANES, SUBLANES), SUBLANES)

    def local(j, state):
        rows = block(j)
        out = []
        for h in heads:
            a = a_scr[d, h, rows, :]
            out += [a * state[2 * h] + u_scr[d, h, rows, :], a * state[2 * h + 1]]
        return tuple(out)

    init = []
    for h in heads:
        init += [jnp.zeros((SUBLANES, LRU_BLOCK), f32), jnp.ones((SUBLANES, LRU_BLOCK), f32)]
    ends = lax.fori_loop(0, sub, local, tuple(init), unroll=True)

    sub_i = lax.broadcasted_iota(i32, (SUBLANES, LRU_BLOCK), 0)
    carries = []
    for h in heads:
        sl = slice(h * LRU_BLOCK, (h + 1) * LRU_BLOCK)
        c = carry_ref[d:d + 1, sl]
        cin = jnp.zeros((SUBLANES, LRU_BLOCK), f32)
        for s in (reversed(range(SUBLANES)) if d else range(SUBLANES)):
            cin = jnp.where(sub_i == s, c, cin)
            c = ends[2 * h][s:s + 1] + ends[2 * h + 1][s:s + 1] * c
        carry_ref[d:d + 1, sl] = c
        carries.append(cin)

    def final(j, state):
        rows = block(j)
        out = []
        for h in heads:
            hcur = a_scr[d, h, rows, :] * state[h] + u_scr[d, h, rows, :]
            u_scr[d, h, rows, :] = hcur
            out.append(hcur)
        return tuple(out)

    lax.fori_loop(0, sub, final, tuple(carries), unroll=True)

    hp = jnp.concatenate([u_scr[d, h].astype(bf16) for h in heads], axis=1)
    o_ref[0] = jnp.dot(perm_ref[1], hp, preferred_element_type=f32).astype(bf16)


def _lru_body(xf_ref, xb_ref, perm_ref, w_ref, ba_ref, bi_ref, lam_ref, of_ref, ob_ref, carry_ref, a_scr, u_scr):
    @pl.when(pl.program_id(1) == 0)
    def _():
        carry_ref[...] = jnp.zeros_like(carry_ref)

    _lru_gates(xf_ref, perm_ref, w_ref, ba_ref, bi_ref, lam_ref, a_scr, u_scr, 0)
    _lru_gates(xb_ref, perm_ref, w_ref, ba_ref, bi_ref, lam_ref, a_scr, u_scr, 1)
    _lru_direction(0, perm_ref, of_ref, carry_ref, a_scr, u_scr)
    _lru_direction(1, perm_ref, ob_ref, carry_ref, a_scr, u_scr)


def _lru_scan(xc, w_gate, ba, bi, lam):
    b, s, w = xc.shape
    tc = _pick(s, 256)
    nc = s // tc
    full = lambda shape: pl.BlockSpec(shape, lambda b_, c_: (0,) * len(shape))
    fwd = pl.BlockSpec((1, tc, w), lambda b_, c_: (b_, c_, 0))
    bwd = pl.BlockSpec((1, tc, w), lambda b_, c_: (b_, nc - 1 - c_, 0))
    p = jnp.arange(tc)
    to_strided = ((p % SUBLANES) * (tc // SUBLANES) + p // SUBLANES)[:, None] == jnp.arange(tc)[None, :]
    perm = jnp.stack([to_strided, to_strided.T]).astype(bf16)
    return pl.pallas_call(
        _lru_body,
        grid=(b, nc),
        in_specs=[fwd, bwd, full((2, tc, tc)), full((2, LRU_HEADS, LRU_BLOCK, 2 * LRU_BLOCK)),
                  full((2, w)), full((2, w)), full((2, w))],
        out_specs=[fwd, bwd],
        out_shape=[jax.ShapeDtypeStruct((b, s, w), bf16)] * 2,
        scratch_shapes=[pltpu.VMEM((2, w), f32), pltpu.VMEM((2, LRU_HEADS, tc, LRU_BLOCK), f32),
                        pltpu.VMEM((2, LRU_HEADS, tc, LRU_BLOCK), f32)],
        compiler_params=_params(("parallel", "arbitrary")),
        name="lru_scan",
    )(xc, xc, perm, w_gate, ba, bi, lam)


def _expand_heads(arr, base):
    rows = arr.shape[0]
    lane = lax.broadcasted_iota(i32, (rows, LANES), 1)
    tiles = []
    for k in range(SSD_HEADS // 2):
        c0 = arr[:, base + 2 * k:base + 2 * k + 1]
        c1 = arr[:, base + 2 * k + 1:base + 2 * k + 2]
        tiles.append(jnp.where(lane < SSD_HEADDIM, c0, c1))
    return jnp.concatenate(tiles, axis=1)


def _cumsum_rows(x, reverse):
    n = x.shape[0]
    r = lax.broadcasted_iota(i32, x.shape, 0)
    k = 1
    while k < n:
        if reverse:
            x = x + jnp.where(r < n - k, pltpu.roll(x, n - k, 0), 0.0)
        else:
            x = x + jnp.where(r >= k, pltpu.roll(x, k, 0), 0.0)
        k *= 2
    return x


def _ssd_body(xf_ref, xb_ref, dtf_ref, dtb_in_ref, alog_ref, dtb_ref, dskip_ref, sele_ref, selc_ref,
              of_ref, ob_ref, state_ref):
    @pl.when(pl.program_id(1) == 0)
    def _():
        state_ref[...] = jnp.zeros_like(state_ref)

    per_step = xf_ref.shape[1] // SSD_CHUNK
    for k in range(per_step):
        rows_f = slice(k * SSD_CHUNK, (k + 1) * SSD_CHUNK)
        rows_b = slice((per_step - 1 - k) * SSD_CHUNK, (per_step - k) * SSD_CHUNK)
        _ssd_dir(xf_ref, dtf_ref, rows_f, alog_ref, dtb_ref, dskip_ref, sele_ref, selc_ref, of_ref,
                 state_ref.at[0], False)
        _ssd_dir(xb_ref, dtb_in_ref, rows_b, alog_ref, dtb_ref, dskip_ref, sele_ref, selc_ref, ob_ref,
                 state_ref.at[1], True)


def _split_bf16(x, parts):
    out = []
    for _ in range(parts):
        p = x.astype(bf16)
        out.append(p)
        x = x - p.astype(f32)
    return jnp.concatenate(out, axis=1)


def _ssd_selectors(chunk):
    lane = jnp.arange(LANES)[:, None]
    spread, cols = [], []
    for d in range(2):
        head = lane - d * SSD_HEADS
        s = (head == jnp.arange(SSD_WIDTH)[None, :] // SSD_HEADDIM).astype(bf16)
        c = (head == jnp.arange(SSD_HEADS * chunk)[None, :] // chunk).astype(bf16)
        spread.append(jnp.concatenate([s, s], axis=0))
        cols.append(jnp.concatenate([c, c], axis=0))
    return jnp.stack(spread), jnp.stack(cols)


def _ssd_dir(x_ref, dt_ref, rows, alog_ref, dtb_ref, dskip_ref, sele_ref, selc_ref, o_ref, state_ref, reverse):
    L = SSD_CHUNK
    base = SSD_HEADS if reverse else 0
    xact = x_ref[0, rows, :]
    xs = xact[:, :SSD_WIDTH].astype(f32)

    dt = jax.nn.softplus(dt_ref[0, rows, :] + dtb_ref[...])
    d_a = dt * (-jnp.exp(alog_ref[...]))
    cum = _cumsum_rows(d_a, reverse)
    edge = cum[0:1] if reverse else cum[L - 1:L]
    cum_t = cum.T

    d = 1 if reverse else 0
    stack = jnp.concatenate([_split_bf16(dt, 2), _split_bf16(jnp.exp(cum), 2), _split_bf16(jnp.exp(edge - cum), 2)],
                            axis=0)
    spread = jnp.dot(stack, sele_ref[d], preferred_element_type=f32)
    dtx = spread[:L] * xs
    e_cum = spread[L:2 * L]
    e_end = spread[2 * L:]
    cum_cols = jnp.dot(_split_bf16(cum, 2), selc_ref[d], preferred_element_type=f32)
    e_edge = _expand_heads(jnp.exp(edge), base)
    w_all = (e_end * dtx).astype(bf16)
    dtx_b = dtx.astype(bf16)

    li = lax.broadcasted_iota(i32, (L, L), 0)
    si = lax.broadcasted_iota(i32, (L, L), 1)
    tri = (si >= li) if reverse else (li >= si)
    lane = lax.broadcasted_iota(i32, (L, LANES), 1)
    lo_half = lane < SSD_HEADDIM

    outs = []
    for g in range(SSD_GROUPS):
        bsl = slice(SSD_WIDTH + g * SSD_STATE, SSD_WIDTH + (g + 1) * SSD_STATE)
        csl = slice(SSD_WIDTH + SSD_GN + g * SSD_STATE, SSD_WIDTH + SSD_GN + (g + 1) * SSD_STATE)
        gsl = slice(g * GROUP_WIDTH, (g + 1) * GROUP_WIDTH)
        bm = xact[:, bsl]
        cm = xact[:, csl]
        cb = lax.dot_general(cm, bm, (((1,), (1,)), ((), ())), preferred_element_type=f32)
        s_old = state_ref[g]
        y_off = jnp.dot(cm, s_old.astype(bf16), preferred_element_type=f32) * e_cum[:, gsl]
        tiles = []
        for p in range(2):
            tsl = slice(g * GROUP_WIDTH + p * LANES, g * GROUP_WIDTH + (p + 1) * LANES)
            ms, rhs = [], []
            for q in range(2):
                h = g * 4 + 2 * p + q
                j = base + h
                seg = cum_cols[:, h * L:(h + 1) * L] - cum_t[j:j + 1, :]
                decay = jnp.exp(jnp.where(tri, seg, -jnp.inf))
                ms.append((cb * decay).astype(bf16))
                rhs.append(jnp.where(lo_half if q == 0 else jnp.logical_not(lo_half), dtx_b[:, tsl], 0.0))
            tiles.append(jnp.dot(jnp.concatenate(ms, axis=1), jnp.concatenate(rhs, axis=0).astype(bf16),
                                 preferred_element_type=f32))
        outs.append(jnp.concatenate(tiles, axis=1) + y_off)
        upd = lax.dot_general(bm, w_all[:, gsl], (((0,), (0,)), ((), ())), preferred_element_type=f32)
        state_ref[g] = s_old * e_edge[:, gsl] + upd
    y = jnp.concatenate(outs, axis=1)
    if not reverse:
        y = y + _expand_heads(dskip_ref[...], 0) * xs
    o_ref[0, rows, :] = y.astype(bf16)


def _ssd_scan(xact, dt_pad, alog_pad, dtb_pad, dskip_pad):
    b, s, w = xact.shape
    L = _pick(s, 4 * SSD_CHUNK)
    assert L % SSD_CHUNK == 0
    nc = s // L
    full = lambda shape: pl.BlockSpec(shape, lambda b_, c_: (0,) * len(shape))
    fwd = lambda width: pl.BlockSpec((1, L, width), lambda b_, c_: (b_, c_, 0))
    bwd = lambda width: pl.BlockSpec((1, L, width), lambda b_, c_: (b_, nc - 1 - c_, 0))
    spread, cols = _ssd_selectors(SSD_CHUNK)
    return pl.pallas_call(
        _ssd_body,
        grid=(b, nc),
        in_specs=[fwd(w), bwd(w), fwd(LANES), bwd(LANES), full((1, LANES)), full((1, LANES)), full((1, LANES)),
                  full(spread.shape), full(cols.shape)],
        out_specs=[fwd(SSD_WIDTH), bwd(SSD_WIDTH)],
        out_shape=[jax.ShapeDtypeStruct((b, s, SSD_WIDTH), bf16)] * 2,
        scratch_shapes=[pltpu.VMEM((2, SSD_GROUPS, SSD_STATE, GROUP_WIDTH), f32)],
        compiler_params=_params(("parallel", "arbitrary")),
        name="ssd_scan",
    )(xact, xact, dt_pad, dt_pad, alog_pad, dtb_pad, dskip_pad, spread, cols)


def _rms(x, g):
    ms = jnp.mean(x * x, axis=-1, keepdims=True)
    return x * lax.rsqrt(ms + EPS) * g


def _mix_out_body(x_ref, hf_ref, hb_ref, gate_ref, yf_ref, yb_ref, z_ref, gn_ref, wo_ref,
                  gf_ref, wrh_ref, wrl_ref, xo_ref, hn_ref, lg_ref):
    sub = min(x_ref.shape[0], MIX_ROWS)
    for i in range(x_ref.shape[0] // sub):
        rows = slice(i * sub, (i + 1) * sub)
        up = lambda ref: ref[rows, :].astype(f32)
        y_lru = (up(hf_ref) + up(hb_ref)) * jax.nn.gelu(up(gate_ref))
        y = (up(yf_ref) + up(yb_ref)) * jax.nn.silu(up(z_ref))
        parts = []
        for g in range(SSD_GROUPS):
            yg = y[:, g * GROUP_WIDTH:(g + 1) * GROUP_WIDTH]
            ms = jnp.mean(yg * yg, axis=-1, keepdims=True)
            parts.append(yg * lax.rsqrt(ms + EPS))
        y_ssd = jnp.concatenate(parts, axis=1) * gn_ref[...]
        mix = jnp.concatenate([y_lru, y_ssd], axis=1).astype(bf16)
        xn = x_ref[rows, :] + jnp.dot(mix, wo_ref[...], preferred_element_type=f32)
        xo_ref[rows, :] = xn
        hn = _rms(xn, gf_ref[...])
        for k in range(hn.shape[1] // LANES):
            hn_ref[pl.ds(i * sub * SUBLANES + k, sub, stride=SUBLANES), :] = hn[:, k * LANES:(k + 1) * LANES]
        h_hi = hn.astype(bf16)
        h_lo = (hn - h_hi.astype(f32)).astype(bf16)
        lg = jnp.dot(h_hi, wrh_ref[...], preferred_element_type=f32)
        lg = lg + jnp.dot(h_lo, wrh_ref[...], preferred_element_type=f32)
        lg = lg + jnp.dot(h_hi, wrl_ref[...], preferred_element_type=f32)
        lg_ref[rows, :] = lg


def _mix_out(x2d, hf, hb, gate, yf, yb, z, ssd_norm, w_out_bf16, norm_ffn, wr_hi, wr_lo):
    t, d = x2d.shape
    tm = _pick(t, 2 * MIX_ROWS)
    row = lambda w: pl.BlockSpec((tm, w), lambda i: (i, 0))
    full = lambda shape: pl.BlockSpec(shape, lambda i: (0,) * len(shape))
    return pl.pallas_call(
        _mix_out_body,
        grid=(t // tm,),
        in_specs=[row(d)] + [row(LRU_WIDTH)] * 3 + [row(SSD_WIDTH)] * 3 + [
            full((1, SSD_WIDTH)), full((LRU_WIDTH + SSD_WIDTH, d)), full((1, d)),
            full((d, LANES)), full((d, LANES)),
        ],
        out_specs=[row(d), pl.BlockSpec((tm * d // LANES, LANES), lambda i: (i, 0)), row(LANES)],
        out_shape=[jax.ShapeDtypeStruct((t, d), f32), jax.ShapeDtypeStruct((t * d // LANES, LANES), f32),
                   jax.ShapeDtypeStruct((t, LANES), f32)],
        compiler_params=_params(("parallel",), VMEM_LIMIT),
        name="mix_out",
    )(x2d, hf, hb, gate, yf, yb, z, ssd_norm.reshape(1, -1), w_out_bf16, norm_ffn.reshape(1, d),
      wr_hi, wr_lo)


def _tile_prefix(tiles, upper_incl, upper_strict, lane):
    incs = [jnp.dot(t.astype(bf16), upper_incl, preferred_element_type=f32) for t in tiles]
    tot = jnp.zeros((N_EXPERTS, LANES), f32)
    for j, inc in enumerate(incs):
        tot = jnp.where(lane == j, inc[:, LANES - 1:LANES], tot)
    start = jnp.dot(tot.astype(bf16), upper_strict, preferred_element_type=f32)
    return incs, tot, start


def _route_body(lg_ref, idx_ref, gate_ref, lpos_ref, tab_ref, lm_scr, vt_scr, list_scr, tabv_scr, tabs_scr, sem,
                *, cap):
    s = lg_ref.shape[1]
    nt = s // LANES
    lt = lg_ref[0].T[:N_EXPERTS]
    mx = jnp.max(lt, axis=0, keepdims=True)
    ex = jnp.exp(lt - mx)
    aff = ex / jnp.sum(ex, axis=0, keepdims=True)
    key = pltpu.bitcast(aff, i32)

    def search(i, thr):
        cand = thr | (jnp.int32(1) << (30 - i))
        cnt = jnp.sum((key >= cand).astype(f32), axis=1, keepdims=True)
        return jnp.where(cnt >= float(cap), cand, thr)

    thr = lax.fori_loop(0, 31, search, jnp.zeros((N_EXPERTS, 1), i32))
    gt = key > thr
    eq = key == thr
    need = float(cap) - jnp.sum(gt.astype(f32), axis=1, keepdims=True)

    sub_i = lax.broadcasted_iota(i32, (LANES, LANES), 0)
    lane_i = lax.broadcasted_iota(i32, (LANES, LANES), 1)
    upper_incl = (sub_i <= lane_i).astype(bf16)
    upper_strict = (sub_i < lane_i).astype(bf16)
    lane_e = lax.broadcasted_iota(i32, (N_EXPERTS, LANES), 1)
    tiles = lambda a: [a[:, j * LANES:(j + 1) * LANES] for j in range(nt)]

    eq_t = tiles(eq.astype(f32))
    incs, _, start = _tile_prefix(eq_t, upper_incl, upper_strict, lane_e)
    sel_t = []
    for j, (gtj, eqj) in enumerate(zip(tiles(gt), eq_t)):
        excl = incs[j] - eqj + start[:, j:j + 1]
        sel_t.append(jnp.logical_or(gtj, jnp.logical_and(eqj > 0.5, excl < need)).astype(f32))

    incs, tot, start = _tile_prefix(sel_t, upper_incl, upper_strict, lane_e)
    start8 = jnp.floor(start * (1.0 / BF16_ROWS)) * float(BF16_ROWS)
    tab_ref[0, 0] = start8.astype(i32)
    tab_ref[0, 1] = tot.astype(i32)
    rows_needed = start - start8 + tot
    tab_ref[0, 2] = rows_needed.astype(i32)
    tab_ref[0, 3] = jnp.broadcast_to(jnp.max(rows_needed, axis=0, keepdims=True), rows_needed.shape).astype(i32)
    tabv_scr[...] = start.astype(i32)
    to_smem = pltpu.make_async_copy(tabv_scr, tabs_scr, sem)
    to_smem.start()

    kind = lax.broadcasted_iota(i32, (SUBLANES, LANES), 0)
    tok_lane = lax.broadcasted_iota(i32, (SUBLANES, LANES), 1).astype(f32)
    fill = jnp.full((LANES - N_EXPERTS, LANES), -1.0, f32)
    zrows = jnp.zeros((LANES - 3 * N_EXPERTS - SUBLANES, LANES), f32)
    for j, a in enumerate(tiles(aff)):
        lm = jnp.where(sel_t[j] > 0.5, incs[j] - sel_t[j], -1.0)
        lm_scr[j] = lm
        shifted = jnp.where(sel_t[j] > 0.5, lm + (start[:, j:j + 1] - start8[:, j:j + 1]), -1.0)
        lpos_ref[0, pl.ds(j * LANES, LANES), :] = jnp.concatenate([shifted, fill], axis=0).T
        a_hi = a.astype(bf16).astype(f32)
        a_mid = (a - a_hi).astype(bf16).astype(f32)
        a_lo = (a - a_hi - a_mid).astype(bf16).astype(f32)
        tok = jnp.where(kind == 0, float(j), jnp.where(kind == 1, tok_lane, 0.0))
        vt_scr[j] = jnp.concatenate([a_hi, a_mid, a_lo, tok, zrows], axis=0).T.astype(bf16)

    to_smem.wait()
    rank = sub_i.astype(f32)

    def compact(j, carry):
        lm = lm_scr[j]
        vt = vt_scr[j]
        for e in range(N_EXPERTS):
            onehot = (lm[e:e + 1] == rank).astype(bf16)
            packed = jnp.dot(onehot, vt, preferred_element_type=f32)
            list_scr[e, pl.ds(tabs_scr[e, j], LANES), :] = packed
        return carry

    lax.fori_loop(0, nt, compact, 0)

    lane_c = lax.broadcasted_iota(i32, (cap, LANES), 1)
    idx_c = jnp.zeros((cap, LANES), f32)
    gate_c = jnp.zeros((cap, LANES), f32)
    for e in range(N_EXPERTS):
        rows = list_scr[e, 0:cap, :]
        g = rows[:, e:e + 1] + rows[:, N_EXPERTS + e:N_EXPERTS + e + 1] + rows[:, 2 * N_EXPERTS + e:2 * N_EXPERTS + e + 1]
        t = rows[:, 3 * N_EXPERTS:3 * N_EXPERTS + 1] * float(LANES) + rows[:, 3 * N_EXPERTS + 1:3 * N_EXPERTS + 2]
        idx_c = jnp.where(lane_c == e, t, idx_c)
        gate_c = jnp.where(lane_c == e, g, gate_c)
    idx_ref[0] = idx_c.T[:N_EXPERTS].astype(i32)
    gate_ref[0] = gate_c.T[:N_EXPERTS]


def _route(logits, cap):
    b, s, _ = logits.shape
    nt = s // LANES
    assert nt <= LANES
    return pl.pallas_call(
        functools.partial(_route_body, cap=cap),
        grid=(b,),
        in_specs=[pl.BlockSpec((1, s, LANES), lambda i: (i, 0, 0))],
        out_specs=[pl.BlockSpec((1, N_EXPERTS, cap), lambda i: (i, 0, 0)),
                   pl.BlockSpec((1, N_EXPERTS, cap), lambda i: (i, 0, 0)),
                   pl.BlockSpec((1, s, LANES), lambda i: (i, 0, 0)),
                   pl.BlockSpec((1, 4, N_EXPERTS, LANES), lambda i: (i, 0, 0, 0))],
        out_shape=[jax.ShapeDtypeStruct((b, N_EXPERTS, cap), i32),
                   jax.ShapeDtypeStruct((b, N_EXPERTS, cap), f32),
                   jax.ShapeDtypeStruct((b, s, LANES), f32),
                   jax.ShapeDtypeStruct((b, 4, N_EXPERTS, LANES), i32)],
        scratch_shapes=[pltpu.VMEM((nt, N_EXPERTS, LANES), f32),
                        pltpu.VMEM((nt, LANES, LANES), bf16),
                        pltpu.VMEM((N_EXPERTS, cap + LANES, LANES), f32),
                        pltpu.VMEM((N_EXPERTS, LANES), i32),
                        pltpu.SMEM((N_EXPERTS, LANES), i32),
                        pltpu.SemaphoreType.DMA],
        compiler_params=_params(("parallel",), VMEM_LIMIT),
        name="route",
    )(logits)


def _token_copy(src_hbm, tok, dst, r, sem):
    src = src_hbm.at[pl.ds(pl.multiple_of(tok * SUBLANES, SUBLANES), SUBLANES), :]
    return pltpu.make_async_copy(src, dst.at[pl.ds(pl.multiple_of(r * SUBLANES, SUBLANES), SUBLANES), :], sem)


def _moe_body(idx_ref, idxn_ref, gate_ref, wg_ref, wu_ref, wd_ref, hn_hbm, o_ref, xg, sem, *, cap, seq, nb):
    n = pl.program_id(0) * nb + pl.program_id(1)
    total = N_EXPERTS * nb
    slot = lax.rem(n, 2)
    other = 1 - slot

    @pl.when(n == 0)
    def _():
        def start(r, carry):
            _token_copy(hn_hbm, idx_ref[0, 0, 0, r], xg.at[slot], r, sem.at[slot]).start()
            return carry

        lax.fori_loop(0, cap, start, 0, unroll=8)

    def wait_all(buf):
        pltpu.make_async_copy(hn_hbm.at[pl.ds(0, cap * SUBLANES), :], xg.at[buf], sem.at[buf]).wait()

    wait_all(slot)

    base_next = lax.rem(jnp.minimum(n + 1, total - 1), nb) * seq
    tm = min(cap, 256)
    ntile = wg_ref.shape[1] // LANES
    for m in range(cap // tm):
        rows = slice(m * tm, (m + 1) * tm)
        for r in range(m * tm, (m + 1) * tm):
            _token_copy(hn_hbm, base_next + idxn_ref[0, 0, 0, r], xg.at[other], r, sem.at[other]).start()
        xm = jnp.concatenate(
            [xg[slot, pl.ds(m * tm * SUBLANES + k, tm, stride=SUBLANES), :] for k in range(ntile)],
            axis=1).astype(bf16)
        hg = jnp.dot(xm, wg_ref[0], preferred_element_type=f32)
        hu = jnp.dot(xm, wu_ref[0], preferred_element_type=f32)
        hid = (jax.nn.silu(hg) * hu).astype(bf16)
        y = jnp.dot(hid, wd_ref[0], preferred_element_type=f32) * gate_ref[0, 0, rows, :]
        o_ref[0, 0, rows, :] = y.astype(bf16)
    o_ref[0, 0, cap:cap + WINDOW, :] = jnp.zeros((WINDOW, o_ref.shape[-1]), bf16)

    @pl.when(n == total - 1)
    def _():
        wait_all(other)


def _moe_ffn(hn_tiles, idx, gates, wg, wu, wd, layer):
    b, _, cap = idx.shape
    d = wg.shape[2]
    s = hn_tiles.shape[0] * LANES // d // b
    idx4 = idx.reshape(b, N_EXPERTS, 1, cap)
    gates4 = gates.reshape(b, N_EXPERTS, cap, 1)
    def next_step(e, i):
        n1 = jnp.minimum(e * b + i + 1, N_EXPERTS * b - 1)
        return (lax.rem(n1, b), n1 // b, 0, 0)

    return pl.pallas_call(
        functools.partial(_moe_body, cap=cap, seq=s, nb=b),
        grid=(N_EXPERTS, b),
        in_specs=[
            pl.BlockSpec((1, 1, 1, cap), lambda e, i: (i, e, 0, 0), memory_space=pltpu.SMEM),
            pl.BlockSpec((1, 1, 1, cap), next_step, memory_space=pltpu.SMEM),
            pl.BlockSpec((1, 1, cap, 1), lambda e, i: (i, e, 0, 0)),
            pl.BlockSpec((None, 1, d, D_FF), lambda e, i: (layer, e, 0, 0)),
            pl.BlockSpec((None, 1, d, D_FF), lambda e, i: (layer, e, 0, 0)),
            pl.BlockSpec((None, 1, D_FF, d), lambda e, i: (layer, e, 0, 0)),
            pl.BlockSpec(memory_space=pl.ANY),
        ],
        out_specs=pl.BlockSpec((1, 1, cap + WINDOW, d), lambda e, i: (i, e, 0, 0)),
        out_shape=jax.ShapeDtypeStruct((b, N_EXPERTS, cap + WINDOW, d), bf16),
        scratch_shapes=[pltpu.VMEM((2, cap * d // LANES, LANES), f32), pltpu.SemaphoreType.DMA((2,))],
        compiler_params=_params(("arbitrary", "arbitrary"), VMEM_LIMIT),
        name="moe_ffn",
    )(idx4, idx4, gates4, wg, wu, wd, hn_tiles)


SLOT = 64


COMBINE_TILES = 2
PACKED_ROWS = N_EXPERTS * SLOT


def _short_copy(yg_hbm, b, e, start, packed, slot, t, sem):
    src = yg_hbm.at[b, e, pl.ds(pl.multiple_of(start, BF16_ROWS), SLOT), :]
    return pltpu.make_async_copy(src, packed.at[slot, pl.ds(t * PACKED_ROWS + e * SLOT, SLOT), :], sem.at[slot, 0])


def _full_copy(yg_hbm, b, e, start, spill, slot, t, sem):
    src = yg_hbm.at[b, e, pl.ds(pl.multiple_of(start, BF16_ROWS), WINDOW), :]
    k = t * N_EXPERTS + e
    return pltpu.make_async_copy(src, spill.at[slot, k], sem.at[slot, 1 + k])


def _combine_body(tab_ref, tabn_ref, lpos_ref, x_ref, g_ref, yg_hbm, o_ref, packed, spill, sem,
                  *, nb, nt, final_norm):
    b = pl.program_id(0)
    j = pl.program_id(1)
    n = b * nt + j
    slot = lax.rem(n, 2)
    tiles_here = x_ref.shape[0] // LANES

    def fetch(tab, bb, jj, buf):
        for t in range(tiles_here):
            tile = jj * tiles_here + t
            for e in range(N_EXPERTS):
                _short_copy(yg_hbm, bb, e, tab[0, 0, e, tile], packed, buf, t, sem).start()

            @pl.when(tab[0, 3, 0, tile] > SLOT)
            def _(t=t, tile=tile):
                for e in range(N_EXPERTS):
                    @pl.when(tab[0, 2, e, tile] > SLOT)
                    def _(e=e):
                        _full_copy(yg_hbm, bb, e, tab[0, 0, e, tile], spill, buf, t, sem).start()

    @pl.when(n == 0)
    def _():
        packed[...] = jnp.zeros_like(packed)
        spill[...] = jnp.zeros_like(spill)
        fetch(tab_ref, b, j, slot)

    @pl.when(n + 1 < nb * nt)
    def _():
        wrap = j + 1 == nt
        fetch(tabn_ref, jnp.where(wrap, b + 1, b), jnp.where(wrap, 0, j + 1), 1 - slot)

    pltpu.make_async_copy(packed.at[1 - slot], packed.at[slot], sem.at[slot, 0]).wait()

    lane = lax.broadcasted_iota(i32, (LANES, LANES), 1).astype(f32)
    per_lane_tile = LANES // SLOT
    for t in range(tiles_here):
        tile = j * tiles_here + t
        rows = slice(t * LANES, (t + 1) * LANES)
        lm = lpos_ref[0, rows, :]
        pieces = []
        for lt in range(N_EXPERTS // per_lane_tile):
            hit = None
            for i in range(per_lane_tile):
                e = lt * per_lane_tile + i
                col = jnp.where(tab_ref[0, 2, e, tile] <= SLOT, lm[:, e:e + 1], -1.0)
                col = jnp.broadcast_to(jnp.where(col >= 0.0, col + float(i * SLOT), -1.0), (LANES, LANES))
                h = col == lane
                hit = h if hit is None else jnp.logical_or(hit, h)
            pieces.append(hit.astype(bf16))
        place = jnp.concatenate(pieces, axis=1)
        window = packed[slot, t * PACKED_ROWS:(t + 1) * PACKED_ROWS, :]
        o_ref[rows, :] = x_ref[rows, :] + jnp.dot(place, window, preferred_element_type=f32)

        @pl.when(tab_ref[0, 3, 0, tile] > SLOT)
        def _(t=t, tile=tile, rows=rows, lm=lm):
            row = lax.broadcasted_iota(i32, (LANES, WINDOW), 1).astype(f32)
            for e in range(N_EXPERTS):
                @pl.when(tab_ref[0, 2, e, tile] > SLOT)
                def _(e=e):
                    _full_copy(yg_hbm, b, e, 0, spill, slot, t, sem).wait()
                    own = (lm[:, e:e + 1] == row).astype(bf16)
                    o_ref[rows, :] += jnp.dot(own, spill[slot, t * N_EXPERTS + e], preferred_element_type=f32)

    if final_norm:
        o_ref[...] = _rms(o_ref[...], g_ref[...])


def _combine(x2d, lpos, tab, yg, nb, final_g=None):
    t, d = x2d.shape
    g = jnp.ones((1, d), f32) if final_g is None else final_g.reshape(1, d)
    tiles = COMBINE_TILES if (t // nb // LANES) % COMBINE_TILES == 0 else 1
    rows = tiles * LANES
    nt = t // nb // rows

    def next_batch(i, j):
        return (jnp.minimum(i + (j + 1) // nt, nb - 1), 0, 0, 0)

    return pl.pallas_call(
        functools.partial(_combine_body, nb=nb, nt=nt, final_norm=final_g is not None),
        grid=(nb, nt),
        in_specs=[
            pl.BlockSpec((1, 4, N_EXPERTS, LANES), lambda i, j: (i, 0, 0, 0), memory_space=pltpu.SMEM),
            pl.BlockSpec((1, 4, N_EXPERTS, LANES), next_batch, memory_space=pltpu.SMEM),
            pl.BlockSpec((1, rows, LANES), lambda i, j: (i, j, 0)),
            pl.BlockSpec((rows, d), lambda i, j: (i * nt + j, 0)),
            pl.BlockSpec((1, d), lambda i, j: (0, 0)),
            pl.BlockSpec(memory_space=pl.ANY),
        ],
        out_specs=pl.BlockSpec((rows, d), lambda i, j: (i * nt + j, 0)),
        out_shape=jax.ShapeDtypeStruct((t, d), f32),
        scratch_shapes=[pltpu.VMEM((2, tiles * PACKED_ROWS, d), bf16),
                        pltpu.VMEM((2, tiles * N_EXPERTS, WINDOW, d), bf16),
                        pltpu.SemaphoreType.DMA((2, 1 + tiles * N_EXPERTS))],
        compiler_params=_params(("arbitrary", "arbitrary")),
        name="moe_combine",
    )(tab, tab, lpos, x2d, g, yg)


def _pad_lanes(v):
    return jnp.zeros((1, LANES), f32).at[0, :v.shape[0]].set(v.astype(f32))


def _mixer(x3, norm_mix, w_in, lru_conv_w, lru_conv_b, lru_wa, lru_ba, lru_wi, lru_bi, lru_lambda,
           ssd_conv_w, ssd_conv_b, ssd_a_log, ssd_dt_bias, ssd_d):
    b, s, d = x3.shape
    x2d = x3.reshape(b * s, d)
    pad = jnp.zeros((d, LANES - 2 * SSD_HEADS), bf16)
    w_bf = jnp.concatenate([w_in.astype(bf16), pad], axis=1)
    xc, gate, z, xact, dt = _in_proj(x2d, norm_mix, w_bf, lru_conv_w, lru_conv_b, ssd_conv_w, ssd_conv_b, s)
    xc = xc.reshape(b, s, LRU_WIDTH)
    xact = xact.reshape(b, s, SSD_CONV_CH)
    dt = dt.reshape(b, s, LANES)

    w_gate = jnp.concatenate([lru_wa, lru_wi], axis=-1).astype(bf16)
    hs = _lru_scan(xc, w_gate, lru_ba, lru_bi, lru_lambda)
    alog_pad = _pad_lanes(ssd_a_log.reshape(-1))
    dtb_pad = _pad_lanes(ssd_dt_bias.reshape(-1))
    dskip_pad = _pad_lanes(ssd_d)
    ys = _ssd_scan(xact, dt, alog_pad, dtb_pad, dskip_pad)
    t = b * s
    return (hs[0].reshape(t, -1), hs[1].reshape(t, -1), gate, ys[0].reshape(t, -1), ys[1].reshape(t, -1), z)


def _layer(x3, norm_mix, w_in, lru_conv_w, lru_conv_b, lru_wa, lru_ba, lru_wi, lru_bi, lru_lambda,
           ssd_conv_w, ssd_conv_b, ssd_a_log, ssd_dt_bias, ssd_d, ssd_norm, w_out, norm_ffn, w_router,
           expert_w, layer, final_g):
    b, s, d = x3.shape
    cap = max(1, CAPACITY_FACTOR * s // N_EXPERTS)
    assert s % LANES == 0 and cap % LANES == 0, "sequence length must give 128-aligned expert capacity"
    hf, hb, gate, yf, yb, z = _mixer(x3, norm_mix, w_in, lru_conv_w, lru_conv_b, lru_wa, lru_ba, lru_wi,
                                     lru_bi, lru_lambda, ssd_conv_w, ssd_conv_b, ssd_a_log, ssd_dt_bias, ssd_d)
    wr = jnp.zeros((d, LANES), f32).at[:, :N_EXPERTS].set(w_router)
    wr_hi = wr.astype(bf16)
    wr_lo = (wr - wr_hi.astype(f32)).astype(bf16)
    xn, hn, logits = _mix_out(x3.reshape(b * s, d), hf, hb, gate, yf, yb, z, ssd_norm, w_out.astype(bf16),
                              norm_ffn, wr_hi, wr_lo)
    idx, gates, lpos, tab = _route(logits.reshape(b, s, LANES), cap)
    yg = _moe_ffn(hn, idx, gates, *expert_w, layer)
    return _combine(xn, lpos, tab, yg, b, final_g).reshape(b, s, d)


def kernel(x, norm_mix, w_in, lru_conv_w, lru_conv_b, lru_wa, lru_ba, lru_wi, lru_bi, lru_lambda, ssd_conv_w, ssd_conv_b, ssd_a_log, ssd_dt_bias, ssd_d, ssd_norm, w_out, norm_ffn, w_router, w_gate, w_up, w_down, norm_final):
    depth = norm_mix.shape[0]
    expert_w = (w_gate.astype(bf16), w_up.astype(bf16), w_down.astype(bf16))
    for l in range(depth):
        x = _layer(x, norm_mix[l], w_in[l], lru_conv_w[l], lru_conv_b[l], lru_wa[l], lru_ba[l], lru_wi[l],
                   lru_bi[l], lru_lambda[l], ssd_conv_w[l], ssd_conv_b[l], ssd_a_log[l], ssd_dt_bias[l],
                   ssd_d[l], ssd_norm[l], w_out[l], norm_ffn[l], w_router[l], expert_w, l,
                   norm_final if l == depth - 1 else None)
    return x
```

```python
import functools

import jax
import jax.numpy as jnp
from jax import lax
from jax.experimental import pallas as pl
from jax.experimental.pallas import tpu as pltpu

f32 = jnp.float32
bf16 = jnp.bfloat16
i32 = jnp.int32

D_MODEL = 1024
EPS = 1e-6
CONV_WIDTH = 4
LRU_WIDTH = 1024
LRU_HEADS = 8
LRU_BLOCK = 128
LRU_C = 8.0
SSD_WIDTH = 1024
SSD_HEADDIM = 64
SSD_HEADS = 16
SSD_GROUPS = 4
SSD_STATE = 128
SSD_CHUNK = 128
SSD_GN = SSD_GROUPS * SSD_STATE
SSD_CONV_CH = SSD_WIDTH + 2 * SSD_GN
GROUP_WIDTH = SSD_WIDTH // SSD_GROUPS
N_EXPERTS = 16
CAPACITY_FACTOR = 2
D_FF = 2048

LANES = 128
SUBLANES = 8
HALO = SUBLANES
BF16_ROWS = 2 * SUBLANES
WINDOW = LANES + BF16_ROWS
VMEM_LIMIT = 56 * 1024 * 1024
MIX_ROWS = 256
TAB_START, TAB_NEED, TAB_MOST, TAB_ROWS = 0, 1, 2, 3


def _pick(n, target):
    if n <= target:
        return n
    t = target
    while t >= LANES:
        if n % t == 0:
            return t
        t -= LANES
    return n


def _params(sem, vmem=None):
    return pltpu.CompilerParams(dimension_semantics=sem, vmem_limit_bytes=vmem)


IN_SEGMENTS = (LRU_WIDTH, LRU_WIDTH, SSD_WIDTH, SSD_CONV_CH, LANES)


def _conv_centred(pe, first, last, cw, cb):
    tm = pe.shape[0] - 2 * HALO
    before = jnp.where(first, 0.0, pe[:HALO])
    after = jnp.where(last, 0.0, pe[HALO + tm:])
    c = pe[HALO:HALO + tm]
    r8 = lax.broadcasted_iota(i32, (SUBLANES, pe.shape[1]), 0)
    back1 = pltpu.roll(c, 1, 0)
    fwd1 = pltpu.roll(c, tm - 1, 0)
    fwd2 = pltpu.roll(c, tm - 2, 0)
    back1 = jnp.concatenate([jnp.where(r8 == 0, before[HALO - 1:HALO], back1[:HALO]), back1[HALO:]], axis=0)
    fwd1 = jnp.concatenate([fwd1[:tm - HALO], jnp.where(r8 == HALO - 1, after[0:1], fwd1[tm - HALO:])], axis=0)
    tail2 = jnp.where(r8 == HALO - 2, after[0:1], jnp.where(r8 == HALO - 1, after[1:2], fwd2[tm - HALO:]))
    fwd2 = jnp.concatenate([fwd2[:tm - HALO], tail2], axis=0)
    return cb + back1 * cw[0:1] + c * cw[1:2] + fwd1 * cw[2:3] + fwd2 * cw[3:4]


def _in_proj_body(prev_ref, x_ref, next_ref, g_ref, w_ref, lcw_ref, lcb_ref, scw_ref, scb_ref,
                  xc_ref, gate_ref, z_ref, xa_ref, dt_ref, *, tiles_per_seq):
    pos = lax.rem(pl.program_id(0), tiles_per_seq)
    first = pos == 0
    last = pos == tiles_per_seq - 1
    tm = x_ref.shape[0]
    xe = jnp.concatenate([prev_ref[...], x_ref[...], next_ref[...]], axis=0)
    ms = jnp.mean(xe * xe, axis=-1, keepdims=True)
    hn = (xe * lax.rsqrt(ms + EPS) * g_ref[...]).astype(bf16)
    hc = hn[HALO:HALO + tm]
    o0, o1, o2, o3 = LRU_WIDTH, 2 * LRU_WIDTH, 2 * LRU_WIDTH + SSD_WIDTH, 2 * LRU_WIDTH + SSD_WIDTH + SSD_CONV_CH
    pe = jnp.dot(hn, w_ref[:, :o0], preferred_element_type=f32)
    xc_ref[...] = _conv_centred(pe, first, last, lcw_ref[...], lcb_ref[...]).astype(bf16)
    gate_ref[...] = jnp.dot(hc, w_ref[:, o0:o1], preferred_element_type=f32).astype(bf16)
    z_ref[...] = jnp.dot(hc, w_ref[:, o1:o2], preferred_element_type=f32).astype(bf16)
    pe = jnp.dot(hn, w_ref[:, o2:o3], preferred_element_type=f32)
    xa_ref[...] = jax.nn.silu(_conv_centred(pe, first, last, scw_ref[...], scb_ref[...])).astype(bf16)
    dt_ref[...] = jnp.dot(hc, w_ref[:, o3:], preferred_element_type=f32)


def _in_proj(x2d, g, w_bf16, lru_cw, lru_cb, ssd_cw, ssd_cb, seq):
    t, d = x2d.shape
    n = w_bf16.shape[1]
    tm = _pick(seq, 512)
    per = tm // HALO
    last_blk = t // HALO - 1
    full = lambda shape: pl.BlockSpec(shape, lambda i: (0,) * len(shape))
    return pl.pallas_call(
        functools.partial(_in_proj_body, tiles_per_seq=seq // tm),
        grid=(t // tm,),
        in_specs=[
            pl.BlockSpec((HALO, d), lambda i: (jnp.maximum(i * per - 1, 0), 0)),
            pl.BlockSpec((tm, d), lambda i: (i, 0)),
            pl.BlockSpec((HALO, d), lambda i: (jnp.minimum((i + 1) * per, last_blk), 0)),
            full((1, d)),
            pl.BlockSpec((d, n), lambda i: (0, 0), pipeline_mode=pl.Buffered(1)),
            full((CONV_WIDTH, LRU_WIDTH)), full((1, LRU_WIDTH)),
            full((CONV_WIDTH, SSD_CONV_CH)), full((1, SSD_CONV_CH)),
        ],
        out_specs=[pl.BlockSpec((tm, w), lambda i: (i, 0)) for w in IN_SEGMENTS],
        out_shape=[jax.ShapeDtypeStruct((t, w), dt) for w, dt in zip(IN_SEGMENTS, (bf16, bf16, bf16, bf16, f32))],
        compiler_params=_params(("parallel",), VMEM_LIMIT),
        name="in_proj",
    )(x2d, x2d, x2d, g.reshape(1, d), w_bf16, lru_cw, lru_cb.reshape(1, -1), ssd_cw, ssd_cb.reshape(1, -1))


def _lru_gates(x_ref, perm_ref, w_ref, ba_ref, bi_ref, lam_ref, a_scr, u_scr, d):
    xc = jnp.dot(perm_ref[0], x_ref[0], preferred_element_type=f32).astype(bf16)
    sp = jax.nn.softplus(-lam_ref[d:d + 1, :])
    for h in range(LRU_HEADS):
        sl = slice(h * LRU_BLOCK, (h + 1) * LRU_BLOCK)
        pre = jnp.dot(xc[:, sl], w_ref[d, h], preferred_element_type=f32)
        xh = xc[:, sl].astype(f32)
        r = jax.nn.sigmoid(pre[:, :LRU_BLOCK] + ba_ref[d:d + 1, sl])
        gi = jax.nn.sigmoid(pre[:, LRU_BLOCK:] + bi_ref[d:d + 1, sl])
        log_a = (-LRU_C) * r * sp[:, sl]
        a = jnp.exp(log_a)
        u = jnp.sqrt(1.0 - a * a) * (gi * xh)
        a_scr[d, h] = a
        u_scr[d, h] = u


def _lru_direction(d, perm_ref, o_ref, carry_ref, a_scr, u_scr):
    tc = o_ref.shape[1]
    sub = tc // SUBLANES
    heads = range(LRU_HEADS)

    def block(j):
        jj = (sub - 1 - j) if d else j
        return pl.ds(pl.multiple_of(jj * SUBLANES, SUBLANES), SUBLANES)

    def local(j, state):
        rows = block(j)
        out = []
        for h in heads:
            a = a_scr[d, h, rows, :]
            out += [a * state[2 * h] + u_scr[d, h, rows, :], a * state[2 * h + 1]]
        return tuple(out)

    init = []
    for h in heads:
        init += [jnp.zeros((SUBLANES, LRU_BLOCK), f32), jnp.ones((SUBLANES, LRU_BLOCK), f32)]
    ends = lax.fori_loop(0, sub, local, tuple(init), unroll=True)

    sub_i = lax.broadcasted_iota(i32, (SUBLANES, LRU_BLOCK), 0)
    carries = []
    for h in heads:
        sl = slice(h * LRU_BLOCK, (h + 1) * LRU_BLOCK)
        c = carry_ref[d:d + 1, sl]
        cin = jnp.zeros((SUBLANES, LRU_BLOCK), f32)
        for s in (reversed(range(SUBLANES)) if d else range(SUBLANES)):
            cin = jnp.where(sub_i == s, c, cin)
            c = ends[2 * h][s:s + 1] + ends[2 * h + 1][s:s + 1] * c
        carry_ref[d:d + 1, sl] = c
        carries.append(cin)

    def final(j, state):
        rows = block(j)
        out = []
        for h in heads:
            hcur = a_scr[d, h, rows, :] * state[h] + u_scr[d, h, rows, :]
            u_scr[d, h, rows, :] = hcur
            out.append(hcur)
        return tuple(out)

    lax.fori_loop(0, sub, final, tuple(carries), unroll=True)

    hp = jnp.concatenate([u_scr[d, h].astype(bf16) for h in heads], axis=1)
    o_ref[0] = jnp.dot(perm_ref[1], hp, preferred_element_type=f32).astype(bf16)


def _lru_body(xf_ref, xb_ref, perm_ref, w_ref, ba_ref, bi_ref, lam_ref, of_ref, ob_ref, carry_ref, a_scr, u_scr):
    @pl.when(pl.program_id(1) == 0)
    def _():
        carry_ref[...] = jnp.zeros_like(carry_ref)

    _lru_gates(xf_ref, perm_ref, w_ref, ba_ref, bi_ref, lam_ref, a_scr, u_scr, 0)
    _lru_gates(xb_ref, perm_ref, w_ref, ba_ref, bi_ref, lam_ref, a_scr, u_scr, 1)
    _lru_direction(0, perm_ref, of_ref, carry_ref, a_scr, u_scr)
    _lru_direction(1, perm_ref, ob_ref, carry_ref, a_scr, u_scr)


def _lru_scan(xc, w_gate, ba, bi, lam):
    b, s, w = xc.shape
    tc = _pick(s, 256)
    nc = s // tc
    full = lambda shape: pl.BlockSpec(shape, lambda b_, c_: (0,) * len(shape))
    fwd = pl.BlockSpec((1, tc, w), lambda b_, c_: (b_, c_, 0))
    bwd = pl.BlockSpec((1, tc, w), lambda b_, c_: (b_, nc - 1 - c_, 0))
    p = jnp.arange(tc)
    to_strided = ((p % SUBLANES) * (tc // SUBLANES) + p // SUBLANES)[:, None] == jnp.arange(tc)[None, :]
    perm = jnp.stack([to_strided, to_strided.T]).astype(bf16)
    return pl.pallas_call(
        _lru_body,
        grid=(b, nc),
        in_specs=[fwd, bwd, full((2, tc, tc)), full((2, LRU_HEADS, LRU_BLOCK, 2 * LRU_BLOCK)),
                  full((2, w)), full((2, w)), full((2, w))],
        out_specs=[fwd, bwd],
        out_shape=[jax.ShapeDtypeStruct((b, s, w), bf16)] * 2,
        scratch_shapes=[pltpu.VMEM((2, w), f32), pltpu.VMEM((2, LRU_HEADS, tc, LRU_BLOCK), f32),
                        pltpu.VMEM((2, LRU_HEADS, tc, LRU_BLOCK), f32)],
        compiler_params=_params(("parallel", "arbitrary")),
        name="lru_scan",
    )(xc, xc, perm, w_gate, ba, bi, lam)


def _expand_heads(arr, base):
    rows = arr.shape[0]
    lane = lax.broadcasted_iota(i32, (rows, LANES), 1)
    tiles = []
    for k in range(SSD_HEADS // 2):
        c0 = arr[:, base + 2 * k:base + 2 * k + 1]
        c1 = arr[:, base + 2 * k + 1:base + 2 * k + 2]
        tiles.append(jnp.where(lane < SSD_HEADDIM, c0, c1))
    return jnp.concatenate(tiles, axis=1)


def _cumsum_rows(x, reverse):
    n = x.shape[0]
    r = lax.broadcasted_iota(i32, x.shape, 0)
    k = 1
    while k < n:
        if reverse:
            x = x + jnp.where(r < n - k, pltpu.roll(x, n - k, 0), 0.0)
        else:
            x = x + jnp.where(r >= k, pltpu.roll(x, k, 0), 0.0)
        k *= 2
    return x


def _ssd_body(xf_ref, xb_ref, dtf_ref, dtb_in_ref, alog_ref, dtb_ref, dskip_ref, sele_ref, selc_ref,
              of_ref, ob_ref, state_ref):
    @pl.when(pl.program_id(1) == 0)
    def _():
        state_ref[...] = jnp.zeros_like(state_ref)

    per_step = xf_ref.shape[1] // SSD_CHUNK
    for k in range(per_step):
        rows_f = slice(k * SSD_CHUNK, (k + 1) * SSD_CHUNK)
        rows_b = slice((per_step - 1 - k) * SSD_CHUNK, (per_step - k) * SSD_CHUNK)
        _ssd_dir(xf_ref, dtf_ref, rows_f, alog_ref, dtb_ref, dskip_ref, sele_ref, selc_ref, of_ref,
                 state_ref.at[0], False)
        _ssd_dir(xb_ref, dtb_in_ref, rows_b, alog_ref, dtb_ref, dskip_ref, sele_ref, selc_ref, ob_ref,
                 state_ref.at[1], True)


def _split_bf16(x, parts):
    out = []
    for _ in range(parts):
        p = x.astype(bf16)
        out.append(p)
        x = x - p.astype(f32)
    return jnp.concatenate(out, axis=1)


def _ssd_selectors(chunk):
    lane = jnp.arange(LANES)[:, None]
    spread, cols = [], []
    for d in range(2):
        head = lane - d * SSD_HEADS
        s = (head == jnp.arange(SSD_WIDTH)[None, :] // SSD_HEADDIM).astype(bf16)
        c = (head == jnp.arange(SSD_HEADS * chunk)[None, :] // chunk).astype(bf16)
        spread.append(jnp.concatenate([s, s], axis=0))
        cols.append(jnp.concatenate([c, c], axis=0))
    return jnp.stack(spread), jnp.stack(cols)


def _ssd_dir(x_ref, dt_ref, rows, alog_ref, dtb_ref, dskip_ref, sele_ref, selc_ref, o_ref, state_ref, reverse):
    L = SSD_CHUNK
    base = SSD_HEADS if reverse else 0
    xact = x_ref[0, rows, :]
    xs = xact[:, :SSD_WIDTH].astype(f32)

    dt = jax.nn.softplus(dt_ref[0, rows, :] + dtb_ref[...])
    d_a = dt * (-jnp.exp(alog_ref[...]))
    cum = _cumsum_rows(d_a, reverse)
    edge = cum[0:1] if reverse else cum[L - 1:L]
    cum_t = cum.T

    d = 1 if reverse else 0
    stack = jnp.concatenate([_split_bf16(dt, 2), _split_bf16(jnp.exp(cum), 2), _split_bf16(jnp.exp(edge - cum), 2)],
                            axis=0)
    spread = jnp.dot(stack, sele_ref[d], preferred_element_type=f32)
    dtx = spread[:L] * xs
    e_cum = spread[L:2 * L]
    e_end = spread[2 * L:]
    cum_cols = jnp.dot(_split_bf16(cum, 2), selc_ref[d], preferred_element_type=f32)
    e_edge = _expand_heads(jnp.exp(edge), base)
    w_all = (e_end * dtx).astype(bf16)
    dtx_b = dtx.astype(bf16)

    li = lax.broadcasted_iota(i32, (L, L), 0)
    si = lax.broadcasted_iota(i32, (L, L), 1)
    tri = (si >= li) if reverse else (li >= si)
    lane = lax.broadcasted_iota(i32, (L, LANES), 1)
    lo_half = lane < SSD_HEADDIM

    outs = []
    for g in range(SSD_GROUPS):
        bsl = slice(SSD_WIDTH + g * SSD_STATE, SSD_WIDTH + (g + 1) * SSD_STATE)
        csl = slice(SSD_WIDTH + SSD_GN + g * SSD_STATE, SSD_WIDTH + SSD_GN + (g + 1) * SSD_STATE)
        gsl = slice(g * GROUP_WIDTH, (g + 1) * GROUP_WIDTH)
        bm = xact[:, bsl]
        cm = xact[:, csl]
        cb = lax.dot_general(cm, bm, (((1,), (1,)), ((), ())), preferred_element_type=f32)
        s_old = state_ref[g]
        y_off = jnp.dot(cm, s_old.astype(bf16), preferred_element_type=f32) * e_cum[:, gsl]
        tiles = []
        for p in range(2):
            tsl = slice(g * GROUP_WIDTH + p * LANES, g * GROUP_WIDTH + (p + 1) * LANES)
            ms, rhs = [], []
            for q in range(2):
                h = g * 4 + 2 * p + q
                j = base + h
                seg = cum_cols[:, h * L:(h + 1) * L] - cum_t[j:j + 1, :]
                decay = jnp.exp(jnp.where(tri, seg, -jnp.inf))
                ms.append((cb * decay).astype(bf16))
                rhs.append(jnp.where(lo_half if q == 0 else jnp.logical_not(lo_half), dtx_b[:, tsl], 0.0))
            tiles.append(jnp.dot(jnp.concatenate(ms, axis=1), jnp.concatenate(rhs, axis=0).astype(bf16),
                                 preferred_element_type=f32))
        outs.append(jnp.concatenate(tiles, axis=1) + y_off)
        upd = lax.dot_general(bm, w_all[:, gsl], (((0,), (0,)), ((), ())), preferred_element_type=f32)
        state_ref[g] = s_old * e_edge[:, gsl] + upd
    y = jnp.concatenate(outs, axis=1)
    if not reverse:
        y = y + _expand_heads(dskip_ref[...], 0) * xs
    o_ref[0, rows, :] = y.astype(bf16)


def _ssd_scan(xact, dt_pad, alog_pad, dtb_pad, dskip_pad):
    b, s, w = xact.shape
    L = _pick(s, 4 * SSD_CHUNK)
    assert L % SSD_CHUNK == 0
    nc = s // L
    full = lambda shape: pl.BlockSpec(shape, lambda b_, c_: (0,) * len(shape))
    fwd = lambda width: pl.BlockSpec((1, L, width), lambda b_, c_: (b_, c_, 0))
    bwd = lambda width: pl.BlockSpec((1, L, width), lambda b_, c_: (b_, nc - 1 - c_, 0))
    spread, cols = _ssd_selectors(SSD_CHUNK)
    return pl.pallas_call(
        _ssd_body,
        grid=(b, nc),
        in_specs=[fwd(w), bwd(w), fwd(LANES), bwd(LANES), full((1, LANES)), full((1, LANES)), full((1, LANES)),
                  full(spread.shape), full(cols.shape)],
        out_specs=[fwd(SSD_WIDTH), bwd(SSD_WIDTH)],
        out_shape=[jax.ShapeDtypeStruct((b, s, SSD_WIDTH), bf16)] * 2,
        scratch_shapes=[pltpu.VMEM((2, SSD_GROUPS, SSD_STATE, GROUP_WIDTH), f32)],
        compiler_params=_params(("parallel", "arbitrary")),
        name="ssd_scan",
    )(xact, xact, dt_pad, dt_pad, alog_pad, dtb_pad, dskip_pad, spread, cols)


def _rms(x, g):
    ms = jnp.mean(x * x, axis=-1, keepdims=True)
    return x * lax.rsqrt(ms + EPS) * g


def _mix_out_body(x_ref, hf_ref, hb_ref, gate_ref, yf_ref, yb_ref, z_ref, gn_ref, wo_ref,
                  gf_ref, wrh_ref, wrl_ref, xo_ref, hn_ref, lg_ref):
    sub = min(x_ref.shape[0], MIX_ROWS)
    for i in range(x_ref.shape[0] // sub):
        rows = slice(i * sub, (i + 1) * sub)
        up = lambda ref: ref[rows, :].astype(f32)
        y_lru = (up(hf_ref) + up(hb_ref)) * jax.nn.gelu(up(gate_ref))
        y = (up(yf_ref) + up(yb_ref)) * jax.nn.silu(up(z_ref))
        parts = []
        for g in range(SSD_GROUPS):
            yg = y[:, g * GROUP_WIDTH:(g + 1) * GROUP_WIDTH]
            ms = jnp.mean(yg * yg, axis=-1, keepdims=True)
            parts.append(yg * lax.rsqrt(ms + EPS))
        y_ssd = jnp.concatenate(parts, axis=1) * gn_ref[...]
        mix = jnp.concatenate([y_lru, y_ssd], axis=1).astype(bf16)
        xn = x_ref[rows, :] + jnp.dot(mix, wo_ref[...], preferred_element_type=f32)
        xo_ref[rows, :] = xn
        hn = _rms(xn, gf_ref[...])
        for k in range(hn.shape[1] // LANES):
            hn_ref[pl.ds(i * sub * SUBLANES + k, sub, stride=SUBLANES), :] = hn[:, k * LANES:(k + 1) * LANES]
        h_hi = hn.astype(bf16)
        h_lo = (hn - h_hi.astype(f32)).astype(bf16)
        lg = jnp.dot(h_hi, wrh_ref[...], preferred_element_type=f32)
        lg = lg + jnp.dot(h_lo, wrh_ref[...], preferred_element_type=f32)
        lg = lg + jnp.dot(h_hi, wrl_ref[...], preferred_element_type=f32)
        lg_ref[rows, :] = lg


def _mix_out(x2d, hf, hb, gate, yf, yb, z, ssd_norm, w_out_bf16, norm_ffn, wr_hi, wr_lo):
    t, d = x2d.shape
    tm = _pick(t, 2 * MIX_ROWS)
    row = lambda w: pl.BlockSpec((tm, w), lambda i: (i, 0))
    full = lambda shape: pl.BlockSpec(shape, lambda i: (0,) * len(shape))
    return pl.pallas_call(
        _mix_out_body,
        grid=(t // tm,),
        in_specs=[row(d)] + [row(LRU_WIDTH)] * 3 + [row(SSD_WIDTH)] * 3 + [
            full((1, SSD_WIDTH)), full((LRU_WIDTH + SSD_WIDTH, d)), full((1, d)),
            full((d, LANES)), full((d, LANES)),
        ],
        out_specs=[row(d), pl.BlockSpec((tm * d // LANES, LANES), lambda i: (i, 0)), row(LANES)],
        out_shape=[jax.ShapeDtypeStruct((t, d), f32), jax.ShapeDtypeStruct((t * d // LANES, LANES), f32),
                   jax.ShapeDtypeStruct((t, LANES), f32)],
        compiler_params=_params(("parallel",), VMEM_LIMIT),
        name="mix_out",
    )(x2d, hf, hb, gate, yf, yb, z, ssd_norm.reshape(1, -1), w_out_bf16, norm_ffn.reshape(1, d),
      wr_hi, wr_lo)


def _tile_prefix(tiles, upper_incl, upper_strict, lane):
    incs = [jnp.dot(t.astype(bf16), upper_incl, preferred_element_type=f32) for t in tiles]
    tot = jnp.zeros((N_EXPERTS, LANES), f32)
    for j, inc in enumerate(incs):
        tot = jnp.where(lane == j, inc[:, LANES - 1:LANES], tot)
    start = jnp.dot(tot.astype(bf16), upper_strict, preferred_element_type=f32)
    return incs, tot, start


def _route_body(lg_ref, idx_ref, gate_ref, lpos_ref, tab_ref, lm_scr, vt_scr, list_scr, tabv_scr, tabs_scr, sem,
                *, cap):
    s = lg_ref.shape[1]
    nt = s // LANES
    lt = lg_ref[0].T[:N_EXPERTS]
    mx = jnp.max(lt, axis=0, keepdims=True)
    ex = jnp.exp(lt - mx)
    aff = ex / jnp.sum(ex, axis=0, keepdims=True)
    key = pltpu.bitcast(aff, i32)

    def search(i, thr):
        cand = thr | (jnp.int32(1) << (30 - i))
        cnt = jnp.sum((key >= cand).astype(f32), axis=1, keepdims=True)
        return jnp.where(cnt >= float(cap), cand, thr)

    thr = lax.fori_loop(0, 31, search, jnp.zeros((N_EXPERTS, 1), i32))
    gt = key > thr
    eq = key == thr
    need = float(cap) - jnp.sum(gt.astype(f32), axis=1, keepdims=True)

    sub_i = lax.broadcasted_iota(i32, (LANES, LANES), 0)
    lane_i = lax.broadcasted_iota(i32, (LANES, LANES), 1)
    upper_incl = (sub_i <= lane_i).astype(bf16)
    upper_strict = (sub_i < lane_i).astype(bf16)
    lane_e = lax.broadcasted_iota(i32, (N_EXPERTS, LANES), 1)
    tiles = lambda a: [a[:, j * LANES:(j + 1) * LANES] for j in range(nt)]

    eq_t = tiles(eq.astype(f32))
    incs, _, start = _tile_prefix(eq_t, upper_incl, upper_strict, lane_e)
    sel_t = []
    for j, (gtj, eqj) in enumerate(zip(tiles(gt), eq_t)):
        excl = incs[j] - eqj + start[:, j:j + 1]
        sel_t.append(jnp.logical_or(gtj, jnp.logical_and(eqj > 0.5, excl < need)).astype(f32))

    incs, tot, start = _tile_prefix(sel_t, upper_incl, upper_strict, lane_e)
    aligned = jnp.floor(start * (1.0 / BF16_ROWS)) * float(BF16_ROWS)
    rows_needed = start - aligned + tot
    most = jnp.broadcast_to(jnp.max(rows_needed, axis=0, keepdims=True), rows_needed.shape)
    tab_ref[0, TAB_START] = aligned.astype(i32)
    tab_ref[0, TAB_NEED] = rows_needed.astype(i32)
    tab_ref[0, TAB_MOST] = most.astype(i32)
    tabv_scr[...] = start.astype(i32)
    to_smem = pltpu.make_async_copy(tabv_scr, tabs_scr, sem)
    to_smem.start()

    kind = lax.broadcasted_iota(i32, (SUBLANES, LANES), 0)
    tok_lane = lax.broadcasted_iota(i32, (SUBLANES, LANES), 1).astype(f32)
    fill = jnp.full((LANES - N_EXPERTS, LANES), -1.0, f32)
    zrows = jnp.zeros((LANES - 3 * N_EXPERTS - SUBLANES, LANES), f32)
    for j, a in enumerate(tiles(aff)):
        lm = jnp.where(sel_t[j] > 0.5, incs[j] - sel_t[j], -1.0)
        lm_scr[j] = lm
        shifted = jnp.where(sel_t[j] > 0.5, lm + (start[:, j:j + 1] - aligned[:, j:j + 1]), -1.0)
        lpos_ref[0, pl.ds(j * LANES, LANES), :] = jnp.concatenate([shifted, fill], axis=0).T
        a_hi = a.astype(bf16).astype(f32)
        a_mid = (a - a_hi).astype(bf16).astype(f32)
        a_lo = (a - a_hi - a_mid).astype(bf16).astype(f32)
        tok = jnp.where(kind == 0, float(j), jnp.where(kind == 1, tok_lane, 0.0))
        vt_scr[j] = jnp.concatenate([a_hi, a_mid, a_lo, tok, zrows], axis=0).T.astype(bf16)

    to_smem.wait()
    rank = sub_i.astype(f32)

    def compact(j, carry):
        lm = lm_scr[j]
        vt = vt_scr[j]
        for e in range(N_EXPERTS):
            onehot = (lm[e:e + 1] == rank).astype(bf16)
            packed = jnp.dot(onehot, vt, preferred_element_type=f32)
            list_scr[e, pl.ds(tabs_scr[e, j], LANES), :] = packed
        return carry

    lax.fori_loop(0, nt, compact, 0)

    lane_c = lax.broadcasted_iota(i32, (cap, LANES), 1)
    idx_c = jnp.zeros((cap, LANES), f32)
    gate_c = jnp.zeros((cap, LANES), f32)
    for e in range(N_EXPERTS):
        rows = list_scr[e, 0:cap, :]
        g = rows[:, e:e + 1] + rows[:, N_EXPERTS + e:N_EXPERTS + e + 1] + rows[:, 2 * N_EXPERTS + e:2 * N_EXPERTS + e + 1]
        t = rows[:, 3 * N_EXPERTS:3 * N_EXPERTS + 1] * float(LANES) + rows[:, 3 * N_EXPERTS + 1:3 * N_EXPERTS + 2]
        idx_c = jnp.where(lane_c == e, t, idx_c)
        gate_c = jnp.where(lane_c == e, g, gate_c)
    idx_ref[0] = idx_c.T[:N_EXPERTS].astype(i32)
    gate_ref[0] = gate_c.T[:N_EXPERTS]


def _route(logits, cap):
    b, s, _ = logits.shape
    nt = s // LANES
    assert nt <= LANES
    return pl.pallas_call(
        functools.partial(_route_body, cap=cap),
        grid=(b,),
        in_specs=[pl.BlockSpec((1, s, LANES), lambda i: (i, 0, 0))],
        out_specs=[pl.BlockSpec((1, N_EXPERTS, cap), lambda i: (i, 0, 0)),
                   pl.BlockSpec((1, N_EXPERTS, cap), lambda i: (i, 0, 0)),
                   pl.BlockSpec((1, s, LANES), lambda i: (i, 0, 0)),
                   pl.BlockSpec((1, TAB_ROWS, N_EXPERTS, LANES), lambda i: (i, 0, 0, 0))],
        out_shape=[jax.ShapeDtypeStruct((b, N_EXPERTS, cap), i32),
                   jax.ShapeDtypeStruct((b, N_EXPERTS, cap), f32),
                   jax.ShapeDtypeStruct((b, s, LANES), f32),
                   jax.ShapeDtypeStruct((b, TAB_ROWS, N_EXPERTS, LANES), i32)],
        scratch_shapes=[pltpu.VMEM((nt, N_EXPERTS, LANES), f32),
                        pltpu.VMEM((nt, LANES, LANES), bf16),
                        pltpu.VMEM((N_EXPERTS, cap + LANES, LANES), f32),
                        pltpu.VMEM((N_EXPERTS, LANES), i32),
                        pltpu.SMEM((N_EXPERTS, LANES), i32),
                        pltpu.SemaphoreType.DMA],
        compiler_params=_params(("parallel",), VMEM_LIMIT),
        name="route",
    )(logits)


def _token_copy(src_hbm, tok, dst, r, sem):
    src = src_hbm.at[pl.ds(pl.multiple_of(tok * SUBLANES, SUBLANES), SUBLANES), :]
    return pltpu.make_async_copy(src, dst.at[pl.ds(pl.multiple_of(r * SUBLANES, SUBLANES), SUBLANES), :], sem)


def _moe_body(idx_ref, idxn_ref, gate_ref, wg_ref, wu_ref, wd_ref, hn_hbm, o_ref, xg, sem, *, cap, seq, nb):
    n = pl.program_id(0) * nb + pl.program_id(1)
    total = N_EXPERTS * nb
    slot = lax.rem(n, 2)
    other = 1 - slot

    @pl.when(n == 0)
    def _():
        def start(r, carry):
            _token_copy(hn_hbm, idx_ref[0, 0, 0, r], xg.at[slot], r, sem.at[slot]).start()
            return carry

        lax.fori_loop(0, cap, start, 0, unroll=8)

    def wait_all(buf):
        pltpu.make_async_copy(hn_hbm.at[pl.ds(0, cap * SUBLANES), :], xg.at[buf], sem.at[buf]).wait()

    wait_all(slot)

    base_next = lax.rem(jnp.minimum(n + 1, total - 1), nb) * seq
    tm = min(cap, 256)
    ntile = wg_ref.shape[1] // LANES
    for m in range(cap // tm):
        rows = slice(m * tm, (m + 1) * tm)
        for r in range(m * tm, (m + 1) * tm):
            _token_copy(hn_hbm, base_next + idxn_ref[0, 0, 0, r], xg.at[other], r, sem.at[other]).start()
        xm = jnp.concatenate(
            [xg[slot, pl.ds(m * tm * SUBLANES + k, tm, stride=SUBLANES), :] for k in range(ntile)],
            axis=1).astype(bf16)
        hg = jnp.dot(xm, wg_ref[0], preferred_element_type=f32)
        hu = jnp.dot(xm, wu_ref[0], preferred_element_type=f32)
        hid = (jax.nn.silu(hg) * hu).astype(bf16)
        y = jnp.dot(hid, wd_ref[0], preferred_element_type=f32) * gate_ref[0, 0, rows, :]
        o_ref[0, 0, rows, :] = y.astype(bf16)
    o_ref[0, 0, cap:cap + WINDOW, :] = jnp.zeros((WINDOW, o_ref.shape[-1]), bf16)

    @pl.when(n == total - 1)
    def _():
        wait_all(other)


def _moe_ffn(hn_tiles, idx, gates, wg, wu, wd, layer):
    b, _, cap = idx.shape
    d = wg.shape[2]
    s = hn_tiles.shape[0] * LANES // d // b
    idx4 = idx.reshape(b, N_EXPERTS, 1, cap)
    gates4 = gates.reshape(b, N_EXPERTS, cap, 1)
    def next_step(e, i):
        n1 = jnp.minimum(e * b + i + 1, N_EXPERTS * b - 1)
        return (lax.rem(n1, b), n1 // b, 0, 0)

    return pl.pallas_call(
        functools.partial(_moe_body, cap=cap, seq=s, nb=b),
        grid=(N_EXPERTS, b),
        in_specs=[
            pl.BlockSpec((1, 1, 1, cap), lambda e, i: (i, e, 0, 0), memory_space=pltpu.SMEM),
            pl.BlockSpec((1, 1, 1, cap), next_step, memory_space=pltpu.SMEM),
            pl.BlockSpec((1, 1, cap, 1), lambda e, i: (i, e, 0, 0)),
            pl.BlockSpec((None, 1, d, D_FF), lambda e, i: (layer, e, 0, 0)),
            pl.BlockSpec((None, 1, d, D_FF), lambda e, i: (layer, e, 0, 0)),
            pl.BlockSpec((None, 1, D_FF, d), lambda e, i: (layer, e, 0, 0)),
            pl.BlockSpec(memory_space=pl.ANY),
        ],
        out_specs=pl.BlockSpec((1, 1, cap + WINDOW, d), lambda e, i: (i, e, 0, 0)),
        out_shape=jax.ShapeDtypeStruct((b, N_EXPERTS, cap + WINDOW, d), bf16),
        scratch_shapes=[pltpu.VMEM((2, cap * d // LANES, LANES), f32), pltpu.SemaphoreType.DMA((2,))],
        compiler_params=_params(("arbitrary", "arbitrary"), VMEM_LIMIT),
        name="moe_ffn",
    )(idx4, idx4, gates4, wg, wu, wd, hn_tiles)


SLOT = 64


COMBINE_TILES = 2
PACKED_ROWS = N_EXPERTS * SLOT


def _short_copy(yg_hbm, b, e, start, packed, slot, t, sem):
    src = yg_hbm.at[b, e, pl.ds(pl.multiple_of(start, BF16_ROWS), SLOT), :]
    return pltpu.make_async_copy(src, packed.at[slot, pl.ds(t * PACKED_ROWS + e * SLOT, SLOT), :], sem.at[slot, 0])


def _full_copy(yg_hbm, b, e, start, spill, slot, t, sem):
    src = yg_hbm.at[b, e, pl.ds(pl.multiple_of(start, BF16_ROWS), WINDOW), :]
    k = t * N_EXPERTS + e
    return pltpu.make_async_copy(src, spill.at[slot, k], sem.at[slot, 1 + k])


def _combine_body(tab_ref, tabn_ref, lpos_ref, x_ref, g_ref, yg_hbm, o_ref, packed, spill, sem,
                  *, nb, nt, final_norm):
    b = pl.program_id(0)
    j = pl.program_id(1)
    n = b * nt + j
    slot = lax.rem(n, 2)
    tiles_here = x_ref.shape[0] // LANES

    def fetch(tab, bb, jj, buf):
        for t in range(tiles_here):
            tile = jj * tiles_here + t
            for e in range(N_EXPERTS):
                _short_copy(yg_hbm, bb, e, tab[0, TAB_START, e, tile], packed, buf, t, sem).start()

            @pl.when(tab[0, TAB_MOST, 0, tile] > SLOT)
            def _(t=t, tile=tile):
                for e in range(N_EXPERTS):
                    @pl.when(tab[0, TAB_NEED, e, tile] > SLOT)
                    def _(e=e):
                        _full_copy(yg_hbm, bb, e, tab[0, TAB_START, e, tile], spill, buf, t, sem).start()

    @pl.when(n == 0)
    def _():
        packed[...] = jnp.zeros_like(packed)
        spill[...] = jnp.zeros_like(spill)
        fetch(tab_ref, b, j, slot)

    @pl.when(n + 1 < nb * nt)
    def _():
        wrap = j + 1 == nt
        fetch(tabn_ref, jnp.where(wrap, b + 1, b), jnp.where(wrap, 0, j + 1), 1 - slot)

    pltpu.make_async_copy(packed.at[1 - slot], packed.at[slot], sem.at[slot, 0]).wait()

    lane = lax.broadcasted_iota(i32, (LANES, LANES), 1).astype(f32)
    per_lane_tile = LANES // SLOT
    for t in range(tiles_here):
        tile = j * tiles_here + t
        rows = slice(t * LANES, (t + 1) * LANES)
        lm = lpos_ref[0, rows, :]
        pieces = []
        for lt in range(N_EXPERTS // per_lane_tile):
            hit = None
            for i in range(per_lane_tile):
                e = lt * per_lane_tile + i
                col = jnp.where(tab_ref[0, TAB_NEED, e, tile] <= SLOT, lm[:, e:e + 1], -1.0)
                col = jnp.broadcast_to(jnp.where(col >= 0.0, col + float(i * SLOT), -1.0), (LANES, LANES))
                h = col == lane
                hit = h if hit is None else jnp.logical_or(hit, h)
            pieces.append(hit.astype(bf16))
        place = jnp.concatenate(pieces, axis=1)
        window = packed[slot, t * PACKED_ROWS:(t + 1) * PACKED_ROWS, :]
        o_ref[rows, :] = x_ref[rows, :] + jnp.dot(place, window, preferred_element_type=f32)

        @pl.when(tab_ref[0, TAB_MOST, 0, tile] > SLOT)
        def _(t=t, tile=tile, rows=rows, lm=lm):
            row = lax.broadcasted_iota(i32, (LANES, WINDOW), 1).astype(f32)
            for e in range(N_EXPERTS):
                @pl.when(tab_ref[0, TAB_NEED, e, tile] > SLOT)
                def _(e=e):
                    _full_copy(yg_hbm, b, e, 0, spill, slot, t, sem).wait()
                    own = (lm[:, e:e + 1] == row).astype(bf16)
                    o_ref[rows, :] += jnp.dot(own, spill[slot, t * N_EXPERTS + e], preferred_element_type=f32)

    if final_norm:
        o_ref[...] = _rms(o_ref[...], g_ref[...])


def _combine(x2d, lpos, tab, yg, nb, final_g=None):
    t, d = x2d.shape
    g = jnp.ones((1, d), f32) if final_g is None else final_g.reshape(1, d)
    tiles = COMBINE_TILES if (t // nb // LANES) % COMBINE_TILES == 0 else 1
    rows = tiles * LANES
    nt = t // nb // rows

    def next_batch(i, j):
        return (jnp.minimum(i + (j + 1) // nt, nb - 1), 0, 0, 0)

    return pl.pallas_call(
        functools.partial(_combine_body, nb=nb, nt=nt, final_norm=final_g is not None),
        grid=(nb, nt),
        in_specs=[
            pl.BlockSpec((1, TAB_ROWS, N_EXPERTS, LANES), lambda i, j: (i, 0, 0, 0), memory_space=pltpu.SMEM),
            pl.BlockSpec((1, TAB_ROWS, N_EXPERTS, LANES), next_batch, memory_space=pltpu.SMEM),
            pl.BlockSpec((1, rows, LANES), lambda i, j: (i, j, 0)),
            pl.BlockSpec((rows, d), lambda i, j: (i * nt + j, 0)),
            pl.BlockSpec((1, d), lambda i, j: (0, 0)),
            pl.BlockSpec(memory_space=pl.ANY),
        ],
        out_specs=pl.BlockSpec((rows, d), lambda i, j: (i * nt + j, 0)),
        out_shape=jax.ShapeDtypeStruct((t, d), f32),
        scratch_shapes=[pltpu.VMEM((2, tiles * PACKED_ROWS, d), bf16),
                        pltpu.VMEM((2, tiles * N_EXPERTS, WINDOW, d), bf16),
                        pltpu.SemaphoreType.DMA((2, 1 + tiles * N_EXPERTS))],
        compiler_params=_params(("arbitrary", "arbitrary")),
        name="moe_combine",
    )(tab, tab, lpos, x2d, g, yg)


def _pad_lanes(v):
    return jnp.zeros((1, LANES), f32).at[0, :v.shape[0]].set(v.astype(f32))


def _mixer(x3, norm_mix, w_in, lru_conv_w, lru_conv_b, lru_wa, lru_ba, lru_wi, lru_bi, lru_lambda,
           ssd_conv_w, ssd_conv_b, ssd_a_log, ssd_dt_bias, ssd_d):
    b, s, d = x3.shape
    x2d = x3.reshape(b * s, d)
    pad = jnp.zeros((d, LANES - 2 * SSD_HEADS), bf16)
    w_bf = jnp.concatenate([w_in.astype(bf16), pad], axis=1)
    xc, gate, z, xact, dt = _in_proj(x2d, norm_mix, w_bf, lru_conv_w, lru_conv_b, ssd_conv_w, ssd_conv_b, s)
    xc = xc.reshape(b, s, LRU_WIDTH)
    xact = xact.reshape(b, s, SSD_CONV_CH)
    dt = dt.reshape(b, s, LANES)

    w_gate = jnp.concatenate([lru_wa, lru_wi], axis=-1).astype(bf16)
    hs = _lru_scan(xc, w_gate, lru_ba, lru_bi, lru_lambda)
    alog_pad = _pad_lanes(ssd_a_log.reshape(-1))
    dtb_pad = _pad_lanes(ssd_dt_bias.reshape(-1))
    dskip_pad = _pad_lanes(ssd_d)
    ys = _ssd_scan(xact, dt, alog_pad, dtb_pad, dskip_pad)
    t = b * s
    return (hs[0].reshape(t, -1), hs[1].reshape(t, -1), gate, ys[0].reshape(t, -1), ys[1].reshape(t, -1), z)


def _layer(x3, norm_mix, w_in, lru_conv_w, lru_conv_b, lru_wa, lru_ba, lru_wi, lru_bi, lru_lambda,
           ssd_conv_w, ssd_conv_b, ssd_a_log, ssd_dt_bias, ssd_d, ssd_norm, w_out, norm_ffn, w_router,
           expert_w, layer, final_g):
    b, s, d = x3.shape
    cap = max(1, CAPACITY_FACTOR * s // N_EXPERTS)
    assert s % LANES == 0 and cap % LANES == 0, "sequence length must give 128-aligned expert capacity"
    hf, hb, gate, yf, yb, z = _mixer(x3, norm_mix, w_in, lru_conv_w, lru_conv_b, lru_wa, lru_ba, lru_wi,
                                     lru_bi, lru_lambda, ssd_conv_w, ssd_conv_b, ssd_a_log, ssd_dt_bias, ssd_d)
    wr = jnp.zeros((d, LANES), f32).at[:, :N_EXPERTS].set(w_router)
    wr_hi = wr.astype(bf16)
    wr_lo = (wr - wr_hi.astype(f32)).astype(bf16)
    xn, hn, logits = _mix_out(x3.reshape(b * s, d), hf, hb, gate, yf, yb, z, ssd_norm, w_out.astype(bf16),
                              norm_ffn, wr_hi, wr_lo)
    idx, gates, lpos, tab = _route(logits.reshape(b, s, LANES), cap)
    yg = _moe_ffn(hn, idx, gates, *expert_w, layer)
    return _combine(xn, lpos, tab, yg, b, final_g).reshape(b, s, d)


def kernel(x, norm_mix, w_in, lru_conv_w, lru_conv_b, lru_wa, lru_ba, lru_wi, lru_bi, lru_lambda, ssd_conv_w, ssd_conv_b, ssd_a_log, ssd_dt_bias, ssd_d, ssd_norm, w_out, norm_ffn, w_router, w_gate, w_up, w_down, norm_final):
    depth = norm_mix.shape[0]
    expert_w = (w_gate.astype(bf16), w_up.astype(bf16), w_down.astype(bf16))
    for l in range(depth):
        x = _layer(x, norm_mix[l], w_in[l], lru_conv_w[l], lru_conv_b[l], lru_wa[l], lru_ba[l], lru_wi[l],
                   lru_bi[l], lru_lambda[l], ssd_conv_w[l], ssd_conv_b[l], ssd_a_log[l], ssd_dt_bias[l],
                   ssd_d[l], ssd_norm[l], w_out[l], norm_ffn[l], w_router[l], expert_w, l,
                   norm_final if l == depth - 1 else None)
    return x
```

```python
import functools

import jax
import jax.numpy as jnp
from jax import lax
from jax.experimental import pallas as pl
from jax.experimental.pallas import tpu as pltpu

f32 = jnp.float32
bf16 = jnp.bfloat16
i32 = jnp.int32

D_MODEL = 1024
EPS = 1e-6
CONV_WIDTH = 4
LRU_WIDTH = 1024
LRU_HEADS = 8
LRU_BLOCK = 128
LRU_C = 8.0
SSD_WIDTH = 1024
SSD_HEADDIM = 64
SSD_HEADS = 16
SSD_GROUPS = 4
SSD_STATE = 128
SSD_CHUNK = 128
SSD_GN = SSD_GROUPS * SSD_STATE
SSD_CONV_CH = SSD_WIDTH + 2 * SSD_GN
GROUP_WIDTH = SSD_WIDTH // SSD_GROUPS
N_EXPERTS = 16
CAPACITY_FACTOR = 2
D_FF = 2048

LANES = 128
SUBLANES = 8
HALO = SUBLANES
BF16_ROWS = 2 * SUBLANES
WINDOW = LANES + BF16_ROWS
VMEM_LIMIT = 56 * 1024 * 1024
MIX_ROWS = 256
TAB_START, TAB_NEED, TAB_MOST, TAB_ROWS = 0, 1, 2, 3


def _pick(n, target):
    if n <= target:
        return n
    t = target
    while t >= LANES:
        if n % t == 0:
            return t
        t -= LANES
    return n


def _params(sem, vmem=None):
    return pltpu.CompilerParams(dimension_semantics=sem, vmem_limit_bytes=vmem)


IN_SEGMENTS = (LRU_WIDTH, LRU_WIDTH, SSD_WIDTH, SSD_CONV_CH, LANES)


def _conv_centred(pe, first, last, cw, cb):
    tm = pe.shape[0] - 2 * HALO
    before = jnp.where(first, 0.0, pe[:HALO])
    after = jnp.where(last, 0.0, pe[HALO + tm:])
    c = pe[HALO:HALO + tm]
    r8 = lax.broadcasted_iota(i32, (SUBLANES, pe.shape[1]), 0)
    back1 = pltpu.roll(c, 1, 0)
    fwd1 = pltpu.roll(c, tm - 1, 0)
    fwd2 = pltpu.roll(c, tm - 2, 0)
    back1 = jnp.concatenate([jnp.where(r8 == 0, before[HALO - 1:HALO], back1[:HALO]), back1[HALO:]], axis=0)
    fwd1 = jnp.concatenate([fwd1[:tm - HALO], jnp.where(r8 == HALO - 1, after[0:1], fwd1[tm - HALO:])], axis=0)
    tail2 = jnp.where(r8 == HALO - 2, after[0:1], jnp.where(r8 == HALO - 1, after[1:2], fwd2[tm - HALO:]))
    fwd2 = jnp.concatenate([fwd2[:tm - HALO], tail2], axis=0)
    return cb + back1 * cw[0:1] + c * cw[1:2] + fwd1 * cw[2:3] + fwd2 * cw[3:4]


def _in_proj_body(prev_ref, x_ref, next_ref, g_ref, w_ref, lcw_ref, lcb_ref, scw_ref, scb_ref,
                  xc_ref, gate_ref, z_ref, xa_ref, dt_ref, *, tiles_per_seq):
    pos = lax.rem(pl.program_id(0), tiles_per_seq)
    first = pos == 0
    last = pos == tiles_per_seq - 1
    tm = x_ref.shape[0]
    xe = jnp.concatenate([prev_ref[...], x_ref[...], next_ref[...]], axis=0)
    ms = jnp.mean(xe * xe, axis=-1, keepdims=True)
    hn = (xe * lax.rsqrt(ms + EPS) * g_ref[...]).astype(bf16)
    hc = hn[HALO:HALO + tm]
    o0, o1, o2, o3 = LRU_WIDTH, 2 * LRU_WIDTH, 2 * LRU_WIDTH + SSD_WIDTH, 2 * LRU_WIDTH + SSD_WIDTH + SSD_CONV_CH
    pe = jnp.dot(hn, w_ref[:, :o0], preferred_element_type=f32)
    xc_ref[...] = _conv_centred(pe, first, last, lcw_ref[...], lcb_ref[...]).astype(bf16)
    gate_ref[...] = jnp.dot(hc, w_ref[:, o0:o1], preferred_element_type=f32).astype(bf16)
    z_ref[...] = jnp.dot(hc, w_ref[:, o1:o2], preferred_element_type=f32).astype(bf16)
    pe = jnp.dot(hn, w_ref[:, o2:o3], preferred_element_type=f32)
    xa_ref[...] = jax.nn.silu(_conv_centred(pe, first, last, scw_ref[...], scb_ref[...])).astype(bf16)
    dt_ref[...] = jnp.dot(hc, w_ref[:, o3:], preferred_element_type=f32)


def _in_proj(x2d, g, w_bf16, lru_cw, lru_cb, ssd_cw, ssd_cb, seq):
    t, d = x2d.shape
    n = w_bf16.shape[1]
    tm = _pick(seq, 512)
    per = tm // HALO
    last_blk = t // HALO - 1
    full = lambda shape: pl.BlockSpec(shape, lambda i: (0,) * len(shape))
    return pl.pallas_call(
        functools.partial(_in_proj_body, tiles_per_seq=seq // tm),
        grid=(t // tm,),
        in_specs=[
            pl.BlockSpec((HALO, d), lambda i: (jnp.maximum(i * per - 1, 0), 0)),
            pl.BlockSpec((tm, d), lambda i: (i, 0)),
            pl.BlockSpec((HALO, d), lambda i: (jnp.minimum((i + 1) * per, last_blk), 0)),
            full((1, d)),
            pl.BlockSpec((d, n), lambda i: (0, 0), pipeline_mode=pl.Buffered(1)),
            full((CONV_WIDTH, LRU_WIDTH)), full((1, LRU_WIDTH)),
            full((CONV_WIDTH, SSD_CONV_CH)), full((1, SSD_CONV_CH)),
        ],
        out_specs=[pl.BlockSpec((tm, w), lambda i: (i, 0)) for w in IN_SEGMENTS],
        out_shape=[jax.ShapeDtypeStruct((t, w), dt) for w, dt in zip(IN_SEGMENTS, (bf16, bf16, bf16, bf16, f32))],
        compiler_params=_params(("parallel",), VMEM_LIMIT),
        name="in_proj",
    )(x2d, x2d, x2d, g.reshape(1, d), w_bf16, lru_cw, lru_cb.reshape(1, -1), ssd_cw, ssd_cb.reshape(1, -1))


def _lru_gates(x_ref, perm_ref, w_ref, ba_ref, bi_ref, lam_ref, a_scr, u_scr, d):
    xc = jnp.dot(perm_ref[0], x_ref[0], preferred_element_type=f32).astype(bf16)
    sp = jax.nn.softplus(-lam_ref[d:d + 1, :])
    for h in range(LRU_HEADS):
        sl = slice(h * LRU_BLOCK, (h + 1) * LRU_BLOCK)
        pre = jnp.dot(xc[:, sl], w_ref[d, h], preferred_element_type=f32)
        xh = xc[:, sl].astype(f32)
        r = jax.nn.sigmoid(pre[:, :LRU_BLOCK] + ba_ref[d:d + 1, sl])
        gi = jax.nn.sigmoid(pre[:, LRU_BLOCK:] + bi_ref[d:d + 1, sl])
        log_a = (-LRU_C) * r * sp[:, sl]
        a = jnp.exp(log_a)
        u = jnp.sqrt(1.0 - a * a) * (gi * xh)
        a_scr[d, h] = a
        u_scr[d, h] = u


def _lru_direction(d, perm_ref, o_ref, carry_ref, a_scr, u_scr):
    tc = o_ref.shape[1]
    sub = tc // SUBLANES
    heads = range(LRU_HEADS)

    def block(j):
        jj = (sub - 1 - j) if d else j
        return pl.ds(pl.multiple_of(jj * SUBLANES, SUBLANES), SUBLANES)

    def local(j, state):
        rows = block(j)
        out = []
        for h in heads:
            a = a_scr[d, h, rows, :]
            out += [a * state[2 * h] + u_scr[d, h, rows, :], a * state[2 * h + 1]]
        return tuple(out)

    init = []
    for h in heads:
        init += [jnp.zeros((SUBLANES, LRU_BLOCK), f32), jnp.ones((SUBLANES, LRU_BLOCK), f32)]
    ends = lax.fori_loop(0, sub, local, tuple(init), unroll=True)

    sub_i = lax.broadcasted_iota(i32, (SUBLANES, LRU_BLOCK), 0)
    carries = []
    for h in heads:
        sl = slice(h * LRU_BLOCK, (h + 1) * LRU_BLOCK)
        c = carry_ref[d:d + 1, sl]
        cin = jnp.zeros((SUBLANES, LRU_BLOCK), f32)
        for s in (reversed(range(SUBLANES)) if d else range(SUBLANES)):
            cin = jnp.where(sub_i == s, c, cin)
            c = ends[2 * h][s:s + 1] + ends[2 * h + 1][s:s + 1] * c
        carry_ref[d:d + 1, sl] = c
        carries.append(cin)

    def final(j, state):
        rows = block(j)
        out = []
        for h in heads:
            hcur = a_scr[d, h, rows, :] * state[h] + u_scr[d, h, rows, :]
            u_scr[d, h, rows, :] = hcur
            out.append(hcur)
        return tuple(out)

    lax.fori_loop(0, sub, final, tuple(carries), unroll=True)

    hp = jnp.concatenate([u_scr[d, h].astype(bf16) for h in heads], axis=1)
    o_ref[0] = jnp.dot(perm_ref[1], hp, preferred_element_type=f32).astype(bf16)


def _lru_body(xf_ref, xb_ref, perm_ref, w_ref, ba_ref, bi_ref, lam_ref, of_ref, ob_ref, carry_ref, a_scr, u_scr):
    @pl.when(pl.program_id(1) == 0)
    def _():
        carry_ref[...] = jnp.zeros_like(carry_ref)

    _lru_gates(xf_ref, perm_ref, w_ref, ba_ref, bi_ref, lam_ref, a_scr, u_scr, 0)
    _lru_gates(xb_ref, perm_ref, w_ref, ba_ref, bi_ref, lam_ref, a_scr, u_scr, 1)
    _lru_direction(0, perm_ref, of_ref, carry_ref, a_scr, u_scr)
    _lru_direction(1, perm_ref, ob_ref, carry_ref, a_scr, u_scr)


def _lru_scan(xc, w_gate, ba, bi, lam):
    b, s, w = xc.shape
    tc = _pick(s, 256)
    nc = s // tc
    full = lambda shape: pl.BlockSpec(shape, lambda b_, c_: (0,) * len(shape))
    fwd = pl.BlockSpec((1, tc, w), lambda b_, c_: (b_, c_, 0))
    bwd = pl.BlockSpec((1, tc, w), lambda b_, c_: (b_, nc - 1 - c_, 0))
    p = jnp.arange(tc)
    to_strided = ((p % SUBLANES) * (tc // SUBLANES) + p // SUBLANES)[:, None] == jnp.arange(tc)[None, :]
    perm = jnp.stack([to_strided, to_strided.T]).astype(bf16)
    return pl.pallas_call(
        _lru_body,
        grid=(b, nc),
        in_specs=[fwd, bwd, full((2, tc, tc)), full((2, LRU_HEADS, LRU_BLOCK, 2 * LRU_BLOCK)),
                  full((2, w)), full((2, w)), full((2, w))],
        out_specs=[fwd, bwd],
        out_shape=[jax.ShapeDtypeStruct((b, s, w), bf16)] * 2,
        scratch_shapes=[pltpu.VMEM((2, w), f32), pltpu.VMEM((2, LRU_HEADS, tc, LRU_BLOCK), f32),
                        pltpu.VMEM((2, LRU_HEADS, tc, LRU_BLOCK), f32)],
        compiler_params=_params(("parallel", "arbitrary")),
        name="lru_scan",
    )(xc, xc, perm, w_gate, ba, bi, lam)


def _expand_heads(arr, base):
    rows = arr.shape[0]
    lane = lax.broadcasted_iota(i32, (rows, LANES), 1)
    tiles = []
    for k in range(SSD_HEADS // 2):
        c0 = arr[:, base + 2 * k:base + 2 * k + 1]
        c1 = arr[:, base + 2 * k + 1:base + 2 * k + 2]
        tiles.append(jnp.where(lane < SSD_HEADDIM, c0, c1))
    return jnp.concatenate(tiles, axis=1)


def _cumsum_rows(x, reverse):
    n = x.shape[0]
    r = lax.broadcasted_iota(i32, x.shape, 0)
    k = 1
    while k < n:
        if reverse:
            x = x + jnp.where(r < n - k, pltpu.roll(x, n - k, 0), 0.0)
        else:
            x = x + jnp.where(r >= k, pltpu.roll(x, k, 0), 0.0)
        k *= 2
    return x


def _ssd_body(xf_ref, xb_ref, dtf_ref, dtb_in_ref, alog_ref, dtb_ref, dskip_ref, sele_ref, selc_ref,
              of_ref, ob_ref, state_ref):
    @pl.when(pl.program_id(1) == 0)
    def _():
        state_ref[...] = jnp.zeros_like(state_ref)

    per_step = xf_ref.shape[1] // SSD_CHUNK
    for k in range(per_step):
        rows_f = slice(k * SSD_CHUNK, (k + 1) * SSD_CHUNK)
        rows_b = slice((per_step - 1 - k) * SSD_CHUNK, (per_step - k) * SSD_CHUNK)
        _ssd_dir(xf_ref, dtf_ref, rows_f, alog_ref, dtb_ref, dskip_ref, sele_ref, selc_ref, of_ref,
                 state_ref.at[0], False)
        _ssd_dir(xb_ref, dtb_in_ref, rows_b, alog_ref, dtb_ref, dskip_ref, sele_ref, selc_ref, ob_ref,
                 state_ref.at[1], True)


def _split_bf16(x, parts):
    out = []
    for _ in range(parts):
        p = x.astype(bf16)
        out.append(p)
        x = x - p.astype(f32)
    return jnp.concatenate(out, axis=1)


def _ssd_selectors(chunk):
    lane = jnp.arange(LANES)[:, None]
    spread, cols = [], []
    for d in range(2):
        head = lane - d * SSD_HEADS
        s = (head == jnp.arange(SSD_WIDTH)[None, :] // SSD_HEADDIM).astype(bf16)
        c = (head == jnp.arange(SSD_HEADS * chunk)[None, :] // chunk).astype(bf16)
        spread.append(jnp.concatenate([s, s], axis=0))
        cols.append(jnp.concatenate([c, c], axis=0))
    return jnp.stack(spread), jnp.stack(cols)


def _ssd_dir(x_ref, dt_ref, rows, alog_ref, dtb_ref, dskip_ref, sele_ref, selc_ref, o_ref, state_ref, reverse):
    L = SSD_CHUNK
    base = SSD_HEADS if reverse else 0
    xact = x_ref[0, rows, :]
    xs = xact[:, :SSD_WIDTH].astype(f32)

    dt = jax.nn.softplus(dt_ref[0, rows, :] + dtb_ref[...])
    d_a = dt * (-jnp.exp(alog_ref[...]))
    cum = _cumsum_rows(d_a, reverse)
    edge = cum[0:1] if reverse else cum[L - 1:L]
    cum_t = cum.T

    d = 1 if reverse else 0
    stack = jnp.concatenate([_split_bf16(dt, 2), _split_bf16(jnp.exp(cum), 2), _split_bf16(jnp.exp(edge - cum), 2)],
                            axis=0)
    spread = jnp.dot(stack, sele_ref[d], preferred_element_type=f32)
    dtx = spread[:L] * xs
    e_cum = spread[L:2 * L]
    e_end = spread[2 * L:]
    cum_cols = jnp.dot(_split_bf16(cum, 2), selc_ref[d], preferred_element_type=f32)
    e_edge = _expand_heads(jnp.exp(edge), base)
    w_all = (e_end * dtx).astype(bf16)
    dtx_b = dtx.astype(bf16)

    li = lax.broadcasted_iota(i32, (L, L), 0)
    si = lax.broadcasted_iota(i32, (L, L), 1)
    tri = (si >= li) if reverse else (li >= si)
    lane = lax.broadcasted_iota(i32, (L, LANES), 1)
    lo_half = lane < SSD_HEADDIM

    outs = []
    for g in range(SSD_GROUPS):
        bsl = slice(SSD_WIDTH + g * SSD_STATE, SSD_WIDTH + (g + 1) * SSD_STATE)
        csl = slice(SSD_WIDTH + SSD_GN + g * SSD_STATE, SSD_WIDTH + SSD_GN + (g + 1) * SSD_STATE)
        gsl = slice(g * GROUP_WIDTH, (g + 1) * GROUP_WIDTH)
        bm = xact[:, bsl]
        cm = xact[:, csl]
        cb = lax.dot_general(cm, bm, (((1,), (1,)), ((), ())), preferred_element_type=f32)
        s_old = state_ref[g]
        y_off = jnp.dot(cm, s_old.astype(bf16), preferred_element_type=f32) * e_cum[:, gsl]
        tiles = []
        for p in range(2):
            tsl = slice(g * GROUP_WIDTH + p * LANES, g * GROUP_WIDTH + (p + 1) * LANES)
            ms, rhs = [], []
            for q in range(2):
                h = g * 4 + 2 * p + q
                j = base + h
                seg = cum_cols[:, h * L:(h + 1) * L] - cum_t[j:j + 1, :]
                decay = jnp.exp(jnp.where(tri, seg, -jnp.inf))
                ms.append((cb * decay).astype(bf16))
                rhs.append(jnp.where(lo_half if q == 0 else jnp.logical_not(lo_half), dtx_b[:, tsl], 0.0))
            tiles.append(jnp.dot(jnp.concatenate(ms, axis=1), jnp.concatenate(rhs, axis=0).astype(bf16),
                                 preferred_element_type=f32))
        outs.append(jnp.concatenate(tiles, axis=1) + y_off)
        upd = lax.dot_general(bm, w_all[:, gsl], (((0,), (0,)), ((), ())), preferred_element_type=f32)
        state_ref[g] = s_old * e_edge[:, gsl] + upd
    y = jnp.concatenate(outs, axis=1)
    if not reverse:
        y = y + _expand_heads(dskip_ref[...], 0) * xs
    o_ref[0, rows, :] = y.astype(bf16)


def _ssd_scan(xact, dt_pad, alog_pad, dtb_pad, dskip_pad):
    b, s, w = xact.shape
    L = _pick(s, 4 * SSD_CHUNK)
    assert L % SSD_CHUNK == 0
    nc = s // L
    full = lambda shape: pl.BlockSpec(shape, lambda b_, c_: (0,) * len(shape))
    fwd = lambda width: pl.BlockSpec((1, L, width), lambda b_, c_: (b_, c_, 0))
    bwd = lambda width: pl.BlockSpec((1, L, width), lambda b_, c_: (b_, nc - 1 - c_, 0))
    spread, cols = _ssd_selectors(SSD_CHUNK)
    return pl.pallas_call(
        _ssd_body,
        grid=(b, nc),
        in_specs=[fwd(w), bwd(w), fwd(LANES), bwd(LANES), full((1, LANES)), full((1, LANES)), full((1, LANES)),
                  full(spread.shape), full(cols.shape)],
        out_specs=[fwd(SSD_WIDTH), bwd(SSD_WIDTH)],
        out_shape=[jax.ShapeDtypeStruct((b, s, SSD_WIDTH), bf16)] * 2,
        scratch_shapes=[pltpu.VMEM((2, SSD_GROUPS, SSD_STATE, GROUP_WIDTH), f32)],
        compiler_params=_params(("parallel", "arbitrary")),
        name="ssd_scan",
    )(xact, xact, dt_pad, dt_pad, alog_pad, dtb_pad, dskip_pad, spread, cols)


def _rms(x, g):
    ms = jnp.mean(x * x, axis=-1, keepdims=True)
    return x * lax.rsqrt(ms + EPS) * g


def _mix_out_body(x_ref, hf_ref, hb_ref, gate_ref, yf_ref, yb_ref, z_ref, gn_ref, wo_ref,
                  gf_ref, wrh_ref, wrl_ref, xo_ref, hn_ref, lg_ref):
    sub = min(x_ref.shape[0], MIX_ROWS)
    for i in range(x_ref.shape[0] // sub):
        rows = slice(i * sub, (i + 1) * sub)
        up = lambda ref: ref[rows, :].astype(f32)
        y_lru = (up(hf_ref) + up(hb_ref)) * jax.nn.gelu(up(gate_ref))
        y = (up(yf_ref) + up(yb_ref)) * jax.nn.silu(up(z_ref))
        parts = []
        for g in range(SSD_GROUPS):
            yg = y[:, g * GROUP_WIDTH:(g + 1) * GROUP_WIDTH]
            ms = jnp.mean(yg * yg, axis=-1, keepdims=True)
            parts.append(yg * lax.rsqrt(ms + EPS))
        y_ssd = jnp.concatenate(parts, axis=1) * gn_ref[...]
        mix = jnp.concatenate([y_lru, y_ssd], axis=1).astype(bf16)
        xn = x_ref[rows, :] + jnp.dot(mix, wo_ref[...], preferred_element_type=f32)
        xo_ref[rows, :] = xn
        hn = _rms(xn, gf_ref[...])
        for k in range(hn.shape[1] // LANES):
            hn_ref[pl.ds(i * sub * SUBLANES + k, sub, stride=SUBLANES), :] = hn[:, k * LANES:(k + 1) * LANES]
        h_hi = hn.astype(bf16)
        h_lo = (hn - h_hi.astype(f32)).astype(bf16)
        lg = jnp.dot(h_hi, wrh_ref[...], preferred_element_type=f32)
        lg = lg + jnp.dot(h_lo, wrh_ref[...], preferred_element_type=f32)
        lg = lg + jnp.dot(h_hi, wrl_ref[...], preferred_element_type=f32)
        lg_ref[rows, :] = lg


def _mix_out(x2d, hf, hb, gate, yf, yb, z, ssd_norm, w_out_bf16, norm_ffn, wr_hi, wr_lo):
    t, d = x2d.shape
    tm = _pick(t, 2 * MIX_ROWS)
    row = lambda w: pl.BlockSpec((tm, w), lambda i: (i, 0))
    full = lambda shape: pl.BlockSpec(shape, lambda i: (0,) * len(shape))
    return pl.pallas_call(
        _mix_out_body,
        grid=(t // tm,),
        in_specs=[row(d)] + [row(LRU_WIDTH)] * 3 + [row(SSD_WIDTH)] * 3 + [
            full((1, SSD_WIDTH)), full((LRU_WIDTH + SSD_WIDTH, d)), full((1, d)),
            full((d, LANES)), full((d, LANES)),
        ],
        out_specs=[row(d), pl.BlockSpec((tm * d // LANES, LANES), lambda i: (i, 0)), row(LANES)],
        out_shape=[jax.ShapeDtypeStruct((t, d), f32), jax.ShapeDtypeStruct((t * d // LANES, LANES), f32),
                   jax.ShapeDtypeStruct((t, LANES), f32)],
        compiler_params=_params(("parallel",), VMEM_LIMIT),
        name="mix_out",
    )(x2d, hf, hb, gate, yf, yb, z, ssd_norm.reshape(1, -1), w_out_bf16, norm_ffn.reshape(1, d),
      wr_hi, wr_lo)


def _tile_prefix(tiles, upper_incl, upper_strict, lane):
    incs = [jnp.dot(t.astype(bf16), upper_incl, preferred_element_type=f32) for t in tiles]
    tot = jnp.zeros((N_EXPERTS, LANES), f32)
    for j, inc in enumerate(incs):
        tot = jnp.where(lane == j, inc[:, LANES - 1:LANES], tot)
    start = jnp.dot(tot.astype(bf16), upper_strict, preferred_element_type=f32)
    return incs, tot, start


def _route_body(lg_ref, idx_ref, gate_ref, lpos_ref, tab_ref, lm_scr, vt_scr, list_scr, tabv_scr, tabs_scr, sem,
                *, cap):
    s = lg_ref.shape[1]
    nt = s // LANES
    lt = lg_ref[0].T[:N_EXPERTS]
    mx = jnp.max(lt, axis=0, keepdims=True)
    ex = jnp.exp(lt - mx)
    aff = ex / jnp.sum(ex, axis=0, keepdims=True)
    key = pltpu.bitcast(aff, i32)

    def search(i, thr):
        cand = thr | (jnp.int32(1) << (30 - i))
        cnt = jnp.sum((key >= cand).astype(f32), axis=1, keepdims=True)
        return jnp.where(cnt >= float(cap), cand, thr)

    thr = lax.fori_loop(0, 31, search, jnp.zeros((N_EXPERTS, 1), i32))
    gt = key > thr
    eq = key == thr
    need = float(cap) - jnp.sum(gt.astype(f32), axis=1, keepdims=True)

    sub_i = lax.broadcasted_iota(i32, (LANES, LANES), 0)
    lane_i = lax.broadcasted_iota(i32, (LANES, LANES), 1)
    upper_incl = (sub_i <= lane_i).astype(bf16)
    upper_strict = (sub_i < lane_i).astype(bf16)
    lane_e = lax.broadcasted_iota(i32, (N_EXPERTS, LANES), 1)
    tiles = lambda a: [a[:, j * LANES:(j + 1) * LANES] for j in range(nt)]

    eq_t = tiles(eq.astype(f32))
    incs, _, start = _tile_prefix(eq_t, upper_incl, upper_strict, lane_e)
    sel_t = []
    for j, (gtj, eqj) in enumerate(zip(tiles(gt), eq_t)):
        excl = incs[j] - eqj + start[:, j:j + 1]
        sel_t.append(jnp.logical_or(gtj, jnp.logical_and(eqj > 0.5, excl < need)).astype(f32))

    incs, tot, start = _tile_prefix(sel_t, upper_incl, upper_strict, lane_e)
    aligned = jnp.floor(start * (1.0 / BF16_ROWS)) * float(BF16_ROWS)
    rows_needed = start - aligned + tot
    most = jnp.broadcast_to(jnp.max(rows_needed, axis=0, keepdims=True), rows_needed.shape)
    tab_ref[0, TAB_START] = aligned.astype(i32)
    tab_ref[0, TAB_NEED] = rows_needed.astype(i32)
    tab_ref[0, TAB_MOST] = most.astype(i32)
    tabv_scr[...] = start.astype(i32)
    to_smem = pltpu.make_async_copy(tabv_scr, tabs_scr, sem)
    to_smem.start()

    kind = lax.broadcasted_iota(i32, (SUBLANES, LANES), 0)
    tok_lane = lax.broadcasted_iota(i32, (SUBLANES, LANES), 1).astype(f32)
    fill = jnp.full((LANES - N_EXPERTS, LANES), -1.0, f32)
    zrows = jnp.zeros((LANES - 3 * N_EXPERTS - SUBLANES, LANES), f32)
    for j, a in enumerate(tiles(aff)):
        lm = jnp.where(sel_t[j] > 0.5, incs[j] - sel_t[j], -1.0)
        lm_scr[j] = lm
        shifted = jnp.where(sel_t[j] > 0.5, lm + (start[:, j:j + 1] - aligned[:, j:j + 1]), -1.0)
        lpos_ref[0, pl.ds(j * LANES, LANES), :] = jnp.concatenate([shifted, fill], axis=0).T
        a_hi = a.astype(bf16).astype(f32)
        a_mid = (a - a_hi).astype(bf16).astype(f32)
        a_lo = (a - a_hi - a_mid).astype(bf16).astype(f32)
        tok = jnp.where(kind == 0, float(j), jnp.where(kind == 1, tok_lane, 0.0))
        vt_scr[j] = jnp.concatenate([a_hi, a_mid, a_lo, tok, zrows], axis=0).T.astype(bf16)

    to_smem.wait()
    rank = sub_i.astype(f32)

    def compact(j, carry):
        lm = lm_scr[j]
        vt = vt_scr[j]
        for e in range(N_EXPERTS):
            onehot = (lm[e:e + 1] == rank).astype(bf16)
            packed = jnp.dot(onehot, vt, preferred_element_type=f32)
            list_scr[e, pl.ds(tabs_scr[e, j], LANES), :] = packed
        return carry

    lax.fori_loop(0, nt, compact, 0)

    lane_c = lax.broadcasted_iota(i32, (cap, LANES), 1)
    idx_c = jnp.zeros((cap, LANES), f32)
    gate_c = jnp.zeros((cap, LANES), f32)
    for e in range(N_EXPERTS):
        rows = list_scr[e, 0:cap, :]
        g = rows[:, e:e + 1] + rows[:, N_EXPERTS + e:N_EXPERTS + e + 1] + rows[:, 2 * N_EXPERTS + e:2 * N_EXPERTS + e + 1]
        t = rows[:, 3 * N_EXPERTS:3 * N_EXPERTS + 1] * float(LANES) + rows[:, 3 * N_EXPERTS + 1:3 * N_EXPERTS + 2]
        idx_c = jnp.where(lane_c == e, t, idx_c)
        gate_c = jnp.where(lane_c == e, g, gate_c)
    idx_ref[0] = idx_c.T[:N_EXPERTS].astype(i32)
    gate_ref[0] = gate_c.T[:N_EXPERTS]


def _route(logits, cap):
    b, s, _ = logits.shape
    nt = s // LANES
    assert nt <= LANES
    return pl.pallas_call(
        functools.partial(_route_body, cap=cap),
        grid=(b,),
        in_specs=[pl.BlockSpec((1, s, LANES), lambda i: (i, 0, 0))],
        out_specs=[pl.BlockSpec((1, N_EXPERTS, cap), lambda i: (i, 0, 0)),
                   pl.BlockSpec((1, N_EXPERTS, cap), lambda i: (i, 0, 0)),
                   pl.BlockSpec((1, s, LANES), lambda i: (i, 0, 0)),
                   pl.BlockSpec((1, TAB_ROWS, N_EXPERTS, LANES), lambda i: (i, 0, 0, 0))],
        out_shape=[jax.ShapeDtypeStruct((b, N_EXPERTS, cap), i32),
                   jax.ShapeDtypeStruct((b, N_EXPERTS, cap), f32),
                   jax.ShapeDtypeStruct((b, s, LANES), f32),
                   jax.ShapeDtypeStruct((b, TAB_ROWS, N_EXPERTS, LANES), i32)],
        scratch_shapes=[pltpu.VMEM((nt, N_EXPERTS, LANES), f32),
                        pltpu.VMEM((nt, LANES, LANES), bf16),
                        pltpu.VMEM((N_EXPERTS, cap + LANES, LANES), f32),
                        pltpu.VMEM((N_EXPERTS, LANES), i32),
                        pltpu.SMEM((N_EXPERTS, LANES), i32),
                        pltpu.SemaphoreType.DMA],
        compiler_params=_params(("parallel",), VMEM_LIMIT),
        name="route",
    )(logits)


def _token_copy(src_hbm, tok, dst, r, sem):
    src = src_hbm.at[pl.ds(pl.multiple_of(tok * SUBLANES, SUBLANES), SUBLANES), :]
    return pltpu.make_async_copy(src, dst.at[pl.ds(pl.multiple_of(r * SUBLANES, SUBLANES), SUBLANES), :], sem)


def _moe_body(idx_ref, idxn_ref, gate_ref, wg_ref, wu_ref, wd_ref, hn_hbm, o_ref, xg, sem, *, cap, seq, nb):
    n = pl.program_id(0) * nb + pl.program_id(1)
    total = N_EXPERTS * nb
    slot = lax.rem(n, 2)
    other = 1 - slot

    @pl.when(n == 0)
    def _():
        def start(r, carry):
            _token_copy(hn_hbm, idx_ref[0, 0, 0, r], xg.at[slot], r, sem.at[slot]).start()
            return carry

        lax.fori_loop(0, cap, start, 0, unroll=8)

    def wait_all(buf):
        pltpu.make_async_copy(hn_hbm.at[pl.ds(0, cap * SUBLANES), :], xg.at[buf], sem.at[buf]).wait()

    wait_all(slot)

    base_next = lax.rem(jnp.minimum(n + 1, total - 1), nb) * seq
    tm = min(cap, 256)
    ntile = wg_ref.shape[1] // LANES
    for m in range(cap // tm):
        rows = slice(m * tm, (m + 1) * tm)
        for r in range(m * tm, (m + 1) * tm):
            _token_copy(hn_hbm, base_next + idxn_ref[0, 0, 0, r], xg.at[other], r, sem.at[other]).start(priority=r % 2)
        xm = jnp.concatenate(
            [xg[slot, pl.ds(m * tm * SUBLANES + k, tm, stride=SUBLANES), :] for k in range(ntile)],
            axis=1).astype(bf16)
        hg = jnp.dot(xm, wg_ref[0], preferred_element_type=f32)
        hu = jnp.dot(xm, wu_ref[0], preferred_element_type=f32)
        hid = (jax.nn.silu(hg) * hu).astype(bf16)
        y = jnp.dot(hid, wd_ref[0], preferred_element_type=f32) * gate_ref[0, 0, rows, :]
        o_ref[0, 0, rows, :] = y.astype(bf16)
    o_ref[0, 0, cap:cap + WINDOW, :] = jnp.zeros((WINDOW, o_ref.shape[-1]), bf16)

    @pl.when(n == total - 1)
    def _():
        wait_all(other)


def _moe_ffn(hn_tiles, idx, gates, wg, wu, wd, layer):
    b, _, cap = idx.shape
    d = wg.shape[2]
    s = hn_tiles.shape[0] * LANES // d // b
    idx4 = idx.reshape(b, N_EXPERTS, 1, cap)
    gates4 = gates.reshape(b, N_EXPERTS, cap, 1)
    def next_step(e, i):
        n1 = jnp.minimum(e * b + i + 1, N_EXPERTS * b - 1)
        return (lax.rem(n1, b), n1 // b, 0, 0)

    return pl.pallas_call(
        functools.partial(_moe_body, cap=cap, seq=s, nb=b),
        grid=(N_EXPERTS, b),
        in_specs=[
            pl.BlockSpec((1, 1, 1, cap), lambda e, i: (i, e, 0, 0), memory_space=pltpu.SMEM),
            pl.BlockSpec((1, 1, 1, cap), next_step, memory_space=pltpu.SMEM),
            pl.BlockSpec((1, 1, cap, 1), lambda e, i: (i, e, 0, 0)),
            pl.BlockSpec((None, 1, d, D_FF), lambda e, i: (layer, e, 0, 0)),
            pl.BlockSpec((None, 1, d, D_FF), lambda e, i: (layer, e, 0, 0)),
            pl.BlockSpec((None, 1, D_FF, d), lambda e, i: (layer, e, 0, 0)),
            pl.BlockSpec(memory_space=pl.ANY),
        ],
        out_specs=pl.BlockSpec((1, 1, cap + WINDOW, d), lambda e, i: (i, e, 0, 0)),
        out_shape=jax.ShapeDtypeStruct((b, N_EXPERTS, cap + WINDOW, d), bf16),
        scratch_shapes=[pltpu.VMEM((2, cap * d // LANES, LANES), f32), pltpu.SemaphoreType.DMA((2,))],
        compiler_params=_params(("arbitrary", "arbitrary"), VMEM_LIMIT),
        name="moe_ffn",
    )(idx4, idx4, gates4, wg, wu, wd, hn_tiles)


SLOT = 64


COMBINE_TILES = 2
PACKED_ROWS = N_EXPERTS * SLOT


def _short_copy(yg_hbm, b, e, start, packed, slot, t, sem):
    src = yg_hbm.at[b, e, pl.ds(pl.multiple_of(start, BF16_ROWS), SLOT), :]
    return pltpu.make_async_copy(src, packed.at[slot, pl.ds(t * PACKED_ROWS + e * SLOT, SLOT), :], sem.at[slot, 0])


def _full_copy(yg_hbm, b, e, start, spill, slot, t, sem):
    src = yg_hbm.at[b, e, pl.ds(pl.multiple_of(start, BF16_ROWS), WINDOW), :]
    k = t * N_EXPERTS + e
    return pltpu.make_async_copy(src, spill.at[slot, k], sem.at[slot, 1 + k])


def _combine_body(tab_ref, tabn_ref, lpos_ref, x_ref, g_ref, yg_hbm, o_ref, packed, spill, sem,
                  *, nb, nt, final_norm):
    b = pl.program_id(0)
    j = pl.program_id(1)
    n = b * nt + j
    slot = lax.rem(n, 2)
    tiles_here = x_ref.shape[0] // LANES

    def fetch(tab, bb, jj, buf):
        for t in range(tiles_here):
            tile = jj * tiles_here + t
            for e in range(N_EXPERTS):
                _short_copy(yg_hbm, bb, e, tab[0, TAB_START, e, tile], packed, buf, t, sem).start()

            @pl.when(tab[0, TAB_MOST, 0, tile] > SLOT)
            def _(t=t, tile=tile):
                for e in range(N_EXPERTS):
                    @pl.when(tab[0, TAB_NEED, e, tile] > SLOT)
                    def _(e=e):
                        _full_copy(yg_hbm, bb, e, tab[0, TAB_START, e, tile], spill, buf, t, sem).start()

    @pl.when(n == 0)
    def _():
        packed[...] = jnp.zeros_like(packed)
        spill[...] = jnp.zeros_like(spill)
        fetch(tab_ref, b, j, slot)

    @pl.when(n + 1 < nb * nt)
    def _():
        wrap = j + 1 == nt
        fetch(tabn_ref, jnp.where(wrap, b + 1, b), jnp.where(wrap, 0, j + 1), 1 - slot)

    pltpu.make_async_copy(packed.at[1 - slot], packed.at[slot], sem.at[slot, 0]).wait()

    lane = lax.broadcasted_iota(i32, (LANES, LANES), 1).astype(f32)
    per_lane_tile = LANES // SLOT
    for t in range(tiles_here):
        tile = j * tiles_here + t
        rows = slice(t * LANES, (t + 1) * LANES)
        lm = lpos_ref[0, rows, :]
        pieces = []
        for lt in range(N_EXPERTS // per_lane_tile):
            hit = None
            for i in range(per_lane_tile):
                e = lt * per_lane_tile + i
                col = jnp.where(tab_ref[0, TAB_NEED, e, tile] <= SLOT, lm[:, e:e + 1], -1.0)
                col = jnp.broadcast_to(jnp.where(col >= 0.0, col + float(i * SLOT), -1.0), (LANES, LANES))
                h = col == lane
                hit = h if hit is None else jnp.logical_or(hit, h)
            pieces.append(hit.astype(bf16))
        place = jnp.concatenate(pieces, axis=1)
        window = packed[slot, t * PACKED_ROWS:(t + 1) * PACKED_ROWS, :]
        o_ref[rows, :] = x_ref[rows, :] + jnp.dot(place, window, preferred_element_type=f32)

        @pl.when(tab_ref[0, TAB_MOST, 0, tile] > SLOT)
        def _(t=t, tile=tile, rows=rows, lm=lm):
            row = lax.broadcasted_iota(i32, (LANES, WINDOW), 1).astype(f32)
            for e in range(N_EXPERTS):
                @pl.when(tab_ref[0, TAB_NEED, e, tile] > SLOT)
                def _(e=e):
                    _full_copy(yg_hbm, b, e, 0, spill, slot, t, sem).wait()
                    own = (lm[:, e:e + 1] == row).astype(bf16)
                    o_ref[rows, :] += jnp.dot(own, spill[slot, t * N_EXPERTS + e], preferred_element_type=f32)

    if final_norm:
        o_ref[...] = _rms(o_ref[...], g_ref[...])


def _combine(x2d, lpos, tab, yg, nb, final_g=None):
    t, d = x2d.shape
    g = jnp.ones((1, d), f32) if final_g is None else final_g.reshape(1, d)
    tiles = COMBINE_TILES if (t // nb // LANES) % COMBINE_TILES == 0 else 1
    rows = tiles * LANES
    nt = t // nb // rows

    def next_batch(i, j):
        return (jnp.minimum(i + (j + 1) // nt, nb - 1), 0, 0, 0)

    return pl.pallas_call(
        functools.partial(_combine_body, nb=nb, nt=nt, final_norm=final_g is not None),
        grid=(nb, nt),
        in_specs=[
            pl.BlockSpec((1, TAB_ROWS, N_EXPERTS, LANES), lambda i, j: (i, 0, 0, 0), memory_space=pltpu.SMEM),
            pl.BlockSpec((1, TAB_ROWS, N_EXPERTS, LANES), next_batch, memory_space=pltpu.SMEM),
            pl.BlockSpec((1, rows, LANES), lambda i, j: (i, j, 0)),
            pl.BlockSpec((rows, d), lambda i, j: (i * nt + j, 0)),
            pl.BlockSpec((1, d), lambda i, j: (0, 0)),
            pl.BlockSpec(memory_space=pl.ANY),
        ],
        out_specs=pl.BlockSpec((rows, d), lambda i, j: (i * nt + j, 0)),
        out_shape=jax.ShapeDtypeStruct((t, d), f32),
        scratch_shapes=[pltpu.VMEM((2, tiles * PACKED_ROWS, d), bf16),
                        pltpu.VMEM((2, tiles * N_EXPERTS, WINDOW, d), bf16),
                        pltpu.SemaphoreType.DMA((2, 1 + tiles * N_EXPERTS))],
        compiler_params=_params(("arbitrary", "arbitrary")),
        name="moe_combine",
    )(tab, tab, lpos, x2d, g, yg)


def _pad_lanes(v):
    return jnp.zeros((1, LANES), f32).at[0, :v.shape[0]].set(v.astype(f32))


def _mixer(x3, norm_mix, w_in, lru_conv_w, lru_conv_b, lru_wa, lru_ba, lru_wi, lru_bi, lru_lambda,
           ssd_conv_w, ssd_conv_b, ssd_a_log, ssd_dt_bias, ssd_d):
    b, s, d = x3.shape
    x2d = x3.reshape(b * s, d)
    pad = jnp.zeros((d, LANES - 2 * SSD_HEADS), bf16)
    w_bf = jnp.concatenate([w_in.astype(bf16), pad], axis=1)
    xc, gate, z, xact, dt = _in_proj(x2d, norm_mix, w_bf, lru_conv_w, lru_conv_b, ssd_conv_w, ssd_conv_b, s)
    xc = xc.reshape(b, s, LRU_WIDTH)
    xact = xact.reshape(b, s, SSD_CONV_CH)
    dt = dt.reshape(b, s, LANES)

    w_gate = jnp.concatenate([lru_wa, lru_wi], axis=-1).astype(bf16)
    hs = _lru_scan(xc, w_gate, lru_ba, lru_bi, lru_lambda)
    alog_pad = _pad_lanes(ssd_a_log.reshape(-1))
    dtb_pad = _pad_lanes(ssd_dt_bias.reshape(-1))
    dskip_pad = _pad_lanes(ssd_d)
    ys = _ssd_scan(xact, dt, alog_pad, dtb_pad, dskip_pad)
    t = b * s
    return (hs[0].reshape(t, -1), hs[1].reshape(t, -1), gate, ys[0].reshape(t, -1), ys[1].reshape(t, -1), z)


def _layer(x3, norm_mix, w_in, lru_conv_w, lru_conv_b, lru_wa, lru_ba, lru_wi, lru_bi, lru_lambda,
           ssd_conv_w, ssd_conv_b, ssd_a_log, ssd_dt_bias, ssd_d, ssd_norm, w_out, norm_ffn, w_router,
           expert_w, layer, final_g):
    b, s, d = x3.shape
    cap = max(1, CAPACITY_FACTOR * s // N_EXPERTS)
    assert s % LANES == 0 and cap % LANES == 0, "sequence length must give 128-aligned expert capacity"
    hf, hb, gate, yf, yb, z = _mixer(x3, norm_mix, w_in, lru_conv_w, lru_conv_b, lru_wa, lru_ba, lru_wi,
                                     lru_bi, lru_lambda, ssd_conv_w, ssd_conv_b, ssd_a_log, ssd_dt_bias, ssd_d)
    wr = jnp.zeros((d, LANES), f32).at[:, :N_EXPERTS].set(w_router)
    wr_hi = wr.astype(bf16)
    wr_lo = (wr - wr_hi.astype(f32)).astype(bf16)
    xn, hn, logits = _mix_out(x3.reshape(b * s, d), hf, hb, gate, yf, yb, z, ssd_norm, w_out.astype(bf16),
                              norm_ffn, wr_hi, wr_lo)
    idx, gates, lpos, tab = _route(logits.reshape(b, s, LANES), cap)
    yg = _moe_ffn(hn, idx, gates, *expert_w, layer)
    return _combine(xn, lpos, tab, yg, b, final_g).reshape(b, s, d)


def kernel(x, norm_mix, w_in, lru_conv_w, lru_conv_b, lru_wa, lru_ba, lru_wi, lru_bi, lru_lambda, ssd_conv_w, ssd_conv_b, ssd_a_log, ssd_dt_bias, ssd_d, ssd_norm, w_out, norm_ffn, w_router, w_gate, w_up, w_down, norm_final):
    depth = norm_mix.shape[0]
    expert_w = (w_gate.astype(bf16), w_up.astype(bf16), w_down.astype(bf16))
    for l in range(depth):
        x = _layer(x, norm_mix[l], w_in[l], lru_conv_w[l], lru_conv_b[l], lru_wa[l], lru_ba[l], lru_wi[l],
                   lru_bi[l], lru_lambda[l], ssd_conv_w[l], ssd_conv_b[l], ssd_a_log[l], ssd_dt_bias[l],
                   ssd_d[l], ssd_norm[l], w_out[l], norm_ffn[l], w_router[l], expert_w, l,
                   norm_final if l == depth - 1 else None)
    return x
```
